```python
import math
import jax, jax.numpy as jnp
from jax import lax
import numpy as np

D_MODEL = 1024
BATCH = 2
SEQ = 8192
DEPTH = 2

HEAD_DIM = 64
H_MLA = 8
H_MOBA = 8
MLA_WIDTH = H_MLA * HEAD_DIM
MOBA_WIDTH = H_MOBA * HEAD_DIM
D_MIX = MLA_WIDTH + MOBA_WIDTH
Q_LORA = 256
KV_LORA = 128
NOPE_DIM = 64
ROPE_DIM = 32
V_DIM = HEAD_DIM
QK_DIM = NOPE_DIM + ROPE_DIM
ROPE_THETA = 10000.0
MOBA_BLOCK = 256
MOBA_TOPK = 3
Q_BLOCK = 128
MOBA_Q_CHUNK = 64
N_BUCKETS = 32
REL_MAX_DIST = 4096
EPS = 1e-6

IN_SPLITS = (Q_LORA, KV_LORA, ROPE_DIM, MLA_WIDTH, MOBA_WIDTH, MOBA_WIDTH, MOBA_WIDTH, MOBA_WIDTH)
IN_OFFSETS = (Q_LORA,
              Q_LORA + KV_LORA,
              Q_LORA + KV_LORA + ROPE_DIM,
              Q_LORA + KV_LORA + ROPE_DIM + MLA_WIDTH,
              Q_LORA + KV_LORA + ROPE_DIM + MLA_WIDTH + MOBA_WIDTH,
              Q_LORA + KV_LORA + ROPE_DIM + MLA_WIDTH + 2 * MOBA_WIDTH,
              Q_LORA + KV_LORA + ROPE_DIM + MLA_WIDTH + 3 * MOBA_WIDTH)
D_IN = Q_LORA + KV_LORA + ROPE_DIM + MLA_WIDTH + 4 * MOBA_WIDTH

kernel_name = "hybrid_mla_moba_adaln_block"


def rms_norm(x, g):
    xf = x.astype(jnp.float32)
    y = xf * lax.rsqrt(jnp.mean(xf * xf, axis=-1, keepdims=True) + EPS)
    return (y * g.astype(jnp.float32)).astype(x.dtype)


def apply_rope(x, positions):
    half = ROPE_DIM // 2
    inv_freq = ROPE_THETA ** (-jnp.arange(0, half, dtype=jnp.float32) / half)
    ang = positions.astype(jnp.float32)[..., None] * inv_freq
    cos = jnp.cos(ang)[:, :, None, :]
    sin = jnp.sin(ang)[:, :, None, :]
    xf = x.astype(jnp.float32)
    x1, x2 = xf[..., :half], xf[..., half:]
    out = jnp.concatenate([x1 * cos - x2 * sin, x1 * sin + x2 * cos], axis=-1)
    return out.astype(x.dtype)


def rel_bucket(dist):
    n = jnp.maximum(dist, 0)
    max_exact = N_BUCKETS // 2
    nf = jnp.maximum(n, 1).astype(jnp.float32)
    large = max_exact + (jnp.log(nf / max_exact) / math.log(REL_MAX_DIST / max_exact)
                         * (N_BUCKETS - max_exact)).astype(jnp.int32)
    large = jnp.minimum(large, N_BUCKETS - 1)
    return jnp.where(n < max_exact, n, large)


def mla_attention(c_q, c_kv, k_rope, positions, q_norm_g, w_uq, kv_norm_g, w_ukv, q_g, k_g):
    B, S, _ = c_q.shape
    q = (rms_norm(c_q, q_norm_g) @ w_uq).reshape(B, S, H_MLA, QK_DIM)
    kv = (rms_norm(c_kv, kv_norm_g) @ w_ukv).reshape(B, S, H_MLA, NOPE_DIM + V_DIM)
    k_nope, v = kv[..., :NOPE_DIM], kv[..., NOPE_DIM:]
    k = jnp.concatenate(
        [k_nope, jnp.broadcast_to(k_rope[:, :, None, :], (B, S, H_MLA, ROPE_DIM))], axis=-1)
    q = rms_norm(q, q_g)
    k = rms_norm(k, k_g)
    q = jnp.concatenate([q[..., :NOPE_DIM], apply_rope(q[..., NOPE_DIM:], positions)], axis=-1)
    k = jnp.concatenate([k[..., :NOPE_DIM], apply_rope(k[..., NOPE_DIM:], positions)], axis=-1)
    n_qb = S // Q_BLOCK
    qb = q.reshape(B, n_qb, Q_BLOCK, H_MLA, QK_DIM).transpose(1, 0, 2, 3, 4)
    kpos = jnp.arange(S)
    scale = QK_DIM ** -0.5

    def block(args):
        q_blk, i = args
        s = jnp.einsum('bqhd,bkhd->bhqk', q_blk, k,
                       preferred_element_type=jnp.float32) * scale
        qpos = i * Q_BLOCK + jnp.arange(Q_BLOCK)
        s = jnp.where(kpos[None, :] <= qpos[:, None], s, -jnp.inf)
        p = jax.nn.softmax(s, axis=-1)
        return jnp.einsum('bhqk,bkhd->bqhd', p.astype(v.dtype), v)

    o = lax.map(block, (qb, jnp.arange(n_qb)))
    return o.transpose(1, 0, 2, 3, 4).reshape(B, S, MLA_WIDTH)


def moba_attention(q, k, v, q_g, k_g, rel_bias):
    B, S, _ = q.shape
    q = rms_norm(q.reshape(B, S, H_MOBA, HEAD_DIM), q_g).transpose(0, 2, 1, 3)
    k = rms_norm(k.reshape(B, S, H_MOBA, HEAD_DIM), k_g).transpose(0, 2, 1, 3)
    v = v.reshape(B, S, H_MOBA, HEAD_DIM).transpose(0, 2, 1, 3)
    nb = -(-S // MOBA_BLOCK)
    pad = nb * MOBA_BLOCK - S
    kp = jnp.pad(k, ((0, 0), (0, 0), (0, pad), (0, 0))).reshape(B, H_MOBA, nb, MOBA_BLOCK, HEAD_DIM)
    vp = jnp.pad(v, ((0, 0), (0, 0), (0, pad), (0, 0))).reshape(B, H_MOBA, nb, MOBA_BLOCK, HEAD_DIM)
    blk_ids = jnp.arange(nb)
    counts = jnp.minimum(S - blk_ids * MOBA_BLOCK, MOBA_BLOCK).astype(jnp.float32)
    k_mean = kp.astype(jnp.float32).sum(axis=3) / counts[None, None, :, None]
    topk = min(MOBA_TOPK, nb)
    n_qc = S // MOBA_Q_CHUNK
    qc = q.reshape(B, H_MOBA, n_qc, MOBA_Q_CHUNK, HEAD_DIM).transpose(2, 0, 1, 3, 4)
    scale = HEAD_DIM ** -0.5
    bias_t = rel_bias.astype(jnp.float32).T
    head_idx = jnp.arange(H_MOBA)[None, :, None, None, None]
    gather_blocks = jax.vmap(jax.vmap(lambda blocks, idx: blocks[idx]))

    def chunk(args):
        q_c, i = args
        q0 = i * MOBA_Q_CHUNK
        own = q0 // MOBA_BLOCK
        qpos = q0 + jnp.arange(MOBA_Q_CHUNK)
        g = jnp.einsum('bhqd,bhnd->bhqn', q_c.astype(jnp.float32), k_mean)
        g = jnp.where(blk_ids < own, g, -jnp.inf)
        _, sel = lax.top_k(g, topk)
        valid = (jnp.arange(topk) < own)[:, None]
        k_sel = gather_blocks(kp, sel)
        v_sel = gather_blocks(vp, sel)
        s_sel = jnp.einsum('bhqd,bhqtkd->bhqtk', q_c, k_sel,
                           preferred_element_type=jnp.float32) * scale
        kpos_sel = sel[..., None] * MOBA_BLOCK + jnp.arange(MOBA_BLOCK)
        bias_sel = bias_t[head_idx, rel_bucket(qpos[:, None, None] - kpos_sel)]
        s_sel = jnp.where(valid, s_sel + bias_sel, -jnp.inf)
        k_own = lax.dynamic_slice_in_dim(kp, own, 1, axis=2)[:, :, 0]
        v_own = lax.dynamic_slice_in_dim(vp, own, 1, axis=2)[:, :, 0]
        s_own = jnp.einsum('bhqd,bhkd->bhqk', q_c, k_own,
                           preferred_element_type=jnp.float32) * scale
        kpos_own = own * MOBA_BLOCK + jnp.arange(MOBA_BLOCK)
        dist_own = qpos[:, None] - kpos_own[None, :]
        bias_own = bias_t[:, rel_bucket(dist_own)]
        s_own = jnp.where(dist_own >= 0, s_own + bias_own[None], -jnp.inf)
        s_all = jnp.concatenate(
            [s_sel.reshape(B, H_MOBA, MOBA_Q_CHUNK, topk * MOBA_BLOCK), s_own], axis=-1)
        p = jax.nn.softmax(s_all, axis=-1).astype(v.dtype)
        p_sel = p[..., :topk * MOBA_BLOCK].reshape(B, H_MOBA, MOBA_Q_CHUNK, topk, MOBA_BLOCK)
        p_own = p[..., topk * MOBA_BLOCK:]
        return (jnp.einsum('bhqtk,bhqtkd->bhqd', p_sel, v_sel)
                + jnp.einsum('bhqk,bhkd->bhqd', p_own, v_own))

    o = lax.map(chunk, (qc, jnp.arange(n_qc)))
    o = o.transpose(1, 2, 0, 3, 4).reshape(B, H_MOBA, S, HEAD_DIM)
    return o.transpose(0, 2, 1, 3).reshape(B, S, MOBA_WIDTH)


def hybrid_layer(x, c, positions, norm_g, w_ada, b_ada, w_in, mla_q_norm_g, mla_w_uq,
                 mla_kv_norm_g, mla_w_ukv, mla_q_g, mla_k_g, moba_q_g, moba_k_g, w_out, rel_bias):
    mod = jax.nn.silu(c) @ w_ada + b_ada
    shift, scale, gate = jnp.split(mod[:, None, :], 3, axis=-1)
    h = rms_norm(x, norm_g) * (1 + scale) + shift
    z = h @ w_in
    c_q, c_kv, k_rope, g_mla, q_b, k_b, v_b, g_moba = jnp.split(z, list(IN_OFFSETS), axis=-1)
    o_mla = mla_attention(c_q, c_kv, k_rope, positions, mla_q_norm_g, mla_w_uq,
                          mla_kv_norm_g, mla_w_ukv, mla_q_g, mla_k_g)
    o_moba = moba_attention(q_b, k_b, v_b, moba_q_g, moba_k_g, rel_bias)
    y = jnp.concatenate([jax.nn.silu(g_mla) * o_mla, jax.nn.silu(g_moba) * o_moba], axis=-1) @ w_out
    return x + gate * y


def setup_inputs(seed: int = 0) -> dict:
    key = jax.random.key(seed)
    ks = jax.random.split(key, 20)
    f32 = jnp.float32
    nrm = lambda k, shape, s: jax.random.normal(k, shape, f32) * s
    return {
        "x": nrm(ks[0], (BATCH, SEQ, D_MODEL), 1.0),
        "c": nrm(ks[1], (BATCH, D_MODEL), 1.0),
        "positions": jnp.broadcast_to(jnp.arange(SEQ, dtype=jnp.int32), (BATCH, SEQ)),
        "norm_g": 1.0 + nrm(ks[2], (DEPTH, D_MODEL), 0.05),
        "w_ada": nrm(ks[3], (DEPTH, D_MODEL, 3 * D_MODEL), 0.5 * D_MODEL ** -0.5),
        "b_ada": nrm(ks[4], (DEPTH, 3 * D_MODEL), 0.01),
        "w_in": nrm(ks[5], (DEPTH, D_MODEL, D_IN), D_MODEL ** -0.5),
        "mla_q_norm_g": 1.0 + nrm(ks[6], (DEPTH, Q_LORA), 0.05),
        "mla_w_uq": nrm(ks[7], (DEPTH, Q_LORA, H_MLA * QK_DIM), Q_LORA ** -0.5),
        "mla_kv_norm_g": 1.0 + nrm(ks[8], (DEPTH, KV_LORA), 0.05),
        "mla_w_ukv": nrm(ks[9], (DEPTH, KV_LORA, H_MLA * (NOPE_DIM + V_DIM)), KV_LORA ** -0.5),
        "mla_q_g": 1.0 + nrm(ks[10], (DEPTH, QK_DIM), 0.05),
        "mla_k_g": 1.0 + nrm(ks[11], (DEPTH, QK_DIM), 0.05),
        "moba_q_g": 1.0 + nrm(ks[12], (DEPTH, HEAD_DIM), 0.05),
        "moba_k_g": 1.0 + nrm(ks[13], (DEPTH, HEAD_DIM), 0.05),
        "w_out": nrm(ks[14], (DEPTH, D_MIX, D_MODEL), D_MIX ** -0.5),
        "rel_bias": nrm(ks[15], (N_BUCKETS, H_MOBA), 0.5),
    }


def reference(x, c, positions, norm_g, w_ada, b_ada, w_in, mla_q_norm_g, mla_w_uq,
              mla_kv_norm_g, mla_w_ukv, mla_q_g, mla_k_g, moba_q_g, moba_k_g, w_out, rel_bias):
    for l in range(DEPTH):
        x = hybrid_layer(x, c, positions, norm_g[l], w_ada[l], b_ada[l], w_in[l],
                         mla_q_norm_g[l], mla_w_uq[l], mla_kv_norm_g[l], mla_w_ukv[l],
                         mla_q_g[l], mla_k_g[l], moba_q_g[l], moba_k_g[l], w_out[l], rel_bias)
    return x
```

```python
import functools
import math

import numpy as np
import jax
import jax.numpy as jnp
from jax import lax
from jax.experimental import pallas as pl
from jax.experimental.pallas import tpu as pltpu

F32 = jnp.float32
BF16 = jnp.bfloat16

D_MODEL = 1024
DEPTH = 2
HEAD_DIM = 64
H_MLA = 8
H_MOBA = 8
MLA_WIDTH = H_MLA * HEAD_DIM
MOBA_WIDTH = H_MOBA * HEAD_DIM
Q_LORA = 256
KV_LORA = 128
NOPE_DIM = 64
ROPE_DIM = 32
QK_DIM = NOPE_DIM + ROPE_DIM
ROPE_THETA = 10000.0
MOBA_BLOCK = 256
MOBA_TOPK = 3
N_BUCKETS = 32
REL_MAX_DIST = 4096
EPS = 1e-6

LANES = 128
HALF_ROPE = ROPE_DIM // 2
N_PAIRS = H_MLA // 2
LOG2E = math.log2(math.e)
MLA_QSCALE = QK_DIM ** -0.5 * LOG2E
MOBA_QSCALE = HEAD_DIM ** -0.5 * LOG2E
NEG_BIG = -1e30
FAR_THRESHOLDS = 2
NEAR_TILES = 3
NEVER = 1 << 20

TM = 512
TQ_MLA = 512
VMEM_LIMIT = 56 * 1024 * 1024

_NT = (((1,), (1,)), ((), ()))


def _head_lane_source():
    src = np.full((LANES,), QK_DIM, np.int32)
    src[0:16] = NOPE_DIM + np.arange(16)
    src[16:64] = np.arange(48)
    src[64:80] = NOPE_DIM + HALF_ROPE + np.arange(16)
    src[80:96] = 48 + np.arange(16)
    return src


_LANE_SRC = _head_lane_source()


def _bucket_np(n, dtype):
    n = np.asarray(n)
    max_exact = N_BUCKETS // 2
    nf = np.maximum(n, 1).astype(dtype)
    large = max_exact + (np.log(nf / dtype(max_exact)) / dtype(math.log(REL_MAX_DIST / max_exact))
                         * dtype(N_BUCKETS - max_exact)).astype(np.int32)
    large = np.minimum(large, N_BUCKETS - 1)
    return np.where(n < max_exact, n, large)


def _check_far_tiles(seq):
    n = np.arange(seq + 2)
    for dtype in (np.float32, np.float64):
        b = _bucket_np(n, dtype)
        chg = np.concatenate([[0], (b[1:] != b[:-1]).astype(np.int64)])
        for d in range(NEAR_TILES, seq // MOBA_BLOCK):
            lo, hi = MOBA_BLOCK * (d - 1) + 1, MOBA_BLOCK * (d + 1) - 1
            assert chg[lo - 1:hi + 1].sum() <= FAR_THRESHOLDS, (d, dtype)


def _rel_bucket(dist):
    n = jnp.maximum(dist, 0)
    max_exact = N_BUCKETS // 2
    nf = jnp.maximum(n, 1).astype(F32)
    large = max_exact + (jnp.log(nf / max_exact) / math.log(REL_MAX_DIST / max_exact)
                         * (N_BUCKETS - max_exact)).astype(jnp.int32)
    large = jnp.minimum(large, N_BUCKETS - 1)
    return jnp.where(n < max_exact, n, large)


def _rms(x, g):
    return x * lax.rsqrt(jnp.mean(x * x, axis=-1, keepdims=True) + EPS) * g


def _silu(x):
    return x * jax.nn.sigmoid(x)


def _mod_kernel(c_ref, w_ref, b_ref, o_ref):
    c = c_ref[...]
    o_ref[0] = jnp.dot(_silu(c), w_ref[0], precision=lax.Precision.HIGHEST,
                       preferred_element_type=F32) + b_ref[0]


def _mod_call(c8, w_ada, b_ada):
    depth, d, d3 = w_ada.shape
    nchunk = d3 // d
    return pl.pallas_call(
        _mod_kernel,
        out_shape=jax.ShapeDtypeStruct((depth, 8, d3), F32),
        grid=(depth, nchunk),
        in_specs=[pl.BlockSpec((8, d), lambda l, j: (0, 0)),
                  pl.BlockSpec((1, d, d), lambda l, j: (l, 0, j)),
                  pl.BlockSpec((1, 1, d), lambda l, j: (l, 0, j))],
        out_specs=pl.BlockSpec((1, 8, d), lambda l, j: (l, 0, j)),
        compiler_params=pltpu.CompilerParams(dimension_semantics=("arbitrary", "arbitrary"),
                                             vmem_limit_bytes=VMEM_LIMIT),
        name="adaln_mod",
    )(c8, w_ada, b_ada)


def _rope_kernel(pos_ref, freq_ref, sign_ref, c_ref, s_ref):
    ang = pos_ref[...].astype(F32) * freq_ref[...]
    c_ref[...] = jnp.cos(ang)
    s_ref[...] = jnp.sin(ang) * sign_ref[...]


def _rope_call(pos_col, freq, sign):
    n = pos_col.shape[0]
    tr = 1024
    return pl.pallas_call(
        _rope_kernel,
        out_shape=(jax.ShapeDtypeStruct((n, LANES), F32), jax.ShapeDtypeStruct((n, LANES), F32)),
        grid=(n // tr,),
        in_specs=[pl.BlockSpec((tr, 1), lambda i: (i, 0)),
                  pl.BlockSpec((1, LANES), lambda i: (0, 0)),
                  pl.BlockSpec((1, LANES), lambda i: (0, 0))],
        out_specs=(pl.BlockSpec((tr, LANES), lambda i: (i, 0)),
                   pl.BlockSpec((tr, LANES), lambda i: (i, 0))),
        compiler_params=pltpu.CompilerParams(dimension_semantics=("arbitrary",)),
        name="rope_tables",
    )(pos_col, freq, sign)


def _bias_tile_kernel(thr_ref, rb_ref, o_ref):
    h = pl.program_id(0)
    d = pl.program_id(1)
    shape = (MOBA_BLOCK, MOBA_BLOCK)
    dist = (lax.broadcasted_iota(jnp.int32, shape, 0) - lax.broadcasted_iota(jnp.int32, shape, 1)
            + d * MOBA_BLOCK)
    val = jnp.full(shape, rb_ref[h], F32)
    for b in range(1, N_BUCKETS):
        val = jnp.where(dist >= thr_ref[b], rb_ref[b * H_MOBA + h], val)
    o_ref[0, 0] = jnp.where(dist >= 0, val * LOG2E, NEG_BIG)


def _bias_tile_call(thr, rb_flat):
    return pl.pallas_call(
        _bias_tile_kernel,
        out_shape=jax.ShapeDtypeStruct((H_MOBA, NEAR_TILES, MOBA_BLOCK, MOBA_BLOCK), F32),
        grid_spec=pltpu.PrefetchScalarGridSpec(
            num_scalar_prefetch=1,
            grid=(H_MOBA, NEAR_TILES),
            in_specs=[pl.BlockSpec(memory_space=pltpu.SMEM)],
            out_specs=pl.BlockSpec((1, 1, MOBA_BLOCK, MOBA_BLOCK), lambda h, d, thr: (h, d, 0, 0)),
        ),
        compiler_params=pltpu.CompilerParams(dimension_semantics=("arbitrary", "arbitrary")),
        name="moba_near_bias",
    )(thr, rb_flat)


def _far_tables(bkt, rel_bias, seq):
    nb = seq // MOBA_BLOCK
    n = jnp.arange(seq, dtype=jnp.int32)
    chg = jnp.concatenate([jnp.zeros((1,), bool), bkt[1:] != bkt[:-1]])
    d = jnp.arange(nb, dtype=jnp.int32)[:, None]
    lo = jnp.maximum(MOBA_BLOCK * (d - 1) + 1, 0)
    hi = MOBA_BLOCK * (d + 1) - 1
    inside = chg[None, :] & (n[None, :] > lo) & (n[None, :] <= hi)
    cum = jnp.cumsum(inside.astype(jnp.int32), axis=1)
    idx = [bkt[jnp.minimum(lo[:, 0], seq - 1)]]
    thr = []
    for k in range(1, FAR_THRESHOLDS + 1):
        t = jnp.min(jnp.where(inside & (cum == k), n[None, :], NEVER), axis=1)
        thr.append(t - MOBA_BLOCK * d[:, 0])
        idx.append(bkt[jnp.minimum(t, seq - 1)])
    ttab = jnp.stack(thr, axis=1).reshape(-1).astype(jnp.int32)
    vals = rel_bias.astype(F32)[jnp.stack(idx, axis=1)] * LOG2E
    vtab = vals.transpose(0, 2, 1).reshape(-1)
    return ttab, vtab


def _mla_head(xh, g, rc, rs):
    ms = jnp.sum(xh * xh, axis=-1, keepdims=True) * (1.0 / QK_DIM)
    xn = xh * lax.rsqrt(ms + EPS) * g
    return xn * rc + pltpu.roll(xn, LANES // 2, 1) * rs


def _moba_pair_norm(x, g, lo_half):
    x2 = x * x
    s_lo = jnp.sum(jnp.where(lo_half, x2, 0.0), axis=-1, keepdims=True)
    s_hi = jnp.sum(jnp.where(lo_half, 0.0, x2), axis=-1, keepdims=True)
    ms = jnp.where(lo_half, s_lo, s_hi) * (1.0 / HEAD_DIM)
    return x * lax.rsqrt(ms + EPS) * g


def _proj_in_kernel(x_ref, mod_ref, ng_ref, win_ref, qng_ref, wuq_ref, kvng_ref, wuk_ref, wuv_ref,
                    qg_ref, kg_ref, bqg_ref, bkg_ref, rc_ref, rs_ref,
                    qm_ref, km_ref, vm_ref, sgm_ref, qb_ref, kb_ref, vb_ref, sgb_ref, kmean_ref):
    x = x_ref[0]
    h = _rms(x, ng_ref[...]) * (1.0 + mod_ref[0, 1:2, :]) + mod_ref[0, 0:1, :]
    hb = h.astype(BF16)

    def proj(lo, hi):
        return jnp.dot(hb, win_ref[:, lo:hi], preferred_element_type=F32)

    rc = rc_ref[...]
    rs = rs_ref[...]

    cqn = _rms(proj(0, 256), qng_ref[...]).astype(BF16)
    q = jnp.dot(cqn, wuq_ref[...], preferred_element_type=F32)
    qg = qg_ref[...]
    for hh in range(H_MLA):
        sl = slice(LANES * hh, LANES * (hh + 1))
        qm_ref[0, :, sl] = (_mla_head(q[:, sl], qg, rc, rs) * MLA_QSCALE).astype(BF16)

    ckvn = _rms(proj(256, 384), kvng_ref[...]).astype(BF16)
    kn = jnp.dot(ckvn, wuk_ref[...], preferred_element_type=F32)
    kr = proj(384, 512)
    kg = kg_ref[...]
    for hh in range(H_MLA):
        sl = slice(LANES * hh, LANES * (hh + 1))
        km_ref[0, :, sl] = _mla_head(kn[:, sl] + kr, kg, rc, rs).astype(BF16)
    vm_ref[0] = jnp.dot(ckvn, wuv_ref[...], preferred_element_type=F32).astype(BF16)
    sgm_ref[0] = _silu(proj(512, 1024)).astype(BF16)

    lo_half = lax.broadcasted_iota(jnp.int32, (TM, LANES), 1) < HEAD_DIM
    qbz = proj(1024, 1536)
    kbz = proj(1536, 2048)
    bqg = bqg_ref[...]
    bkg = bkg_ref[...]
    for p in range(N_PAIRS):
        sl = slice(LANES * p, LANES * (p + 1))
        qb_ref[0, :, sl] = _moba_pair_norm(qbz[:, sl], bqg, lo_half)
        kn_p = _moba_pair_norm(kbz[:, sl], bkg, lo_half)
        kb_ref[0, :, sl] = kn_p.astype(BF16)
        for blk in range(TM // MOBA_BLOCK):
            rows = kn_p[MOBA_BLOCK * blk:MOBA_BLOCK * (blk + 1)]
            kmean_ref[0, 0, blk:blk + 1, sl] = jnp.sum(rows, axis=0, keepdims=True) * (1.0 / MOBA_BLOCK)
    vb_ref[0] = proj(2048, 2560).astype(BF16)
    sgb_ref[0] = _silu(proj(2560, 3072)).astype(BF16)


def _proj_in_call(x, mod3, ng, win, qng, wuq, kvng, wuk, wuv, qg, kg, bqg, bkg, rope_c, rope_s):
    b, s, d = x.shape
    nt = s // TM
    const2 = lambda bi, i: (0, 0)
    tok = lambda w: pl.BlockSpec((1, TM, w), lambda bi, i: (bi, i, 0))
    full = lambda a: pl.BlockSpec(a.shape, const2)
    rope_spec = pl.BlockSpec((TM, LANES), lambda bi, i: (bi * nt + i, 0))
    out_shape = (
        jax.ShapeDtypeStruct((b, s, H_MLA * LANES), BF16),
        jax.ShapeDtypeStruct((b, s, H_MLA * LANES), BF16),
        jax.ShapeDtypeStruct((b, s, MLA_WIDTH), BF16),
        jax.ShapeDtypeStruct((b, s, MLA_WIDTH), BF16),
        jax.ShapeDtypeStruct((b, s, MOBA_WIDTH), F32),
        jax.ShapeDtypeStruct((b, s, MOBA_WIDTH), BF16),
        jax.ShapeDtypeStruct((b, s, MOBA_WIDTH), BF16),
        jax.ShapeDtypeStruct((b, s, MOBA_WIDTH), BF16),
        jax.ShapeDtypeStruct((b, nt, TM // MOBA_BLOCK, MOBA_WIDTH), F32),
    )
    out_specs = (tok(H_MLA * LANES), tok(H_MLA * LANES), tok(MLA_WIDTH), tok(MLA_WIDTH),
                 tok(MOBA_WIDTH), tok(MOBA_WIDTH), tok(MOBA_WIDTH), tok(MOBA_WIDTH),
                 pl.BlockSpec((1, 1, TM // MOBA_BLOCK, MOBA_WIDTH), lambda bi, i: (bi, i, 0, 0)))
    return pl.pallas_call(
        _proj_in_kernel,
        out_shape=out_shape,
        grid=(b, nt),
        in_specs=[tok(d), pl.BlockSpec((1, 3, d), lambda bi, i: (bi, 0, 0)), full(ng), full(win),
                  full(qng), full(wuq), full(kvng), full(wuk), full(wuv),
                  full(qg), full(kg), full(bqg), full(bkg), rope_spec, rope_spec],
        out_specs=out_specs,
        compiler_params=pltpu.CompilerParams(dimension_semantics=("arbitrary", "arbitrary"),
                                             vmem_limit_bytes=VMEM_LIMIT),
        name="proj_in",
    )(x, mod3, ng, win, qng, wuq, kvng, wuk, wuv, qg, kg, bqg, bkg, rope_c, rope_s)


def _flash_update(hh, s, v, m_sc, l_sc, acc_sc, row_on=None):
    m_prev = m_sc[hh]
    m_new = jnp.maximum(m_prev, jnp.max(s, axis=-1, keepdims=True))
    if row_on is None:
        shift = m_new
    else:
        m_new = jnp.where(row_on, m_new, m_prev)
        shift = jnp.where(row_on, m_new, jnp.inf)
    alpha = jnp.exp2(m_prev - m_new)
    p = jnp.exp2(s - shift)
    l_sc[hh] = alpha * l_sc[hh] + jnp.sum(p, axis=-1, keepdims=True)
    acc_sc[hh] = alpha * acc_sc[hh] + jnp.dot(p.astype(BF16), v, preferred_element_type=F32)
    m_sc[hh] = m_new


def _flash_init(m_sc, l_sc, acc_sc):
    m_sc[...] = jnp.full(m_sc.shape, NEG_BIG, F32)
    l_sc[...] = jnp.zeros(l_sc.shape, F32)
    acc_sc[...] = jnp.zeros(acc_sc.shape, F32)


def _flash_finish(o_ref, sg_ref, l_sc, acc_sc):
    lo_half = lax.broadcasted_iota(jnp.int32, acc_sc.shape[1:], 1) < HEAD_DIM
    o = jnp.where(lo_half, acc_sc[0] / l_sc[0], acc_sc[1] / l_sc[1])
    o_ref[0] = (o * sg_ref[0].astype(F32)).astype(o_ref.dtype)


def _mla_kernel(q_ref, k_ref, v_ref, sg_ref, o_ref, m_sc, l_sc, acc_sc):
    i = pl.program_id(2)
    t = TQ_MLA
    _flash_init(m_sc, l_sc, acc_sc)

    def step(kstart, diagonal):
        v = v_ref[0, pl.ds(kstart, t), :]
        for hh in range(2):
            sl = slice(LANES * hh, LANES * (hh + 1))
            s = lax.dot_general(q_ref[0, :, sl], k_ref[0, pl.ds(kstart, t), sl], _NT,
                                preferred_element_type=F32)
            if diagonal:
                row = lax.broadcasted_iota(jnp.int32, (t, t), 0)
                col = lax.broadcasted_iota(jnp.int32, (t, t), 1)
                s = jnp.where(col <= row, s, NEG_BIG)
            _flash_update(hh, s, v, m_sc, l_sc, acc_sc)

    def body(j, carry):
        step(pl.multiple_of(j * t, t), False)
        return carry

    lax.fori_loop(0, i, body, 0)
    step(pl.multiple_of(i * t, t), True)
    _flash_finish(o_ref, sg_ref, l_sc, acc_sc)


def _mla_call(qm, km, vm, sgm):
    b, s, _ = qm.shape
    t = TQ_MLA
    return pl.pallas_call(
        _mla_kernel,
        out_shape=jax.ShapeDtypeStruct((b, s, MLA_WIDTH), BF16),
        grid=(b, N_PAIRS, s // t),
        in_specs=[pl.BlockSpec((1, t, 2 * LANES), lambda bi, p, i: (bi, i, p)),
                  pl.BlockSpec((1, s, 2 * LANES), lambda bi, p, i: (bi, 0, p)),
                  pl.BlockSpec((1, s, LANES), lambda bi, p, i: (bi, 0, p)),
                  pl.BlockSpec((1, t, LANES), lambda bi, p, i: (bi, i, p))],
        out_specs=pl.BlockSpec((1, t, LANES), lambda bi, p, i: (bi, i, p)),
        scratch_shapes=[pltpu.VMEM((2, t, 1), F32), pltpu.VMEM((2, t, 1), F32),
                        pltpu.VMEM((2, t, LANES), F32)],
        compiler_params=pltpu.CompilerParams(
            dimension_semantics=("arbitrary", "arbitrary", "arbitrary"),
            vmem_limit_bytes=VMEM_LIMIT),
        name="mla_attention",
    )(qm, km, vm, sgm)


def _moba_kernel(ttab_ref, vtab_ref, q_ref, k_ref, v_ref, kmean_ref, bt_ref, sg_ref, o_ref,
                 m_sc, l_sc, acc_sc, rc_sc, qs_sc, sel_sc):
    pair = pl.program_id(1)
    own = pl.program_id(2)
    t = MOBA_BLOCK
    nb = kmean_ref.shape[1]
    _flash_init(m_sc, l_sc, acc_sc)
    rc_sc[...] = (lax.broadcasted_iota(jnp.int32, (t, t), 0)
                  - lax.broadcasted_iota(jnp.int32, (t, t), 1))

    q = q_ref[0]
    km = kmean_ref[0]
    lane = lax.broadcasted_iota(jnp.int32, (t, LANES), 1)
    col = lax.broadcasted_iota(jnp.int32, (t, nb), 1).astype(F32)
    for hh in range(2):
        qh = jnp.where((lane < HEAD_DIM) if hh == 0 else (lane >= HEAD_DIM), q, 0.0)
        qs_sc[hh] = (qh * MOBA_QSCALE).astype(BF16)
        g = lax.dot_general(qh, km, _NT, precision=lax.Precision.HIGHEST,
                            preferred_element_type=F32)
        rem = jnp.where(col < own.astype(F32), g, -jnp.inf)
        for k in range(MOBA_TOPK):
            mx = jnp.max(rem, axis=-1, keepdims=True)
            idx = jnp.min(jnp.where(rem == mx, col, float(nb)), axis=-1, keepdims=True)
            sel_sc[hh, k] = jnp.where(k < own, idx, -1.0)
            rem = jnp.where(col == idx, -jnp.inf, rem)

    def step(j, bias_fn, select_rows):
        kstart = pl.multiple_of(j * t, t)
        kk = k_ref[0, pl.ds(kstart, t), :]
        vv = v_ref[0, pl.ds(kstart, t), :]
        jf = j.astype(F32)
        for hh in range(2):
            s = lax.dot_general(qs_sc[hh], kk, _NT, preferred_element_type=F32) + bias_fn(hh)
            row_on = None
            if select_rows:
                row_on = (sel_sc[hh, 0] == jf) | (sel_sc[hh, 1] == jf) | (sel_sc[hh, 2] == jf)
            _flash_update(hh, s, vv, m_sc, l_sc, acc_sc, row_on)

    def near_bias(d):
        return lambda hh: bt_ref[0, hh, d]

    def far_bias(d):
        def fn(hh):
            base = (d * H_MOBA + pair * 2 + hh) * (FAR_THRESHOLDS + 1)
            rc = rc_sc[...]
            bias = jnp.full((t, t), vtab_ref[base], F32)
            for k in range(FAR_THRESHOLDS):
                bias = jnp.where(rc >= ttab_ref[d * FAR_THRESHOLDS + k], vtab_ref[base + k + 1], bias)
            return bias
        return fn

    step(own, near_bias(0), False)
    for d in range(1, NEAR_TILES):
        @pl.when(own >= d)
        def _():
            step(own - d, near_bias(d), True)

    def body(j, carry):
        step(j, far_bias(own - j), True)
        return carry

    lax.fori_loop(0, jnp.maximum(own - (NEAR_TILES - 1), 0), body, 0)
    _flash_finish(o_ref, sg_ref, l_sc, acc_sc)


def _moba_call(ttab, vtab, qb, kb, vb, kmean, bias_tiles, sgb):
    b, s, _ = qb.shape
    t = MOBA_BLOCK
    nb = s // t
    tokq = pl.BlockSpec((1, t, LANES), lambda bi, p, i, tt: (bi, i, p))
    seq = pl.BlockSpec((1, s, LANES), lambda bi, p, i, tt: (bi, 0, p))
    return pl.pallas_call(
        _moba_kernel,
        out_shape=jax.ShapeDtypeStruct((b, s, MOBA_WIDTH), BF16),
        grid_spec=pltpu.PrefetchScalarGridSpec(
            num_scalar_prefetch=1,
            grid=(b, N_PAIRS, nb),
            in_specs=[pl.BlockSpec(memory_space=pltpu.SMEM),
                      tokq, seq, seq,
                      pl.BlockSpec((1, nb, LANES), lambda bi, p, i, tt: (bi, 0, p)),
                      pl.BlockSpec((1, 2, NEAR_TILES, t, t), lambda bi, p, i, tt: (p, 0, 0, 0, 0)),
                      tokq],
            out_specs=tokq,
            scratch_shapes=[pltpu.VMEM((2, t, 1), F32), pltpu.VMEM((2, t, 1), F32),
                            pltpu.VMEM((2, t, LANES), F32), pltpu.VMEM((t, t), jnp.int32),
                            pltpu.VMEM((2, t, LANES), BF16), pltpu.VMEM((2, MOBA_TOPK, t, 1), F32)],
        ),
        compiler_params=pltpu.CompilerParams(
            dimension_semantics=("arbitrary", "arbitrary", "arbitrary"),
            vmem_limit_bytes=VMEM_LIMIT),
        name="moba_attention",
    )(ttab, vtab, qb, kb, vb, kmean, bias_tiles, sgb)


def _proj_out_kernel(am_ref, ab_ref, w_ref, x_ref, mod_ref, o_ref):
    y = (jnp.dot(am_ref[0], w_ref[0:MLA_WIDTH, :], preferred_element_type=F32)
         + jnp.dot(ab_ref[0], w_ref[MLA_WIDTH:, :], preferred_element_type=F32))
    o_ref[0] = x_ref[0] + mod_ref[0, 2:3, :] * y


def _proj_out_call(a_mla, a_moba, w_out, x, mod3):
    b, s, d = x.shape
    tok = lambda w: pl.BlockSpec((1, TM, w), lambda bi, i: (bi, i, 0))
    return pl.pallas_call(
        _proj_out_kernel,
        out_shape=jax.ShapeDtypeStruct((b, s, d), F32),
        grid=(b, s // TM),
        in_specs=[tok(MLA_WIDTH), tok(MOBA_WIDTH), pl.BlockSpec(w_out.shape, lambda bi, i: (0, 0)),
                  tok(d), pl.BlockSpec((1, 3, d), lambda bi, i: (bi, 0, 0))],
        out_specs=tok(d),
        compiler_params=pltpu.CompilerParams(dimension_semantics=("arbitrary", "arbitrary"),
                                             vmem_limit_bytes=VMEM_LIMIT),
        name="proj_out",
    )(a_mla, a_moba, w_out, x, mod3)


def _to_head_lanes(a):
    a = jnp.concatenate([a, jnp.zeros(a.shape[:-1] + (1,), a.dtype)], axis=-1)
    return a[..., _LANE_SRC]


def _layer_weights(w_in, w_uq, w_ukv, q_g, k_g, bq_g, bk_g):
    o0 = Q_LORA
    o1 = o0 + KV_LORA
    o2 = o1 + ROPE_DIM
    k_rope = w_in[:, o1:o2]
    kr128 = jnp.zeros((D_MODEL, LANES), w_in.dtype)
    kr128 = kr128.at[:, 0:HALF_ROPE].set(k_rope[:, :HALF_ROPE])
    kr128 = kr128.at[:, LANES // 2:LANES // 2 + HALF_ROPE].set(k_rope[:, HALF_ROPE:])
    win = jnp.concatenate([w_in[:, :o1], kr128, w_in[:, o2:]], axis=1).astype(BF16)
    wuq = _to_head_lanes(w_uq.reshape(Q_LORA, H_MLA, QK_DIM)).reshape(Q_LORA, H_MLA * LANES)
    ukv = w_ukv.reshape(KV_LORA, H_MLA, NOPE_DIM + HEAD_DIM)
    k_nope = jnp.concatenate([ukv[..., :NOPE_DIM], jnp.zeros((KV_LORA, H_MLA, ROPE_DIM), ukv.dtype)], -1)
    wuk = _to_head_lanes(k_nope).reshape(KV_LORA, H_MLA * LANES)
    wuv = ukv[..., NOPE_DIM:].reshape(KV_LORA, MLA_WIDTH)
    pair = lambda g: jnp.concatenate([g, g]).reshape(1, LANES)
    return (win, wuq.astype(BF16), wuk.astype(BF16), wuv.astype(BF16),
            _to_head_lanes(q_g).reshape(1, LANES), _to_head_lanes(k_g).reshape(1, LANES),
            pair(bq_g), pair(bk_g))


def kernel(x, c, positions, norm_g, w_ada, b_ada, w_in, mla_q_norm_g, mla_w_uq, mla_kv_norm_g,
           mla_w_ukv, mla_q_g, mla_k_g, moba_q_g, moba_k_g, w_out, rel_bias):
    b, s, d = x.shape
    depth = w_in.shape[0]
    assert d == D_MODEL and s % TM == 0 and s % TQ_MLA == 0 and TM % MOBA_BLOCK == 0 and b <= 8
    _check_far_tiles(s)
    nb = s // MOBA_BLOCK

    c8 = jnp.pad(c, ((0, 8 - b), (0, 0)))
    mod = _mod_call(c8, w_ada, b_ada.reshape(depth, 1, 3 * d))[:, :b].reshape(depth, b, 3, d)

    inv_freq = ROPE_THETA ** (-jnp.arange(0, HALF_ROPE, dtype=F32) / HALF_ROPE)
    freq = jnp.zeros((LANES,), F32).at[0:HALF_ROPE].set(inv_freq)
    freq = freq.at[LANES // 2:LANES // 2 + HALF_ROPE].set(inv_freq).reshape(1, LANES)
    sign = jnp.zeros((LANES,), F32).at[0:HALF_ROPE].set(-1.0)
    sign = sign.at[LANES // 2:LANES // 2 + HALF_ROPE].set(1.0).reshape(1, LANES)
    rope_c, rope_s = _rope_call(positions.reshape(b * s, 1), freq, sign)

    bkt = _rel_bucket(jnp.arange(s, dtype=jnp.int32))
    thr = jnp.sum(bkt[None, :] < jnp.arange(N_BUCKETS, dtype=jnp.int32)[:, None], axis=1).astype(jnp.int32)
    bias_tiles = _bias_tile_call(thr, rel_bias.astype(F32).reshape(-1))
    bias_tiles = bias_tiles.reshape(N_PAIRS, 2, NEAR_TILES, MOBA_BLOCK, MOBA_BLOCK)
    ttab, vtab = _far_tables(bkt, rel_bias, s)

    for l in range(depth):
        win, wuq, wuk, wuv, qg, kg, bqg, bkg = _layer_weights(
            w_in[l], mla_w_uq[l], mla_w_ukv[l], mla_q_g[l], mla_k_g[l], moba_q_g[l], moba_k_g[l])
        qm, km, vm, sgm, qb, kb, vb, sgb, kmean = _proj_in_call(
            x, mod[l], norm_g[l].reshape(1, d), win, mla_q_norm_g[l].reshape(1, Q_LORA), wuq,
            mla_kv_norm_g[l].reshape(1, KV_LORA), wuk, wuv, qg, kg, bqg, bkg, rope_c, rope_s)
        a_mla = _mla_call(qm, km, vm, sgm)
        a_moba = _moba_call(ttab, vtab, qb, kb, vb, kmean.reshape(b, nb, MOBA_WIDTH), bias_tiles, sgb)
        x = _proj_out_call(a_mla, a_moba, w_out[l].astype(BF16), x, mod[l])
    return x
```

```python
import functools
import math

import numpy as np
import jax
import jax.numpy as jnp
from jax import lax
from jax.experimental import pallas as pl
from jax.experimental.pallas import tpu as pltpu

F32 = jnp.float32
BF16 = jnp.bfloat16

D_MODEL = 1024
DEPTH = 2
HEAD_DIM = 64
H_MLA = 8
H_MOBA = 8
MLA_WIDTH = H_MLA * HEAD_DIM
MOBA_WIDTH = H_MOBA * HEAD_DIM
Q_LORA = 256
KV_LORA = 128
NOPE_DIM = 64
ROPE_DIM = 32
QK_DIM = NOPE_DIM + ROPE_DIM
ROPE_THETA = 10000.0
MOBA_BLOCK = 256
MOBA_TOPK = 3
N_BUCKETS = 32
REL_MAX_DIST = 4096
EPS = 1e-6

LANES = 128
HALF_ROPE = ROPE_DIM // 2
N_PAIRS = H_MLA // 2
LOG2E = math.log2(math.e)
MLA_QSCALE = QK_DIM ** -0.5 * LOG2E
MOBA_QSCALE = HEAD_DIM ** -0.5 * LOG2E
NEG_BIG = -1e30
FAR_THRESHOLDS = 2
NEAR_TILES = 3
NEVER = 1 << 20

TM = 512
TQ_MLA = 512
VMEM_LIMIT = 56 * 1024 * 1024

_NT = (((1,), (1,)), ((), ()))


def _head_lane_source():
    src = np.full((LANES,), QK_DIM, np.int32)
    src[0:16] = NOPE_DIM + np.arange(16)
    src[16:64] = np.arange(48)
    src[64:80] = NOPE_DIM + HALF_ROPE + np.arange(16)
    src[80:96] = 48 + np.arange(16)
    return src


_LANE_SRC = _head_lane_source()


def _bucket_np(n, dtype):
    n = np.asarray(n)
    max_exact = N_BUCKETS // 2
    nf = np.maximum(n, 1).astype(dtype)
    large = max_exact + (np.log(nf / dtype(max_exact)) / dtype(math.log(REL_MAX_DIST / max_exact))
                         * dtype(N_BUCKETS - max_exact)).astype(np.int32)
    large = np.minimum(large, N_BUCKETS - 1)
    return np.where(n < max_exact, n, large)


def _check_far_tiles(seq):
    n = np.arange(seq + 2)
    for dtype in (np.float32, np.float64):
        b = _bucket_np(n, dtype)
        chg = np.concatenate([[0], (b[1:] != b[:-1]).astype(np.int64)])
        for d in range(NEAR_TILES, seq // MOBA_BLOCK):
            lo, hi = MOBA_BLOCK * (d - 1) + 1, MOBA_BLOCK * (d + 1) - 1
            assert chg[lo - 1:hi + 1].sum() <= FAR_THRESHOLDS, (d, dtype)


def _rel_bucket(dist):
    n = jnp.maximum(dist, 0)
    max_exact = N_BUCKETS // 2
    nf = jnp.maximum(n, 1).astype(F32)
    large = max_exact + (jnp.log(nf / max_exact) / math.log(REL_MAX_DIST / max_exact)
                         * (N_BUCKETS - max_exact)).astype(jnp.int32)
    large = jnp.minimum(large, N_BUCKETS - 1)
    return jnp.where(n < max_exact, n, large)


def _rms(x, g):
    return x * lax.rsqrt(jnp.mean(x * x, axis=-1, keepdims=True) + EPS) * g


def _silu(x):
    return x * jax.nn.sigmoid(x)


def _mod_kernel(c_ref, w_ref, b_ref, o_ref):
    c = c_ref[...]
    o_ref[0] = jnp.dot(_silu(c), w_ref[0], precision=lax.Precision.HIGHEST,
                       preferred_element_type=F32) + b_ref[0]


def _mod_call(c8, w_ada, b_ada):
    depth, d, d3 = w_ada.shape
    nchunk = d3 // d
    return pl.pallas_call(
        _mod_kernel,
        out_shape=jax.ShapeDtypeStruct((depth, 8, d3), F32),
        grid=(depth, nchunk),
        in_specs=[pl.BlockSpec((8, d), lambda l, j: (0, 0)),
                  pl.BlockSpec((1, d, d), lambda l, j: (l, 0, j)),
                  pl.BlockSpec((1, 1, d), lambda l, j: (l, 0, j))],
        out_specs=pl.BlockSpec((1, 8, d), lambda l, j: (l, 0, j)),
        compiler_params=pltpu.CompilerParams(dimension_semantics=("arbitrary", "arbitrary"),
                                             vmem_limit_bytes=VMEM_LIMIT),
        name="adaln_mod",
    )(c8, w_ada, b_ada)


def _rope_kernel(pos_ref, freq_ref, sign_ref, c_ref, s_ref):
    ang = pos_ref[...].astype(F32) * freq_ref[...]
    c_ref[...] = jnp.cos(ang)
    s_ref[...] = jnp.sin(ang) * sign_ref[...]


def _rope_call(pos_col, freq, sign):
    n = pos_col.shape[0]
    tr = 1024
    return pl.pallas_call(
        _rope_kernel,
        out_shape=(jax.ShapeDtypeStruct((n, LANES), F32), jax.ShapeDtypeStruct((n, LANES), F32)),
        grid=(n // tr,),
        in_specs=[pl.BlockSpec((tr, 1), lambda i: (i, 0)),
                  pl.BlockSpec((1, LANES), lambda i: (0, 0)),
                  pl.BlockSpec((1, LANES), lambda i: (0, 0))],
        out_specs=(pl.BlockSpec((tr, LANES), lambda i: (i, 0)),
                   pl.BlockSpec((tr, LANES), lambda i: (i, 0))),
        compiler_params=pltpu.CompilerParams(dimension_semantics=("arbitrary",)),
        name="rope_tables",
    )(pos_col, freq, sign)


def _bias_tile_kernel(thr_ref, rb_ref, o_ref):
    h = pl.program_id(0)
    d = pl.program_id(1)
    shape = (MOBA_BLOCK, MOBA_BLOCK)
    dist = (lax.broadcasted_iota(jnp.int32, shape, 0) - lax.broadcasted_iota(jnp.int32, shape, 1)
            + d * MOBA_BLOCK)
    val = jnp.full(shape, rb_ref[h], F32)
    for b in range(1, N_BUCKETS):
        val = jnp.where(dist >= thr_ref[b], rb_ref[b * H_MOBA + h], val)
    o_ref[0, 0] = jnp.where(dist >= 0, val * LOG2E, NEG_BIG)


def _bias_tile_call(thr, rb_flat):
    return pl.pallas_call(
        _bias_tile_kernel,
        out_shape=jax.ShapeDtypeStruct((H_MOBA, NEAR_TILES, MOBA_BLOCK, MOBA_BLOCK), F32),
        grid_spec=pltpu.PrefetchScalarGridSpec(
            num_scalar_prefetch=1,
            grid=(H_MOBA, NEAR_TILES),
            in_specs=[pl.BlockSpec(memory_space=pltpu.SMEM)],
            out_specs=pl.BlockSpec((1, 1, MOBA_BLOCK, MOBA_BLOCK), lambda h, d, thr: (h, d, 0, 0)),
        ),
        compiler_params=pltpu.CompilerParams(dimension_semantics=("arbitrary", "arbitrary")),
        name="moba_near_bias",
    )(thr, rb_flat)


def _far_tables(bkt, rel_bias, seq):
    nb = seq // MOBA_BLOCK
    n = jnp.arange(seq, dtype=jnp.int32)
    chg = jnp.concatenate([jnp.zeros((1,), bool), bkt[1:] != bkt[:-1]])
    d = jnp.arange(nb, dtype=jnp.int32)[:, None]
    lo = jnp.maximum(MOBA_BLOCK * (d - 1) + 1, 0)
    hi = MOBA_BLOCK * (d + 1) - 1
    inside = chg[None, :] & (n[None, :] > lo) & (n[None, :] <= hi)
    cum = jnp.cumsum(inside.astype(jnp.int32), axis=1)
    idx = [bkt[jnp.minimum(lo[:, 0], seq - 1)]]
    thr = []
    for k in range(1, FAR_THRESHOLDS + 1):
        t = jnp.min(jnp.where(inside & (cum == k), n[None, :], NEVER), axis=1)
        thr.append(t - MOBA_BLOCK * d[:, 0])
        idx.append(bkt[jnp.minimum(t, seq - 1)])
    ttab = jnp.stack(thr, axis=1).reshape(-1).astype(jnp.int32)
    vals = rel_bias.astype(F32)[jnp.stack(idx, axis=1)] * LOG2E
    vtab = vals.transpose(0, 2, 1).reshape(-1)
    return ttab, vtab


def _mla_head(xh, g, rc, rs):
    ms = jnp.sum(xh * xh, axis=-1, keepdims=True) * (1.0 / QK_DIM)
    xn = xh * lax.rsqrt(ms + EPS) * g
    return xn * rc + pltpu.roll(xn, LANES // 2, 1) * rs


def _moba_pair_norm(x, g, lo_half):
    x2 = x * x
    s_lo = jnp.sum(jnp.where(lo_half, x2, 0.0), axis=-1, keepdims=True)
    s_hi = jnp.sum(jnp.where(lo_half, 0.0, x2), axis=-1, keepdims=True)
    ms = jnp.where(lo_half, s_lo, s_hi) * (1.0 / HEAD_DIM)
    return x * lax.rsqrt(ms + EPS) * g


def _proj_in_kernel(x_ref, mod_ref, ng_ref, win_ref, qng_ref, wuq_ref, kvng_ref, wuk_ref, wuv_ref,
                    qg_ref, kg_ref, bqg_ref, bkg_ref, rc_ref, rs_ref,
                    qm_ref, km_ref, vm_ref, sgm_ref, qb_ref, kb_ref, vb_ref, sgb_ref, kmean_ref):
    x = x_ref[0]
    h = _rms(x, ng_ref[...]) * (1.0 + mod_ref[0, 1:2, :]) + mod_ref[0, 0:1, :]
    hb = h.astype(BF16)

    def proj(lo, hi):
        return jnp.dot(hb, win_ref[:, lo:hi], preferred_element_type=F32)

    rc = rc_ref[...]
    rs = rs_ref[...]

    cqn = _rms(proj(0, 256), qng_ref[...]).astype(BF16)
    q = jnp.dot(cqn, wuq_ref[...], preferred_element_type=F32)
    qg = qg_ref[...]
    for hh in range(H_MLA):
        sl = slice(LANES * hh, LANES * (hh + 1))
        qm_ref[0, :, sl] = (_mla_head(q[:, sl], qg, rc, rs) * MLA_QSCALE).astype(BF16)

    ckvn = _rms(proj(256, 384), kvng_ref[...]).astype(BF16)
    kn = jnp.dot(ckvn, wuk_ref[...], preferred_element_type=F32)
    kr = proj(384, 512)
    kg = kg_ref[...]
    for hh in range(H_MLA):
        sl = slice(LANES * hh, LANES * (hh + 1))
        km_ref[0, :, sl] = _mla_head(kn[:, sl] + kr, kg, rc, rs).astype(BF16)
    vm_ref[0] = jnp.dot(ckvn, wuv_ref[...], preferred_element_type=F32).astype(BF16)
    sgm_ref[0] = _silu(proj(512, 1024)).astype(BF16)

    lo_half = lax.broadcasted_iota(jnp.int32, (TM, LANES), 1) < HEAD_DIM
    qbz = proj(1024, 1536)
    kbz = proj(1536, 2048)
    bqg = bqg_ref[...]
    bkg = bkg_ref[...]
    for p in range(N_PAIRS):
        sl = slice(LANES * p, LANES * (p + 1))
        qb_ref[0, :, sl] = _moba_pair_norm(qbz[:, sl], bqg, lo_half)
        kn_p = _moba_pair_norm(kbz[:, sl], bkg, lo_half)
        kb_ref[0, :, sl] = kn_p.astype(BF16)
        for blk in range(TM // MOBA_BLOCK):
            rows = kn_p[MOBA_BLOCK * blk:MOBA_BLOCK * (blk + 1)]
            kmean_ref[0, 0, blk:blk + 1, sl] = jnp.sum(rows, axis=0, keepdims=True) * (1.0 / MOBA_BLOCK)
    vb_ref[0] = proj(2048, 2560).astype(BF16)
    sgb_ref[0] = _silu(proj(2560, 3072)).astype(BF16)


def _proj_in_call(x, mod3, ng, win, qng, wuq, kvng, wuk, wuv, qg, kg, bqg, bkg, rope_c, rope_s):
    b, s, d = x.shape
    nt = s // TM
    const2 = lambda bi, i: (0, 0)
    tok = lambda w: pl.BlockSpec((1, TM, w), lambda bi, i: (bi, i, 0))
    full = lambda a: pl.BlockSpec(a.shape, const2)
    rope_spec = pl.BlockSpec((TM, LANES), lambda bi, i: (bi * nt + i, 0))
    out_shape = (
        jax.ShapeDtypeStruct((b, s, H_MLA * LANES), BF16),
        jax.ShapeDtypeStruct((b, s, H_MLA * LANES), BF16),
        jax.ShapeDtypeStruct((b, s, MLA_WIDTH), BF16),
        jax.ShapeDtypeStruct((b, s, MLA_WIDTH), BF16),
        jax.ShapeDtypeStruct((b, s, MOBA_WIDTH), F32),
        jax.ShapeDtypeStruct((b, s, MOBA_WIDTH), BF16),
        jax.ShapeDtypeStruct((b, s, MOBA_WIDTH), BF16),
        jax.ShapeDtypeStruct((b, s, MOBA_WIDTH), BF16),
        jax.ShapeDtypeStruct((b, nt, TM // MOBA_BLOCK, MOBA_WIDTH), F32),
    )
    out_specs = (tok(H_MLA * LANES), tok(H_MLA * LANES), tok(MLA_WIDTH), tok(MLA_WIDTH),
                 tok(MOBA_WIDTH), tok(MOBA_WIDTH), tok(MOBA_WIDTH), tok(MOBA_WIDTH),
                 pl.BlockSpec((1, 1, TM // MOBA_BLOCK, MOBA_WIDTH), lambda bi, i: (bi, i, 0, 0)))
    return pl.pallas_call(
        _proj_in_kernel,
        out_shape=out_shape,
        grid=(b, nt),
        in_specs=[tok(d), pl.BlockSpec((1, 3, d), lambda bi, i: (bi, 0, 0)), full(ng), full(win),
                  full(qng), full(wuq), full(kvng), full(wuk), full(wuv),
                  full(qg), full(kg), full(bqg), full(bkg), rope_spec, rope_spec],
        out_specs=out_specs,
        compiler_params=pltpu.CompilerParams(dimension_semantics=("arbitrary", "arbitrary"),
                                             vmem_limit_bytes=VMEM_LIMIT),
        name="proj_in",
    )(x, mod3, ng, win, qng, wuq, kvng, wuk, wuv, qg, kg, bqg, bkg, rope_c, rope_s)


def _flash_update(hh, s, v_ext, m_sc, acc_sc, row_on=None):
    m_prev = m_sc[hh]
    m_new = jnp.maximum(m_prev, jnp.max(s, axis=-1, keepdims=True))
    if row_on is None:
        shift = m_new
    else:
        m_new = jnp.where(row_on, m_new, m_prev)
        shift = jnp.where(row_on, m_new, jnp.inf)
    alpha = jnp.exp2(m_prev - m_new)
    p = jnp.concatenate([jnp.exp2(s[:, LANES * c:LANES * (c + 1)] - shift)
                         for c in range(s.shape[1] // LANES)], axis=1).astype(BF16)
    pv = jnp.dot(p, v_ext, preferred_element_type=F32)
    acc_sc[hh] = jnp.concatenate([alpha, alpha], axis=1) * acc_sc[hh] + pv
    m_sc[hh] = m_new


def _with_ones(v):
    return jnp.concatenate([v, jnp.ones(v.shape, v.dtype)], axis=1)


def _flash_init(m_sc, acc_sc):
    m_sc[...] = jnp.full(m_sc.shape, NEG_BIG, F32)
    acc_sc[...] = jnp.zeros(acc_sc.shape, F32)


def _flash_finish(o_ref, sg_ref, acc_sc):
    lo_half = lax.broadcasted_iota(jnp.int32, (acc_sc.shape[1], LANES), 1) < HEAD_DIM
    o = jnp.where(lo_half, acc_sc[0, :, :LANES] / acc_sc[0, :, LANES:],
                  acc_sc[1, :, :LANES] / acc_sc[1, :, LANES:])
    o_ref[0] = (o * sg_ref[0].astype(F32)).astype(o_ref.dtype)


def _mla_kernel(q_ref, k_ref, v_ref, sg_ref, o_ref, m_sc, acc_sc):
    i = pl.program_id(2)
    t = TQ_MLA
    _flash_init(m_sc, acc_sc)

    def step(kstart, diagonal):
        v = _with_ones(v_ref[0, pl.ds(kstart, t), :])
        for hh in range(2):
            sl = slice(LANES * hh, LANES * (hh + 1))
            s = lax.dot_general(q_ref[0, :, sl], k_ref[0, pl.ds(kstart, t), sl], _NT,
                                preferred_element_type=F32)
            if diagonal:
                row = lax.broadcasted_iota(jnp.int32, (t, t), 0)
                col = lax.broadcasted_iota(jnp.int32, (t, t), 1)
                s = jnp.where(col <= row, s, NEG_BIG)
            _flash_update(hh, s, v, m_sc, acc_sc)

    def body(j, carry):
        step(pl.multiple_of(j * t, t), False)
        return carry

    lax.fori_loop(0, i, body, 0)
    step(pl.multiple_of(i * t, t), True)
    _flash_finish(o_ref, sg_ref, acc_sc)


def _mla_call(qm, km, vm, sgm):
    b, s, _ = qm.shape
    t = TQ_MLA
    return pl.pallas_call(
        _mla_kernel,
        out_shape=jax.ShapeDtypeStruct((b, s, MLA_WIDTH), BF16),
        grid=(b, N_PAIRS, s // t),
        in_specs=[pl.BlockSpec((1, t, 2 * LANES), lambda bi, p, i: (bi, i, p)),
                  pl.BlockSpec((1, s, 2 * LANES), lambda bi, p, i: (bi, 0, p)),
                  pl.BlockSpec((1, s, LANES), lambda bi, p, i: (bi, 0, p)),
                  pl.BlockSpec((1, t, LANES), lambda bi, p, i: (bi, i, p))],
        out_specs=pl.BlockSpec((1, t, LANES), lambda bi, p, i: (bi, i, p)),
        scratch_shapes=[pltpu.VMEM((2, t, LANES), F32), pltpu.VMEM((2, t, 2 * LANES), F32)],
        compiler_params=pltpu.CompilerParams(
            dimension_semantics=("arbitrary", "arbitrary", "arbitrary"),
            vmem_limit_bytes=VMEM_LIMIT),
        name="mla_attention",
    )(qm, km, vm, sgm)


def _moba_kernel(ttab_ref, vtab_ref, q_ref, k_ref, v_ref, kmean_ref, bt_ref, sg_ref, o_ref,
                 m_sc, acc_sc, rc_sc, qs_sc, sel_sc):
    pair = pl.program_id(1)
    own = pl.program_id(2)
    t = MOBA_BLOCK
    nb = kmean_ref.shape[1]
    _flash_init(m_sc, acc_sc)
    rc_sc[...] = (lax.broadcasted_iota(jnp.int32, (t, t), 0)
                  - lax.broadcasted_iota(jnp.int32, (t, t), 1))

    q = q_ref[0]
    km = kmean_ref[0]
    lane = lax.broadcasted_iota(jnp.int32, (t, LANES), 1)
    col = lax.broadcasted_iota(jnp.int32, (t, nb), 1).astype(F32)
    for hh in range(2):
        qh = jnp.where((lane < HEAD_DIM) if hh == 0 else (lane >= HEAD_DIM), q, 0.0)
        qs_sc[hh] = (qh * MOBA_QSCALE).astype(BF16)
        g = lax.dot_general(qh, km, _NT, precision=lax.Precision.HIGHEST,
                            preferred_element_type=F32)
        rem = jnp.where(col < own.astype(F32), g, -jnp.inf)
        for k in range(MOBA_TOPK):
            mx = jnp.max(rem, axis=-1, keepdims=True)
            idx = jnp.min(jnp.where(rem == mx, col, float(nb)), axis=-1, keepdims=True)
            sel_sc[hh, k] = jnp.broadcast_to(jnp.where(k < own, idx, -1.0), (t, LANES))
            rem = jnp.where(col == idx, -jnp.inf, rem)

    def step(j, bias_fn, select_rows):
        kstart = pl.multiple_of(j * t, t)
        kk = k_ref[0, pl.ds(kstart, t), :]
        vv = _with_ones(v_ref[0, pl.ds(kstart, t), :])
        jf = j.astype(F32)
        for hh in range(2):
            s = lax.dot_general(qs_sc[hh], kk, _NT, preferred_element_type=F32) + bias_fn(hh)
            row_on = None
            if select_rows:
                row_on = (sel_sc[hh, 0] == jf) | (sel_sc[hh, 1] == jf) | (sel_sc[hh, 2] == jf)
            _flash_update(hh, s, vv, m_sc, acc_sc, row_on)

    def near_bias(d):
        return lambda hh: bt_ref[0, hh, d]

    def far_bias(d):
        def fn(hh):
            base = (d * H_MOBA + pair * 2 + hh) * (FAR_THRESHOLDS + 1)
            rc = rc_sc[...]
            bias = jnp.full((t, t), vtab_ref[base], F32)
            for k in range(FAR_THRESHOLDS):
                bias = jnp.where(rc >= ttab_ref[d * FAR_THRESHOLDS + k], vtab_ref[base + k + 1], bias)
            return bias
        return fn

    step(own, near_bias(0), False)
    for d in range(1, NEAR_TILES):
        @pl.when(own >= d)
        def _():
            step(own - d, near_bias(d), True)

    def body(j, carry):
        step(j, far_bias(own - j), True)
        return carry

    lax.fori_loop(0, jnp.maximum(own - (NEAR_TILES - 1), 0), body, 0)
    _flash_finish(o_ref, sg_ref, acc_sc)


def _moba_call(ttab, vtab, qb, kb, vb, kmean, bias_tiles, sgb):
    b, s, _ = qb.shape
    t = MOBA_BLOCK
    nb = s // t
    tokq = pl.BlockSpec((1, t, LANES), lambda bi, p, i, tt: (bi, i, p))
    seq = pl.BlockSpec((1, s, LANES), lambda bi, p, i, tt: (bi, 0, p))
    return pl.pallas_call(
        _moba_kernel,
        out_shape=jax.ShapeDtypeStruct((b, s, MOBA_WIDTH), BF16),
        grid_spec=pltpu.PrefetchScalarGridSpec(
            num_scalar_prefetch=1,
            grid=(b, N_PAIRS, nb),
            in_specs=[pl.BlockSpec(memory_space=pltpu.SMEM),
                      tokq, seq, seq,
                      pl.BlockSpec((1, nb, LANES), lambda bi, p, i, tt: (bi, 0, p)),
                      pl.BlockSpec((1, 2, NEAR_TILES, t, t), lambda bi, p, i, tt: (p, 0, 0, 0, 0)),
                      tokq],
            out_specs=tokq,
            scratch_shapes=[pltpu.VMEM((2, t, LANES), F32), pltpu.VMEM((2, t, 2 * LANES), F32),
                            pltpu.VMEM((t, t), jnp.int32), pltpu.VMEM((2, t, LANES), BF16),
                            pltpu.VMEM((2, MOBA_TOPK, t, LANES), F32)],
        ),
        compiler_params=pltpu.CompilerParams(
            dimension_semantics=("arbitrary", "arbitrary", "arbitrary"),
            vmem_limit_bytes=VMEM_LIMIT),
        name="moba_attention",
    )(ttab, vtab, qb, kb, vb, kmean, bias_tiles, sgb)


def _proj_out_kernel(am_ref, ab_ref, w_ref, x_ref, mod_ref, o_ref):
    y = (jnp.dot(am_ref[0], w_ref[0:MLA_WIDTH, :], preferred_element_type=F32)
         + jnp.dot(ab_ref[0], w_ref[MLA_WIDTH:, :], preferred_element_type=F32))
    o_ref[0] = x_ref[0] + mod_ref[0, 2:3, :] * y


def _proj_out_call(a_mla, a_moba, w_out, x, mod3):
    b, s, d = x.shape
    tok = lambda w: pl.BlockSpec((1, TM, w), lambda bi, i: (bi, i, 0))
    return pl.pallas_call(
        _proj_out_kernel,
        out_shape=jax.ShapeDtypeStruct((b, s, d), F32),
        grid=(b, s // TM),
        in_specs=[tok(MLA_WIDTH), tok(MOBA_WIDTH), pl.BlockSpec(w_out.shape, lambda bi, i: (0, 0)),
                  tok(d), pl.BlockSpec((1, 3, d), lambda bi, i: (bi, 0, 0))],
        out_specs=tok(d),
        compiler_params=pltpu.CompilerParams(dimension_semantics=("arbitrary", "arbitrary"),
                                             vmem_limit_bytes=VMEM_LIMIT),
        name="proj_out",
    )(a_mla, a_moba, w_out, x, mod3)


def _to_head_lanes(a):
    a = jnp.concatenate([a, jnp.zeros(a.shape[:-1] + (1,), a.dtype)], axis=-1)
    return a[..., _LANE_SRC]


def _layer_weights(w_in, w_uq, w_ukv, q_g, k_g, bq_g, bk_g):
    o0 = Q_LORA
    o1 = o0 + KV_LORA
    o2 = o1 + ROPE_DIM
    k_rope = w_in[:, o1:o2]
    kr128 = jnp.zeros((D_MODEL, LANES), w_in.dtype)
    kr128 = kr128.at[:, 0:HALF_ROPE].set(k_rope[:, :HALF_ROPE])
    kr128 = kr128.at[:, LANES // 2:LANES // 2 + HALF_ROPE].set(k_rope[:, HALF_ROPE:])
    win = jnp.concatenate([w_in[:, :o1], kr128, w_in[:, o2:]], axis=1).astype(BF16)
    wuq = _to_head_lanes(w_uq.reshape(Q_LORA, H_MLA, QK_DIM)).reshape(Q_LORA, H_MLA * LANES)
    ukv = w_ukv.reshape(KV_LORA, H_MLA, NOPE_DIM + HEAD_DIM)
    k_nope = jnp.concatenate([ukv[..., :NOPE_DIM], jnp.zeros((KV_LORA, H_MLA, ROPE_DIM), ukv.dtype)], -1)
    wuk = _to_head_lanes(k_nope).reshape(KV_LORA, H_MLA * LANES)
    wuv = ukv[..., NOPE_DIM:].reshape(KV_LORA, MLA_WIDTH)
    pair = lambda g: jnp.concatenate([g, g]).reshape(1, LANES)
    return (win, wuq.astype(BF16), wuk.astype(BF16), wuv.astype(BF16),
            _to_head_lanes(q_g).reshape(1, LANES), _to_head_lanes(k_g).reshape(1, LANES),
            pair(bq_g), pair(bk_g))


def kernel(x, c, positions, norm_g, w_ada, b_ada, w_in, mla_q_norm_g, mla_w_uq, mla_kv_norm_g,
           mla_w_ukv, mla_q_g, mla_k_g, moba_q_g, moba_k_g, w_out, rel_bias):
    b, s, d = x.shape
    depth = w_in.shape[0]
    assert d == D_MODEL and s % TM == 0 and s % TQ_MLA == 0 and TM % MOBA_BLOCK == 0 and b <= 8
    _check_far_tiles(s)
    nb = s // MOBA_BLOCK

    c8 = jnp.pad(c, ((0, 8 - b), (0, 0)))
    mod = _mod_call(c8, w_ada, b_ada.reshape(depth, 1, 3 * d))[:, :b].reshape(depth, b, 3, d)

    inv_freq = ROPE_THETA ** (-jnp.arange(0, HALF_ROPE, dtype=F32) / HALF_ROPE)
    freq = jnp.zeros((LANES,), F32).at[0:HALF_ROPE].set(inv_freq)
    freq = freq.at[LANES // 2:LANES // 2 + HALF_ROPE].set(inv_freq).reshape(1, LANES)
    sign = jnp.zeros((LANES,), F32).at[0:HALF_ROPE].set(-1.0)
    sign = sign.at[LANES // 2:LANES // 2 + HALF_ROPE].set(1.0).reshape(1, LANES)
    rope_c, rope_s = _rope_call(positions.reshape(b * s, 1), freq, sign)

    bkt = _rel_bucket(jnp.arange(s, dtype=jnp.int32))
    thr = jnp.sum(bkt[None, :] < jnp.arange(N_BUCKETS, dtype=jnp.int32)[:, None], axis=1).astype(jnp.int32)
    bias_tiles = _bias_tile_call(thr, rel_bias.astype(F32).reshape(-1))
    bias_tiles = bias_tiles.reshape(N_PAIRS, 2, NEAR_TILES, MOBA_BLOCK, MOBA_BLOCK)
    ttab, vtab = _far_tables(bkt, rel_bias, s)

    for l in range(depth):
        win, wuq, wuk, wuv, qg, kg, bqg, bkg = _layer_weights(
            w_in[l], mla_w_uq[l], mla_w_ukv[l], mla_q_g[l], mla_k_g[l], moba_q_g[l], moba_k_g[l])
        qm, km, vm, sgm, qb, kb, vb, sgb, kmean = _proj_in_call(
            x, mod[l], norm_g[l].reshape(1, d), win, mla_q_norm_g[l].reshape(1, Q_LORA), wuq,
            mla_kv_norm_g[l].reshape(1, KV_LORA), wuk, wuv, qg, kg, bqg, bkg, rope_c, rope_s)
        a_mla = _mla_call(qm, km, vm, sgm)
        a_moba = _moba_call(ttab, vtab, qb, kb, vb, kmean.reshape(b, nb, MOBA_WIDTH), bias_tiles, sgb)
        x = _proj_out_call(a_mla, a_moba, w_out[l].astype(BF16), x, mod[l])
    return x
```

```python
import functools
import math

import numpy as np
import jax
import jax.numpy as jnp
from jax import lax
from jax.experimental import pallas as pl
from jax.experimental.pallas import tpu as pltpu

F32 = jnp.float32
BF16 = jnp.bfloat16

D_MODEL = 1024
DEPTH = 2
HEAD_DIM = 64
H_MLA = 8
H_MOBA = 8
MLA_WIDTH = H_MLA * HEAD_DIM
MOBA_WIDTH = H_MOBA * HEAD_DIM
Q_LORA = 256
KV_LORA = 128
NOPE_DIM = 64
ROPE_DIM = 32
QK_DIM = NOPE_DIM + ROPE_DIM
ROPE_THETA = 10000.0
MOBA_BLOCK = 256
MOBA_TOPK = 3
N_BUCKETS = 32
REL_MAX_DIST = 4096
EPS = 1e-6

LANES = 128
HALF_ROPE = ROPE_DIM // 2
N_PAIRS = H_MLA // 2
LOG2E = math.log2(math.e)
MLA_QSCALE = QK_DIM ** -0.5 * LOG2E
MOBA_QSCALE = HEAD_DIM ** -0.5 * LOG2E
NEG_BIG = -1e30
FAR_THRESHOLDS = 2
NEAR_TILES = 3
NEVER = 1 << 20
PEN_OFF = -2.0 ** 127
SHIFT_LANE = QK_DIM
PEN_LANE = HEAD_DIM
MAX_SHIFT = 48.0
SHIFT_SLACK = 1.02

TM = 512
TQ_MLA = 512
MOBA_KEYS = 2 * MOBA_BLOCK
VMEM_LIMIT = 56 * 1024 * 1024

_NT = (((1,), (1,)), ((), ()))


def _head_lane_source():
    src = np.full((LANES,), QK_DIM, np.int32)
    src[0:16] = NOPE_DIM + np.arange(16)
    src[16:64] = np.arange(48)
    src[64:80] = NOPE_DIM + HALF_ROPE + np.arange(16)
    src[80:96] = 48 + np.arange(16)
    return src


_LANE_SRC = _head_lane_source()


def _bucket_np(n, dtype):
    n = np.asarray(n)
    max_exact = N_BUCKETS // 2
    nf = np.maximum(n, 1).astype(dtype)
    large = max_exact + (np.log(nf / dtype(max_exact)) / dtype(math.log(REL_MAX_DIST / max_exact))
                         * dtype(N_BUCKETS - max_exact)).astype(np.int32)
    large = np.minimum(large, N_BUCKETS - 1)
    return np.where(n < max_exact, n, large)


def _check_far_tiles(seq):
    n = np.arange(seq + 2)
    for dtype in (np.float32, np.float64):
        b = _bucket_np(n, dtype)
        chg = np.concatenate([[0], (b[1:] != b[:-1]).astype(np.int64)])
        for d in range(NEAR_TILES, seq // MOBA_BLOCK):
            lo, hi = MOBA_BLOCK * (d - 1) + 1, MOBA_BLOCK * (d + 1) - 1
            assert chg[lo - 1:hi + 1].sum() <= FAR_THRESHOLDS, (d, dtype)


def _rel_bucket(dist):
    n = jnp.maximum(dist, 0)
    max_exact = N_BUCKETS // 2
    nf = jnp.maximum(n, 1).astype(F32)
    large = max_exact + (jnp.log(nf / max_exact) / math.log(REL_MAX_DIST / max_exact)
                         * (N_BUCKETS - max_exact)).astype(jnp.int32)
    large = jnp.minimum(large, N_BUCKETS - 1)
    return jnp.where(n < max_exact, n, large)


def _rms(x, g):
    return x * lax.rsqrt(jnp.mean(x * x, axis=-1, keepdims=True) + EPS) * g


def _silu(x):
    return x * jax.nn.sigmoid(x)


def _mod_kernel(c_ref, w_ref, b_ref, o_ref):
    c = c_ref[...]
    o_ref[0] = jnp.dot(_silu(c), w_ref[0], precision=lax.Precision.HIGHEST,
                       preferred_element_type=F32) + b_ref[0]


def _mod_call(c8, w_ada, b_ada):
    depth, d, d3 = w_ada.shape
    nchunk = d3 // d
    return pl.pallas_call(
        _mod_kernel,
        out_shape=jax.ShapeDtypeStruct((depth, 8, d3), F32),
        grid=(depth, nchunk),
        in_specs=[pl.BlockSpec((8, d), lambda l, j: (0, 0)),
                  pl.BlockSpec((1, d, d), lambda l, j: (l, 0, j)),
                  pl.BlockSpec((1, 1, d), lambda l, j: (l, 0, j))],
        out_specs=pl.BlockSpec((1, 8, d), lambda l, j: (l, 0, j)),
        compiler_params=pltpu.CompilerParams(dimension_semantics=("arbitrary", "arbitrary"),
                                             vmem_limit_bytes=VMEM_LIMIT),
        name="adaln_mod",
    )(c8, w_ada, b_ada)


def _rope_kernel(pos_ref, freq_ref, sign_ref, c_ref, s_ref):
    ang = pos_ref[...].astype(F32) * freq_ref[...]
    c_ref[...] = jnp.cos(ang)
    s_ref[...] = jnp.sin(ang) * sign_ref[...]


def _rope_call(pos_col, freq, sign):
    n = pos_col.shape[0]
    tr = 1024
    return pl.pallas_call(
        _rope_kernel,
        out_shape=(jax.ShapeDtypeStruct((n, LANES), F32), jax.ShapeDtypeStruct((n, LANES), F32)),
        grid=(n // tr,),
        in_specs=[pl.BlockSpec((tr, 1), lambda i: (i, 0)),
                  pl.BlockSpec((1, LANES), lambda i: (0, 0)),
                  pl.BlockSpec((1, LANES), lambda i: (0, 0))],
        out_specs=(pl.BlockSpec((tr, LANES), lambda i: (i, 0)),
                   pl.BlockSpec((tr, LANES), lambda i: (i, 0))),
        compiler_params=pltpu.CompilerParams(dimension_semantics=("arbitrary",)),
        name="rope_tables",
    )(pos_col, freq, sign)


def _bias_tile_kernel(thr_ref, ttab_ref, rb_ref, vtab_ref, o_ref):
    h = pl.program_id(0)
    d = pl.program_id(1)
    shape = (MOBA_BLOCK, MOBA_BLOCK)
    rc = lax.broadcasted_iota(jnp.int32, shape, 0) - lax.broadcasted_iota(jnp.int32, shape, 1)

    @pl.when(d < NEAR_TILES)
    def _():
        dist = rc + d * MOBA_BLOCK
        val = jnp.full(shape, rb_ref[h], F32)
        for b in range(1, N_BUCKETS):
            val = jnp.where(dist >= thr_ref[b], rb_ref[b * H_MOBA + h], val)
        o_ref[0, 0] = jnp.where(dist >= 0, val * LOG2E, NEG_BIG)

    @pl.when(d >= NEAR_TILES)
    def _():
        base = (d * H_MOBA + h) * (FAR_THRESHOLDS + 1)
        val = jnp.full(shape, vtab_ref[base], F32)
        for k in range(FAR_THRESHOLDS):
            val = jnp.where(rc >= ttab_ref[d * FAR_THRESHOLDS + k], vtab_ref[base + k + 1], val)
        o_ref[0, 0] = val


def _bias_tile_call(thr, ttab, rb_flat, vtab, nb):
    smem = pl.BlockSpec(memory_space=pltpu.SMEM)
    return pl.pallas_call(
        _bias_tile_kernel,
        out_shape=jax.ShapeDtypeStruct((H_MOBA, nb, MOBA_BLOCK, MOBA_BLOCK), F32),
        grid_spec=pltpu.PrefetchScalarGridSpec(
            num_scalar_prefetch=2,
            grid=(H_MOBA, nb),
            in_specs=[smem, smem],
            out_specs=pl.BlockSpec((1, 1, MOBA_BLOCK, MOBA_BLOCK), lambda h, d, *_: (h, d, 0, 0)),
        ),
        compiler_params=pltpu.CompilerParams(dimension_semantics=("arbitrary", "arbitrary")),
        name="moba_bias_tiles",
    )(thr, ttab, rb_flat, vtab)


def _far_tables(bkt, rel_bias, seq):
    nb = seq // MOBA_BLOCK
    n = jnp.arange(seq, dtype=jnp.int32)
    chg = jnp.concatenate([jnp.zeros((1,), bool), bkt[1:] != bkt[:-1]])
    d = jnp.arange(nb, dtype=jnp.int32)[:, None]
    lo = jnp.maximum(MOBA_BLOCK * (d - 1) + 1, 0)
    hi = MOBA_BLOCK * (d + 1) - 1
    inside = chg[None, :] & (n[None, :] > lo) & (n[None, :] <= hi)
    cum = jnp.cumsum(inside.astype(jnp.int32), axis=1)
    idx = [bkt[jnp.minimum(lo[:, 0], seq - 1)]]
    thr = []
    for k in range(1, FAR_THRESHOLDS + 1):
        t = jnp.min(jnp.where(inside & (cum == k), n[None, :], NEVER), axis=1)
        thr.append(t - MOBA_BLOCK * d[:, 0])
        idx.append(bkt[jnp.minimum(t, seq - 1)])
    ttab = jnp.stack(thr, axis=1).reshape(-1).astype(jnp.int32)
    vals = rel_bias.astype(F32)[jnp.stack(idx, axis=1)] * LOG2E
    vtab = vals.transpose(0, 2, 1).reshape(-1)
    return ttab, vtab


def _mla_head(xh, g, rc, rs):
    ms = jnp.sum(xh * xh, axis=-1, keepdims=True) * (1.0 / QK_DIM)
    xn = xh * lax.rsqrt(ms + EPS) * g
    return xn * rc + pltpu.roll(xn, LANES // 2, 1) * rs


def _moba_pair_norm(x, g, lo_half):
    x2 = x * x
    s_lo = jnp.sum(jnp.where(lo_half, x2, 0.0), axis=-1, keepdims=True)
    s_hi = jnp.sum(jnp.where(lo_half, 0.0, x2), axis=-1, keepdims=True)
    ms = jnp.where(lo_half, s_lo, s_hi) * (1.0 / HEAD_DIM)
    return x * lax.rsqrt(ms + EPS) * g


def _proj_in_kernel(x_ref, mod_ref, ng_ref, win_ref, qng_ref, wuq_ref, kvng_ref, wuk_ref, wuv_ref,
                    qg_ref, kg_ref, bqg_ref, bkg_ref, rc_ref, rs_ref, qpad_ref, kpad_ref,
                    qm_ref, km_ref, vm_ref, sgm_ref, qb_ref, kb_ref, vb_ref, sgb_ref, kmean_ref):
    tile = pl.program_id(1)
    x = x_ref[0]
    h = _rms(x, ng_ref[...]) * (1.0 + mod_ref[0, 1:2, :]) + mod_ref[0, 0:1, :]
    hb = h.astype(BF16)

    def proj(lo, hi):
        return jnp.dot(hb, win_ref[:, lo:hi], preferred_element_type=F32)

    rc = rc_ref[...]
    rs = rs_ref[...]

    cqn = _rms(proj(0, 256), qng_ref[...]).astype(BF16)
    q = jnp.dot(cqn, wuq_ref[...], preferred_element_type=F32)
    qg = qg_ref[...]
    qpad = qpad_ref[...]
    kpad = kpad_ref[...]
    for hh in range(H_MLA):
        sl = slice(LANES * hh, LANES * (hh + 1))
        qm_ref[0, :, sl] = (_mla_head(q[:, sl], qg, rc, rs) * MLA_QSCALE + qpad).astype(BF16)

    ckvn = _rms(proj(256, 384), kvng_ref[...]).astype(BF16)
    kn = jnp.dot(ckvn, wuk_ref[...], preferred_element_type=F32)
    kr = proj(384, 512)
    kg = kg_ref[...]
    for hh in range(H_MLA):
        sl = slice(LANES * hh, LANES * (hh + 1))
        km_ref[0, :, sl] = (_mla_head(kn[:, sl] + kr, kg, rc, rs) + kpad).astype(BF16)
    vm_ref[0] = jnp.dot(ckvn, wuv_ref[...], preferred_element_type=F32).astype(BF16)
    sgm_ref[0] = _silu(proj(512, 1024)).astype(BF16)

    lane = lax.broadcasted_iota(jnp.int32, (TM, LANES), 1)
    lo_half = lane < HEAD_DIM
    blk_id = tile * (TM // MOBA_BLOCK) + jnp.right_shift(
        lax.broadcasted_iota(jnp.int32, (TM, LANES), 0), MOBA_BLOCK.bit_length() - 1)
    onehot = (lane == PEN_LANE + blk_id).astype(F32)
    qbz = proj(1024, 1536)
    kbz = proj(1536, 2048)
    bqg = bqg_ref[...]
    bkg = bkg_ref[...]
    for p in range(N_PAIRS):
        sl = slice(LANES * p, LANES * (p + 1))
        qb_ref[0, :, sl] = _moba_pair_norm(qbz[:, sl], bqg, lo_half)
        kn_p = _moba_pair_norm(kbz[:, sl], bkg, lo_half)
        kb_ref[0, :, LANES * 2 * p:LANES * (2 * p + 1)] = jnp.where(lo_half, kn_p, onehot).astype(BF16)
        kb_ref[0, :, LANES * (2 * p + 1):LANES * (2 * p + 2)] = jnp.where(
            lo_half, pltpu.roll(kn_p, HEAD_DIM, 1), onehot).astype(BF16)
        for blk in range(TM // MOBA_BLOCK):
            rows = kn_p[MOBA_BLOCK * blk:MOBA_BLOCK * (blk + 1)]
            kmean_ref[0, 0, blk:blk + 1, sl] = jnp.sum(rows, axis=0, keepdims=True) * (1.0 / MOBA_BLOCK)
    vb_ref[0] = proj(2048, 2560).astype(BF16)
    sgb_ref[0] = _silu(proj(2560, 3072)).astype(BF16)


def _proj_in_call(x, mod3, ng, win, qng, wuq, kvng, wuk, wuv, qg, kg, bqg, bkg, rope_c, rope_s,
                  qpad, kpad):
    b, s, d = x.shape
    nt = s // TM
    const2 = lambda bi, i: (0, 0)
    tok = lambda w: pl.BlockSpec((1, TM, w), lambda bi, i: (bi, i, 0))
    full = lambda a: pl.BlockSpec(a.shape, const2)
    rope_spec = pl.BlockSpec((TM, LANES), lambda bi, i: (bi * nt + i, 0))
    out_shape = (
        jax.ShapeDtypeStruct((b, s, H_MLA * LANES), BF16),
        jax.ShapeDtypeStruct((b, s, H_MLA * LANES), BF16),
        jax.ShapeDtypeStruct((b, s, MLA_WIDTH), BF16),
        jax.ShapeDtypeStruct((b, s, MLA_WIDTH), BF16),
        jax.ShapeDtypeStruct((b, s, MOBA_WIDTH), F32),
        jax.ShapeDtypeStruct((b, s, H_MOBA * LANES), BF16),
        jax.ShapeDtypeStruct((b, s, MOBA_WIDTH), BF16),
        jax.ShapeDtypeStruct((b, s, MOBA_WIDTH), BF16),
        jax.ShapeDtypeStruct((b, nt, TM // MOBA_BLOCK, MOBA_WIDTH), F32),
    )
    out_specs = (tok(H_MLA * LANES), tok(H_MLA * LANES), tok(MLA_WIDTH), tok(MLA_WIDTH),
                 tok(MOBA_WIDTH), tok(H_MOBA * LANES), tok(MOBA_WIDTH), tok(MOBA_WIDTH),
                 pl.BlockSpec((1, 1, TM // MOBA_BLOCK, MOBA_WIDTH), lambda bi, i: (bi, i, 0, 0)))
    return pl.pallas_call(
        _proj_in_kernel,
        out_shape=out_shape,
        grid=(b, nt),
        in_specs=[tok(d), pl.BlockSpec((1, 3, d), lambda bi, i: (bi, 0, 0)), full(ng), full(win),
                  full(qng), full(wuq), full(kvng), full(wuk), full(wuv),
                  full(qg), full(kg), full(bqg), full(bkg), rope_spec, rope_spec,
                  full(qpad), full(kpad)],
        out_specs=out_specs,
        compiler_params=pltpu.CompilerParams(dimension_semantics=("arbitrary", "arbitrary"),
                                             vmem_limit_bytes=VMEM_LIMIT),
        name="proj_in",
    )(x, mod3, ng, win, qng, wuq, kvng, wuk, wuv, qg, kg, bqg, bkg, rope_c, rope_s, qpad, kpad)


def _moba_gate_kernel(neg_shift_ref, q_ref, kmean_ref, qx_ref):
    nb = kmean_ref.shape[1]
    km = kmean_ref[0]
    q = q_ref[0]
    lo_k = lax.broadcasted_iota(jnp.int32, (nb, LANES), 1) < HEAD_DIM
    km2 = jnp.concatenate([jnp.where(lo_k, km, 0.0), jnp.where(lo_k, 0.0, km)], axis=0)
    g2 = lax.dot_general(km2, q, _NT, precision=lax.Precision.HIGHEST, preferred_element_type=F32)
    blk = lax.broadcasted_iota(jnp.int32, (nb, TM), 0)
    blkf = blk.astype(F32)
    own = pl.program_id(2) * (TM // MOBA_BLOCK) + jnp.right_shift(
        lax.broadcasted_iota(jnp.int32, (nb, TM), 1), MOBA_BLOCK.bit_length() - 1)
    neg_shift = neg_shift_ref[0]
    pens = []
    for hh in range(2):
        rem = jnp.where(blk < own, g2[nb * hh:nb * (hh + 1)], -jnp.inf)
        on = blk == own
        for k in range(MOBA_TOPK):
            mx = jnp.max(rem, axis=0, keepdims=True)
            idx = jnp.min(jnp.where(rem == mx, blkf, float(nb)), axis=0, keepdims=True)
            hit = blkf == idx
            on = on | (hit & (own > k))
            rem = jnp.where(hit, -jnp.inf, rem)
        pens.append(jnp.where(on, neg_shift, PEN_OFF).astype(BF16))
    r = lax.broadcasted_iota(jnp.int32, (2 * nb, 2 * LANES), 0)
    to_lane = (lax.broadcasted_iota(jnp.int32, (2 * nb, 2 * LANES), 1)
               == jnp.where(r < nb, PEN_LANE + r, LANES + PEN_LANE + r - nb)).astype(BF16)
    pen_lanes = lax.dot_general(jnp.concatenate(pens, axis=0), to_lane, (((0,), (0,)), ((), ())),
                                preferred_element_type=F32)
    lo_q = lax.broadcasted_iota(jnp.int32, (TM, LANES), 1) < HEAD_DIM
    qs = q * MOBA_QSCALE
    qs2 = jnp.concatenate([jnp.where(lo_q, qs, 0.0),
                           pltpu.roll(jnp.where(lo_q, 0.0, qs), HEAD_DIM, 1)], axis=1)
    qx_ref[0] = (qs2 + pen_lanes).astype(BF16)


def _moba_gate_call(neg_shift, qb, kmean):
    b, s, _ = qb.shape
    nb = kmean.shape[1]
    return pl.pallas_call(
        _moba_gate_kernel,
        out_shape=jax.ShapeDtypeStruct((b, s, H_MOBA * LANES), BF16),
        grid=(b, N_PAIRS, s // TM),
        in_specs=[pl.BlockSpec(memory_space=pltpu.SMEM),
                  pl.BlockSpec((1, TM, LANES), lambda bi, p, i: (bi, i, p)),
                  pl.BlockSpec((1, nb, LANES), lambda bi, p, i: (bi, 0, p))],
        out_specs=pl.BlockSpec((1, TM, 2 * LANES), lambda bi, p, i: (bi, i, p)),
        compiler_params=pltpu.CompilerParams(
            dimension_semantics=("arbitrary", "arbitrary", "arbitrary"),
            vmem_limit_bytes=VMEM_LIMIT),
        name="moba_gate",
    )(neg_shift, qb, kmean)


def _flash_update(hh, s, v_ext, m_sc, acc_sc, fixed_shift):
    if fixed_shift:
        acc_sc[hh] += jnp.dot(jnp.exp2(s).astype(BF16), v_ext, preferred_element_type=F32)
        return
    m_prev = m_sc[hh]
    m_new = jnp.maximum(m_prev, jnp.max(s, axis=-1, keepdims=True))
    alpha = jnp.exp2(m_prev - m_new)
    p = jnp.concatenate([jnp.exp2(s[:, LANES * c:LANES * (c + 1)] - m_new)
                         for c in range(s.shape[1] // LANES)], axis=1).astype(BF16)
    pv = jnp.dot(p, v_ext, preferred_element_type=F32)
    acc_sc[hh] = jnp.concatenate([alpha, alpha], axis=1) * acc_sc[hh] + pv
    m_sc[hh] = m_new


def _with_ones(v):
    return jnp.concatenate([v, jnp.ones(v.shape, v.dtype)], axis=1)


def _flash_init(m_sc, acc_sc):
    m_sc[...] = jnp.full(m_sc.shape, NEG_BIG, F32)
    acc_sc[...] = jnp.zeros(acc_sc.shape, F32)


def _flash_finish(o_ref, sg_ref, acc_sc):
    lo_half = lax.broadcasted_iota(jnp.int32, (acc_sc.shape[1], LANES), 1) < HEAD_DIM
    o = jnp.where(lo_half, acc_sc[0, :, :LANES] / acc_sc[0, :, LANES:],
                  acc_sc[1, :, :LANES] / acc_sc[1, :, LANES:])
    o_ref[0] = (o * sg_ref[0].astype(F32)).astype(o_ref.dtype)


def _pipelined(n, produce, consume, bufs, produce_last=None):
    a, b = bufs
    produce_last = produce_last or produce
    last = n - 1

    @pl.when(n == 1)
    def _():
        produce_last(0, a)
        consume(0, a)

    @pl.when(n > 1)
    def _():
        produce(0, a)
        n2 = lax.shift_right_logical(n - 2, 1)

        def body(it, carry):
            u = 2 * it + 1
            produce(u, b)
            consume(u - 1, a)
            produce(u + 1, a)
            consume(u, b)
            return carry

        lax.fori_loop(0, n2, body, 0)

        @pl.when(last == 2 * n2 + 2)
        def _():
            u = last - 1
            produce(u, b)
            consume(u - 1, a)
            produce_last(last, a)
            consume(u, b)
            consume(last, a)

        @pl.when(last == 2 * n2 + 1)
        def _():
            produce_last(last, b)
            consume(last - 1, a)
            consume(last, b)


def _mla_kernel(q_ref, k_ref, v_ref, sg_ref, o_ref, m_sc, acc_sc, sa_sc, sb_sc, *, fixed_shift):
    i = pl.program_id(2)
    t = TQ_MLA
    _flash_init(m_sc, acc_sc)

    def produce(u, buf, diagonal=False):
        kstart = pl.multiple_of(u * t, t)
        for hh in range(2):
            sl = slice(LANES * hh, LANES * (hh + 1))
            s = lax.dot_general(q_ref[0, :, sl], k_ref[0, pl.ds(kstart, t), sl], _NT,
                                preferred_element_type=F32)
            if diagonal:
                row = lax.broadcasted_iota(jnp.int32, (t, t), 0)
                col = lax.broadcasted_iota(jnp.int32, (t, t), 1)
                s = jnp.where(col <= row, s, NEG_BIG)
            buf[hh] = s

    def consume(u, buf):
        v = _with_ones(v_ref[0, pl.ds(pl.multiple_of(u * t, t), t), :])
        for hh in range(2):
            _flash_update(hh, buf[hh], v, m_sc, acc_sc, fixed_shift)

    _pipelined(i + 1, produce, consume, (sa_sc, sb_sc),
               produce_last=functools.partial(produce, diagonal=True))
    _flash_finish(o_ref, sg_ref, acc_sc)


def _mla_call(qm, km, vm, sgm, fixed_shift):
    b, s, _ = qm.shape
    t = TQ_MLA
    return pl.pallas_call(
        functools.partial(_mla_kernel, fixed_shift=fixed_shift),
        out_shape=jax.ShapeDtypeStruct((b, s, MLA_WIDTH), BF16),
        grid=(b, N_PAIRS, s // t),
        in_specs=[pl.BlockSpec((1, t, 2 * LANES), lambda bi, p, i: (bi, i, p)),
                  pl.BlockSpec((1, s, 2 * LANES), lambda bi, p, i: (bi, 0, p)),
                  pl.BlockSpec((1, s, LANES), lambda bi, p, i: (bi, 0, p)),
                  pl.BlockSpec((1, t, LANES), lambda bi, p, i: (bi, i, p))],
        out_specs=pl.BlockSpec((1, t, LANES), lambda bi, p, i: (bi, i, p)),
        scratch_shapes=[pltpu.VMEM((2, t, LANES), F32), pltpu.VMEM((2, t, 2 * LANES), F32),
                        pltpu.VMEM((2, t, t), F32), pltpu.VMEM((2, t, t), F32)],
        compiler_params=pltpu.CompilerParams(
            dimension_semantics=("arbitrary", "arbitrary", "arbitrary"),
            vmem_limit_bytes=VMEM_LIMIT),
        name="mla_attention",
    )(qm, km, vm, sgm)


def _moba_kernel(qx_ref, k_ref, v_ref, bt_ref, sg_ref, o_ref, m_sc, acc_sc, sa_sc, sb_sc, *,
                 fixed_shift):
    own = pl.program_id(2)
    t = MOBA_BLOCK
    w = MOBA_KEYS
    _flash_init(m_sc, acc_sc)

    def produce(u, buf):
        kstart = pl.multiple_of(u * w, w)
        for hh in range(2):
            sl = slice(LANES * hh, LANES * (hh + 1))
            s = lax.dot_general(qx_ref[0, :, sl], k_ref[0, pl.ds(kstart, w), sl], _NT,
                                preferred_element_type=F32)
            for half in range(w // t):
                d = jnp.maximum(own - ((w // t) * u + half), 0)
                buf[hh, :, t * half:t * (half + 1)] = s[:, t * half:t * (half + 1)] + bt_ref[0, hh, d]

    def consume(u, buf):
        v = _with_ones(v_ref[0, pl.ds(pl.multiple_of(u * w, w), w), :])
        for hh in range(2):
            _flash_update(hh, buf[hh], v, m_sc, acc_sc, fixed_shift)

    _pipelined(lax.shift_right_logical(own, 1) + 1, produce, consume, (sa_sc, sb_sc))
    _flash_finish(o_ref, sg_ref, acc_sc)


def _moba_call(qx, kb, vb, bias_tiles, sgb, fixed_shift):
    b, s, _ = vb.shape
    t = MOBA_BLOCK
    nb = s // t
    tok = lambda w: pl.BlockSpec((1, t, w), lambda bi, p, i: (bi, i, p))
    return pl.pallas_call(
        functools.partial(_moba_kernel, fixed_shift=fixed_shift),
        out_shape=jax.ShapeDtypeStruct((b, s, MOBA_WIDTH), BF16),
        grid=(b, N_PAIRS, nb),
        in_specs=[tok(2 * LANES),
                  pl.BlockSpec((1, s, 2 * LANES), lambda bi, p, i: (bi, 0, p)),
                  pl.BlockSpec((1, s, LANES), lambda bi, p, i: (bi, 0, p)),
                  pl.BlockSpec((1, 2, nb, t, t), lambda bi, p, i: (p, 0, 0, 0, 0),
                               pipeline_mode=pl.Buffered(1)),
                  tok(LANES)],
        out_specs=tok(LANES),
        scratch_shapes=[pltpu.VMEM((2, t, LANES), F32), pltpu.VMEM((2, t, 2 * LANES), F32),
                        pltpu.VMEM((2, t, MOBA_KEYS), F32), pltpu.VMEM((2, t, MOBA_KEYS), F32)],
        compiler_params=pltpu.CompilerParams(
            dimension_semantics=("arbitrary", "arbitrary", "arbitrary"),
            vmem_limit_bytes=VMEM_LIMIT),
        name="moba_attention",
    )(qx, kb, vb, bias_tiles, sgb)


def _proj_out_kernel(am_ref, ab_ref, w_ref, x_ref, mod_ref, o_ref):
    y = (jnp.dot(am_ref[0], w_ref[0:MLA_WIDTH, :], preferred_element_type=F32)
         + jnp.dot(ab_ref[0], w_ref[MLA_WIDTH:, :], preferred_element_type=F32))
    o_ref[0] = x_ref[0] + mod_ref[0, 2:3, :] * y


def _proj_out_call(a_mla, a_moba, w_out, x, mod3):
    b, s, d = x.shape
    tok = lambda w: pl.BlockSpec((1, TM, w), lambda bi, i: (bi, i, 0))
    return pl.pallas_call(
        _proj_out_kernel,
        out_shape=jax.ShapeDtypeStruct((b, s, d), F32),
        grid=(b, s // TM),
        in_specs=[tok(MLA_WIDTH), tok(MOBA_WIDTH), pl.BlockSpec(w_out.shape, lambda bi, i: (0, 0)),
                  tok(d), pl.BlockSpec((1, 3, d), lambda bi, i: (bi, 0, 0))],
        out_specs=tok(d),
        compiler_params=pltpu.CompilerParams(dimension_semantics=("arbitrary", "arbitrary"),
                                             vmem_limit_bytes=VMEM_LIMIT),
        name="proj_out",
    )(a_mla, a_moba, w_out, x, mod3)


def _to_head_lanes(a):
    a = jnp.concatenate([a, jnp.zeros(a.shape[:-1] + (1,), a.dtype)], axis=-1)
    return a[..., _LANE_SRC]


def _layer_weights(w_in, w_uq, w_ukv, q_g, k_g, bq_g, bk_g):
    o0 = Q_LORA
    o1 = o0 + KV_LORA
    o2 = o1 + ROPE_DIM
    k_rope = w_in[:, o1:o2]
    kr128 = jnp.zeros((D_MODEL, LANES), w_in.dtype)
    kr128 = kr128.at[:, 0:HALF_ROPE].set(k_rope[:, :HALF_ROPE])
    kr128 = kr128.at[:, LANES // 2:LANES // 2 + HALF_ROPE].set(k_rope[:, HALF_ROPE:])
    win = jnp.concatenate([w_in[:, :o1], kr128, w_in[:, o2:]], axis=1).astype(BF16)
    wuq = _to_head_lanes(w_uq.reshape(Q_LORA, H_MLA, QK_DIM)).reshape(Q_LORA, H_MLA * LANES)
    ukv = w_ukv.reshape(KV_LORA, H_MLA, NOPE_DIM + HEAD_DIM)
    k_nope = jnp.concatenate([ukv[..., :NOPE_DIM], jnp.zeros((KV_LORA, H_MLA, ROPE_DIM), ukv.dtype)], -1)
    wuk = _to_head_lanes(k_nope).reshape(KV_LORA, H_MLA * LANES)
    wuv = ukv[..., NOPE_DIM:].reshape(KV_LORA, MLA_WIDTH)
    pair = lambda g: jnp.concatenate([g, g]).reshape(1, LANES)
    return (win, wuq.astype(BF16), wuk.astype(BF16), wuv.astype(BF16),
            _to_head_lanes(q_g).reshape(1, LANES), _to_head_lanes(k_g).reshape(1, LANES),
            pair(bq_g), pair(bk_g))


def kernel(x, c, positions, norm_g, w_ada, b_ada, w_in, mla_q_norm_g, mla_w_uq, mla_kv_norm_g,
           mla_w_ukv, mla_q_g, mla_k_g, moba_q_g, moba_k_g, w_out, rel_bias):
    b, s, d = x.shape
    depth = w_in.shape[0]
    nb = s // MOBA_BLOCK
    assert d == D_MODEL and s % TM == 0 and s % TQ_MLA == 0 and TM % MOBA_BLOCK == 0 and b <= 8
    assert nb % 2 == 0 and nb <= LANES // 4
    _check_far_tiles(s)

    c8 = jnp.pad(c, ((0, 8 - b), (0, 0)))
    mod = _mod_call(c8, w_ada, b_ada.reshape(depth, 1, 3 * d))[:, :b].reshape(depth, b, 3, d)

    inv_freq = ROPE_THETA ** (-jnp.arange(0, HALF_ROPE, dtype=F32) / HALF_ROPE)
    freq = jnp.zeros((LANES,), F32).at[0:HALF_ROPE].set(inv_freq)
    freq = freq.at[LANES // 2:LANES // 2 + HALF_ROPE].set(inv_freq).reshape(1, LANES)
    sign = jnp.zeros((LANES,), F32).at[0:HALF_ROPE].set(-1.0)
    sign = sign.at[LANES // 2:LANES // 2 + HALF_ROPE].set(1.0).reshape(1, LANES)
    rope_c, rope_s = _rope_call(positions.reshape(b * s, 1), freq, sign)

    bkt = _rel_bucket(jnp.arange(s, dtype=jnp.int32))
    thr = jnp.sum(bkt[None, :] < jnp.arange(N_BUCKETS, dtype=jnp.int32)[:, None], axis=1).astype(jnp.int32)
    ttab, vtab = _far_tables(bkt, rel_bias, s)
    bias_tiles = _bias_tile_call(thr, ttab, rel_bias.astype(F32).reshape(-1), vtab, nb)
    bias_tiles = bias_tiles.reshape(N_PAIRS, 2, nb, MOBA_BLOCK, MOBA_BLOCK)
    kpad = jnp.zeros((1, LANES), F32).at[0, SHIFT_LANE].set(1.0)
    amax = lambda a: jnp.max(jnp.abs(a.astype(F32)))

    def attention(fixed_shift, qm, km, vm, sgm, qx, kb, vb, sgb):
        return (_mla_call(qm, km, vm, sgm, fixed_shift),
                _moba_call(qx, kb, vb, bias_tiles, sgb, fixed_shift))

    for l in range(depth):
        win, wuq, wuk, wuv, qg, kg, bqg, bkg = _layer_weights(
            w_in[l], mla_w_uq[l], mla_w_ukv[l], mla_q_g[l], mla_k_g[l], moba_q_g[l], moba_k_g[l])
        bound_mla = QK_DIM * amax(mla_q_g[l]) * amax(mla_k_g[l]) * MLA_QSCALE * SHIFT_SLACK
        bound_moba = (HEAD_DIM * amax(moba_q_g[l]) * amax(moba_k_g[l]) * MOBA_QSCALE
                      + amax(rel_bias) * LOG2E) * SHIFT_SLACK
        fixed_ok = (bound_mla <= MAX_SHIFT) & (bound_moba <= MAX_SHIFT)
        qpad = jnp.zeros((1, LANES), F32).at[0, SHIFT_LANE].set(jnp.where(fixed_ok, -bound_mla, 0.0))
        qm, km, vm, sgm, qb, kb, vb, sgb, kmean = _proj_in_call(
            x, mod[l], norm_g[l].reshape(1, d), win, mla_q_norm_g[l].reshape(1, Q_LORA), wuq,
            mla_kv_norm_g[l].reshape(1, KV_LORA), wuk, wuv, qg, kg, bqg, bkg, rope_c, rope_s,
            qpad, kpad)
        qx = _moba_gate_call(jnp.where(fixed_ok, -bound_moba, 0.0).reshape(1), qb,
                             kmean.reshape(b, nb, MOBA_WIDTH))
        a_mla, a_moba = lax.cond(
            fixed_ok, functools.partial(attention, True), functools.partial(attention, False),
            qm, km, vm, sgm, qx, kb, vb, sgb)
        x = _proj_out_call(a_mla, a_moba, w_out[l].astype(BF16), x, mod[l])
    return x
```

```python
import functools
import math

import numpy as np
import jax
import jax.numpy as jnp
from jax import lax
from jax.experimental import pallas as pl
from jax.experimental.pallas import tpu as pltpu

F32 = jnp.float32
BF16 = jnp.bfloat16

D_MODEL = 1024
DEPTH = 2
HEAD_DIM = 64
H_MLA = 8
H_MOBA = 8
MLA_WIDTH = H_MLA * HEAD_DIM
MOBA_WIDTH = H_MOBA * HEAD_DIM
Q_LORA = 256
KV_LORA = 128
NOPE_DIM = 64
ROPE_DIM = 32
QK_DIM = NOPE_DIM + ROPE_DIM
ROPE_THETA = 10000.0
MOBA_BLOCK = 256
MOBA_TOPK = 3
N_BUCKETS = 32
REL_MAX_DIST = 4096
EPS = 1e-6

LANES = 128
HALF_ROPE = ROPE_DIM // 2
N_PAIRS = H_MLA // 2
LOG2E = math.log2(math.e)
MLA_QSCALE = QK_DIM ** -0.5 * LOG2E
MOBA_QSCALE = HEAD_DIM ** -0.5 * LOG2E
NEG_BIG = -1e30
FAR_THRESHOLDS = 2
NEAR_TILES = 3
NEVER = 1 << 20
PEN_OFF = -2.0 ** 127
SHIFT_LANE = QK_DIM
PEN_LANE = HEAD_DIM
MAX_SHIFT = 48.0
SHIFT_SLACK = 1.02

TM = 256
TG = 512
TQ_MLA = 512
MOBA_TILE = 2 * MOBA_BLOCK
VMEM_LIMIT = 56 * 1024 * 1024

_NT = (((1,), (1,)), ((), ()))


def _head_lane_source():
    src = np.full((LANES,), QK_DIM, np.int32)
    src[0:16] = NOPE_DIM + np.arange(16)
    src[16:64] = np.arange(48)
    src[64:80] = NOPE_DIM + HALF_ROPE + np.arange(16)
    src[80:96] = 48 + np.arange(16)
    return src


_LANE_SRC = _head_lane_source()


def _bucket_np(n, dtype):
    n = np.asarray(n)
    max_exact = N_BUCKETS // 2
    nf = np.maximum(n, 1).astype(dtype)
    large = max_exact + (np.log(nf / dtype(max_exact)) / dtype(math.log(REL_MAX_DIST / max_exact))
                         * dtype(N_BUCKETS - max_exact)).astype(np.int32)
    large = np.minimum(large, N_BUCKETS - 1)
    return np.where(n < max_exact, n, large)


def _check_far_tiles(seq):
    n = np.arange(seq + 2)
    for dtype in (np.float32, np.float64):
        b = _bucket_np(n, dtype)
        chg = np.concatenate([[0], (b[1:] != b[:-1]).astype(np.int64)])
        for d in range(NEAR_TILES, seq // MOBA_BLOCK):
            lo, hi = MOBA_BLOCK * (d - 1) + 1, MOBA_BLOCK * (d + 1) - 1
            assert chg[lo - 1:hi + 1].sum() <= FAR_THRESHOLDS, (d, dtype)


def _rel_bucket(dist):
    n = jnp.maximum(dist, 0)
    max_exact = N_BUCKETS // 2
    nf = jnp.maximum(n, 1).astype(F32)
    large = max_exact + (jnp.log(nf / max_exact) / math.log(REL_MAX_DIST / max_exact)
                         * (N_BUCKETS - max_exact)).astype(jnp.int32)
    large = jnp.minimum(large, N_BUCKETS - 1)
    return jnp.where(n < max_exact, n, large)


def _rms(x, g):
    return x * lax.rsqrt(jnp.mean(x * x, axis=-1, keepdims=True) + EPS) * g


def _silu(x):
    return x * jax.nn.sigmoid(x)


def _mod_kernel(c_ref, w_ref, b_ref, o_ref):
    c = c_ref[...]
    o_ref[0] = jnp.dot(_silu(c), w_ref[0], precision=lax.Precision.HIGHEST,
                       preferred_element_type=F32) + b_ref[0]


def _mod_call(c8, w_ada, b_ada):
    depth, d, d3 = w_ada.shape
    nchunk = d3 // d
    return pl.pallas_call(
        _mod_kernel,
        out_shape=jax.ShapeDtypeStruct((depth, 8, d3), F32),
        grid=(depth, nchunk),
        in_specs=[pl.BlockSpec((8, d), lambda l, j: (0, 0)),
                  pl.BlockSpec((1, d, d), lambda l, j: (l, 0, j)),
                  pl.BlockSpec((1, 1, d), lambda l, j: (l, 0, j))],
        out_specs=pl.BlockSpec((1, 8, d), lambda l, j: (l, 0, j)),
        compiler_params=pltpu.CompilerParams(dimension_semantics=("arbitrary", "arbitrary"),
                                             vmem_limit_bytes=VMEM_LIMIT),
        name="adaln_mod",
    )(c8, w_ada, b_ada)


def _rope_kernel(pos_ref, freq_ref, sign_ref, c_ref, s_ref):
    ang = pos_ref[...].astype(F32) * freq_ref[...]
    c_ref[...] = jnp.cos(ang)
    s_ref[...] = jnp.sin(ang) * sign_ref[...]


def _rope_call(pos_col, freq, sign):
    n = pos_col.shape[0]
    tr = 1024
    return pl.pallas_call(
        _rope_kernel,
        out_shape=(jax.ShapeDtypeStruct((n, LANES), F32), jax.ShapeDtypeStruct((n, LANES), F32)),
        grid=(n // tr,),
        in_specs=[pl.BlockSpec((tr, 1), lambda i: (i, 0)),
                  pl.BlockSpec((1, LANES), lambda i: (0, 0)),
                  pl.BlockSpec((1, LANES), lambda i: (0, 0))],
        out_specs=(pl.BlockSpec((tr, LANES), lambda i: (i, 0)),
                   pl.BlockSpec((tr, LANES), lambda i: (i, 0))),
        compiler_params=pltpu.CompilerParams(dimension_semantics=("arbitrary",)),
        name="rope_tables",
    )(pos_col, freq, sign)


def _bias_tile_kernel(thr_ref, ttab_ref, rb_ref, vtab_ref, o_ref):
    h = pl.program_id(0)
    nb = o_ref.shape[1]
    shape = (MOBA_BLOCK, MOBA_BLOCK)
    rc = lax.broadcasted_iota(jnp.int32, shape, 0) - lax.broadcasted_iota(jnp.int32, shape, 1)

    for d in range(NEAR_TILES):
        dist = rc + d * MOBA_BLOCK
        val = jnp.full(shape, rb_ref[h], F32)
        for b in range(1, N_BUCKETS):
            val = jnp.where(dist >= thr_ref[b], rb_ref[b * H_MOBA + h], val)
        o_ref[0, d] = jnp.where(dist >= 0, val * LOG2E, NEG_BIG)

    def far(d, carry):
        base = (d * H_MOBA + h) * (FAR_THRESHOLDS + 1)
        val = jnp.full(shape, vtab_ref[base], F32)
        for k in range(FAR_THRESHOLDS):
            val = jnp.where(rc >= ttab_ref[d * FAR_THRESHOLDS + k], vtab_ref[base + k + 1], val)
        o_ref[0, d] = val
        return carry

    lax.fori_loop(NEAR_TILES, nb, far, 0)


def _bias_tile_call(thr, ttab, rb_flat, vtab, nb):
    smem = pl.BlockSpec(memory_space=pltpu.SMEM)
    return pl.pallas_call(
        _bias_tile_kernel,
        out_shape=jax.ShapeDtypeStruct((H_MOBA, nb, MOBA_BLOCK, MOBA_BLOCK), F32),
        grid_spec=pltpu.PrefetchScalarGridSpec(
            num_scalar_prefetch=2,
            grid=(H_MOBA,),
            in_specs=[smem, smem],
            out_specs=pl.BlockSpec((1, nb, MOBA_BLOCK, MOBA_BLOCK), lambda h, *_: (h, 0, 0, 0)),
        ),
        compiler_params=pltpu.CompilerParams(dimension_semantics=("arbitrary",),
                                             vmem_limit_bytes=VMEM_LIMIT),
        name="moba_bias_tiles",
    )(thr, ttab, rb_flat, vtab)


def _far_tables(bkt, rel_bias, seq):
    nb = seq // MOBA_BLOCK
    n = jnp.arange(seq, dtype=jnp.int32)
    chg = jnp.concatenate([jnp.zeros((1,), bool), bkt[1:] != bkt[:-1]])
    d = jnp.arange(nb, dtype=jnp.int32)[:, None]
    lo = jnp.maximum(MOBA_BLOCK * (d - 1) + 1, 0)
    hi = MOBA_BLOCK * (d + 1) - 1
    inside = chg[None, :] & (n[None, :] > lo) & (n[None, :] <= hi)
    cum = jnp.cumsum(inside.astype(jnp.int32), axis=1)
    idx = [bkt[jnp.minimum(lo[:, 0], seq - 1)]]
    thr = []
    for k in range(1, FAR_THRESHOLDS + 1):
        t = jnp.min(jnp.where(inside & (cum == k), n[None, :], NEVER), axis=1)
        thr.append(t - MOBA_BLOCK * d[:, 0])
        idx.append(bkt[jnp.minimum(t, seq - 1)])
    ttab = jnp.stack(thr, axis=1).reshape(-1).astype(jnp.int32)
    vals = rel_bias.astype(F32)[jnp.stack(idx, axis=1)] * LOG2E
    vtab = vals.transpose(0, 2, 1).reshape(-1)
    return ttab, vtab


def _mla_head(xh, g, rc, rs):
    ms = jnp.sum(xh * xh, axis=-1, keepdims=True) * (1.0 / QK_DIM)
    xn = xh * lax.rsqrt(ms + EPS) * g
    return xn * rc + pltpu.roll(xn, LANES // 2, 1) * rs


def _moba_pair_norm(x, g, lo_half):
    x2 = x * x
    s_lo = jnp.sum(jnp.where(lo_half, x2, 0.0), axis=-1, keepdims=True)
    s_hi = jnp.sum(jnp.where(lo_half, 0.0, x2), axis=-1, keepdims=True)
    ms = jnp.where(lo_half, s_lo, s_hi) * (1.0 / HEAD_DIM)
    return x * lax.rsqrt(ms + EPS) * g


def _proj_in_kernel(x_ref, mod_ref, ng_ref, win_ref, qng_ref, wuq_ref, kvng_ref, wuk_ref, wuv_ref,
                    qg_ref, kg_ref, bqg_ref, bkg_ref, rc_ref, rs_ref, qpad_ref, kpad_ref,
                    qm_ref, km_ref, vm_ref, sgm_ref, qb_ref, kb_ref, vb_ref, sgb_ref, kmean_ref):
    tile = pl.program_id(1)
    x = x_ref[0]
    h = _rms(x, ng_ref[...] * (1.0 + mod_ref[0, 1:2, :])) + mod_ref[0, 0:1, :]
    hb = h.astype(BF16)

    def proj(lo, hi):
        return jnp.dot(hb, win_ref[:, lo:hi], preferred_element_type=F32)

    rc = rc_ref[...]
    rs = rs_ref[...]

    cqn = _rms(proj(0, 256), qng_ref[...]).astype(BF16)
    q = jnp.dot(cqn, wuq_ref[...], preferred_element_type=F32)
    qg = qg_ref[...]
    qpad = qpad_ref[...]
    kpad = kpad_ref[...]
    for hh in range(H_MLA):
        sl = slice(LANES * hh, LANES * (hh + 1))
        qm_ref[0, :, sl] = (_mla_head(q[:, sl], qg, rc, rs) + qpad).astype(BF16)

    ckvn = _rms(proj(256, 384), kvng_ref[...]).astype(BF16)
    kn = jnp.dot(ckvn, wuk_ref[...], preferred_element_type=F32)
    kr = proj(384, 512)
    kg = kg_ref[...]
    for hh in range(H_MLA):
        sl = slice(LANES * hh, LANES * (hh + 1))
        km_ref[0, :, sl] = (_mla_head(kn[:, sl] + kr, kg, rc, rs) + kpad).astype(BF16)
    vm_ref[0] = jnp.dot(ckvn, wuv_ref[...], preferred_element_type=F32).astype(BF16)
    sgm_ref[0] = _silu(proj(512, 1024)).astype(BF16)

    lane = lax.broadcasted_iota(jnp.int32, (TM, LANES), 1)
    lo_half = lane < HEAD_DIM
    blk_id = tile * (TM // MOBA_BLOCK) + jnp.right_shift(
        lax.broadcasted_iota(jnp.int32, (TM, LANES), 0), MOBA_BLOCK.bit_length() - 1)
    onehot = (lane == PEN_LANE + blk_id).astype(F32)
    qbz = proj(1024, 1536)
    kbz = proj(1536, 2048)
    bqg = bqg_ref[...]
    bkg = bkg_ref[...]
    for p in range(N_PAIRS):
        sl = slice(LANES * p, LANES * (p + 1))
        qb_ref[0, :, sl] = _moba_pair_norm(qbz[:, sl], bqg, lo_half)
        kn_p = _moba_pair_norm(kbz[:, sl], bkg, lo_half)
        kb_ref[0, :, LANES * 2 * p:LANES * (2 * p + 1)] = jnp.where(lo_half, kn_p, onehot).astype(BF16)
        kb_ref[0, :, LANES * (2 * p + 1):LANES * (2 * p + 2)] = jnp.where(
            lo_half, pltpu.roll(kn_p, HEAD_DIM, 1), onehot).astype(BF16)
        for blk in range(TM // MOBA_BLOCK):
            rows = kn_p[MOBA_BLOCK * blk:MOBA_BLOCK * (blk + 1)]
            kmean_ref[0, 0, blk:blk + 1, sl] = jnp.sum(rows, axis=0, keepdims=True) * (1.0 / MOBA_BLOCK)
    vb_ref[0] = proj(2048, 2560).astype(BF16)
    sgb_ref[0] = _silu(proj(2560, 3072)).astype(BF16)


def _proj_in_call(x, mod3, ng, win, qng, wuq, kvng, wuk, wuv, qg, kg, bqg, bkg, rope_c, rope_s,
                  qpad, kpad):
    b, s, d = x.shape
    nt = s // TM
    const2 = lambda bi, i: (0, 0)
    tok = lambda w: pl.BlockSpec((1, TM, w), lambda bi, i: (bi, i, 0))
    full = lambda a: pl.BlockSpec(a.shape, const2)
    rope_spec = pl.BlockSpec((TM, LANES), lambda bi, i: (bi * nt + i, 0))
    out_shape = (
        jax.ShapeDtypeStruct((b, s, H_MLA * LANES), BF16),
        jax.ShapeDtypeStruct((b, s, H_MLA * LANES), BF16),
        jax.ShapeDtypeStruct((b, s, MLA_WIDTH), BF16),
        jax.ShapeDtypeStruct((b, s, MLA_WIDTH), BF16),
        jax.ShapeDtypeStruct((b, s, MOBA_WIDTH), F32),
        jax.ShapeDtypeStruct((b, s, H_MOBA * LANES), BF16),
        jax.ShapeDtypeStruct((b, s, MOBA_WIDTH), BF16),
        jax.ShapeDtypeStruct((b, s, MOBA_WIDTH), BF16),
        jax.ShapeDtypeStruct((b, nt, TM // MOBA_BLOCK, MOBA_WIDTH), F32),
    )
    out_specs = (tok(H_MLA * LANES), tok(H_MLA * LANES), tok(MLA_WIDTH), tok(MLA_WIDTH),
                 tok(MOBA_WIDTH), tok(H_MOBA * LANES), tok(MOBA_WIDTH), tok(MOBA_WIDTH),
                 pl.BlockSpec((1, 1, TM // MOBA_BLOCK, MOBA_WIDTH), lambda bi, i: (bi, i, 0, 0)))
    return pl.pallas_call(
        _proj_in_kernel,
        out_shape=out_shape,
        grid=(b, nt),
        in_specs=[tok(d), pl.BlockSpec((1, 3, d), lambda bi, i: (bi, 0, 0)), full(ng), full(win),
                  full(qng), full(wuq), full(kvng), full(wuk), full(wuv),
                  full(qg), full(kg), full(bqg), full(bkg), rope_spec, rope_spec,
                  full(qpad), full(kpad)],
        out_specs=out_specs,
        compiler_params=pltpu.CompilerParams(dimension_semantics=("arbitrary", "arbitrary"),
                                             vmem_limit_bytes=VMEM_LIMIT),
        name="proj_in",
    )(x, mod3, ng, win, qng, wuq, kvng, wuk, wuv, qg, kg, bqg, bkg, rope_c, rope_s, qpad, kpad)


def _moba_gate_kernel(neg_shift_ref, q_ref, kmean_ref, qx_ref):
    nb = kmean_ref.shape[1]
    km = kmean_ref[0]
    q = q_ref[0]
    lo_k = lax.broadcasted_iota(jnp.int32, (nb, LANES), 1) < HEAD_DIM
    km2 = jnp.concatenate([jnp.where(lo_k, km, 0.0), jnp.where(lo_k, 0.0, km)], axis=0)
    g2 = lax.dot_general(km2, q, _NT, precision=lax.Precision.HIGHEST, preferred_element_type=F32)
    blk = lax.broadcasted_iota(jnp.int32, (nb, TG), 0)
    blkf = blk.astype(F32)
    own = pl.program_id(2) * (TG // MOBA_BLOCK) + jnp.right_shift(
        lax.broadcasted_iota(jnp.int32, (nb, TG), 1), MOBA_BLOCK.bit_length() - 1)
    neg_shift = neg_shift_ref[0]
    pens = []
    for hh in range(2):
        rem = jnp.where(blk < own, g2[nb * hh:nb * (hh + 1)], -jnp.inf)
        on = blk == own
        for k in range(MOBA_TOPK):
            mx = jnp.max(rem, axis=0, keepdims=True)
            idx = jnp.min(jnp.where(rem == mx, blkf, float(nb)), axis=0, keepdims=True)
            hit = blkf == idx
            on = on | (hit & (own > k))
            rem = jnp.where(hit, -jnp.inf, rem)
        pens.append(jnp.where(on, neg_shift, PEN_OFF).astype(BF16))
    r = lax.broadcasted_iota(jnp.int32, (2 * nb, 2 * LANES), 0)
    to_lane = (lax.broadcasted_iota(jnp.int32, (2 * nb, 2 * LANES), 1)
               == jnp.where(r < nb, PEN_LANE + r, LANES + PEN_LANE + r - nb)).astype(BF16)
    pen_lanes = lax.dot_general(jnp.concatenate(pens, axis=0), to_lane, (((0,), (0,)), ((), ())),
                                preferred_element_type=F32)
    lo_q = lax.broadcasted_iota(jnp.int32, (TG, LANES), 1) < HEAD_DIM
    qs = q * MOBA_QSCALE
    qs2 = jnp.concatenate([jnp.where(lo_q, qs, 0.0),
                           pltpu.roll(jnp.where(lo_q, 0.0, qs), HEAD_DIM, 1)], axis=1)
    qx_ref[0] = (qs2 + pen_lanes).astype(BF16)


def _moba_gate_call(neg_shift, qb, kmean):
    b, s, _ = qb.shape
    nb = kmean.shape[1]
    return pl.pallas_call(
        _moba_gate_kernel,
        out_shape=jax.ShapeDtypeStruct((b, s, H_MOBA * LANES), BF16),
        grid=(b, N_PAIRS, s // TG),
        in_specs=[pl.BlockSpec(memory_space=pltpu.SMEM),
                  pl.BlockSpec((1, TG, LANES), lambda bi, p, i: (bi, i, p)),
                  pl.BlockSpec((1, nb, LANES), lambda bi, p, i: (bi, 0, p))],
        out_specs=pl.BlockSpec((1, TG, 2 * LANES), lambda bi, p, i: (bi, i, p)),
        compiler_params=pltpu.CompilerParams(
            dimension_semantics=("arbitrary", "arbitrary", "arbitrary"),
            vmem_limit_bytes=VMEM_LIMIT),
        name="moba_gate",
    )(neg_shift, qb, kmean)


def _flash_update(hh, s, v_ext, m_sc, acc_sc, fixed_shift):
    if fixed_shift:
        acc_sc[hh] += jnp.dot(jnp.exp2(s).astype(BF16), v_ext, preferred_element_type=F32)
        return
    m_prev = m_sc[hh]
    m_new = jnp.maximum(m_prev, jnp.max(s, axis=-1, keepdims=True))
    alpha = jnp.exp2(m_prev - m_new)
    p = jnp.concatenate([jnp.exp2(s[:, LANES * c:LANES * (c + 1)] - m_new)
                         for c in range(s.shape[1] // LANES)], axis=1).astype(BF16)
    pv = jnp.dot(p, v_ext, preferred_element_type=F32)
    acc_sc[hh] = jnp.concatenate([alpha, alpha], axis=1) * acc_sc[hh] + pv
    m_sc[hh] = m_new


def _with_ones(v):
    return jnp.concatenate([v, jnp.ones(v.shape, v.dtype)], axis=1)


def _flash_init(m_sc, acc_sc):
    m_sc[...] = jnp.full(m_sc.shape, NEG_BIG, F32)
    acc_sc[...] = jnp.zeros(acc_sc.shape, F32)


def _flash_finish(o_ref, sg_ref, acc_sc):
    lo_half = lax.broadcasted_iota(jnp.int32, (acc_sc.shape[1], LANES), 1) < HEAD_DIM
    o = jnp.where(lo_half, acc_sc[0, :, :LANES] / acc_sc[0, :, LANES:],
                  acc_sc[1, :, :LANES] / acc_sc[1, :, LANES:])
    o_ref[0] = (o * sg_ref[0].astype(F32)).astype(o_ref.dtype)


def _pipelined(n, produce, consume, bufs, produce_last=None):
    a, b = bufs
    produce_last = produce_last or produce
    last = n - 1

    @pl.when(n == 1)
    def _():
        produce_last(0, a)
        consume(0, a)

    @pl.when(n > 1)
    def _():
        produce(0, a)
        n2 = lax.shift_right_logical(n - 2, 1)

        def body(it, carry):
            u = 2 * it + 1
            produce(u, b)
            consume(u - 1, a)
            produce(u + 1, a)
            consume(u, b)
            return carry

        lax.fori_loop(0, n2, body, 0)

        @pl.when(last == 2 * n2 + 2)
        def _():
            u = last - 1
            produce(u, b)
            consume(u - 1, a)
            produce_last(last, a)
            consume(u, b)
            consume(last, a)

        @pl.when(last == 2 * n2 + 1)
        def _():
            produce_last(last, b)
            consume(last - 1, a)
            consume(last, b)


def _mla_kernel(q_ref, k_ref, v_ref, sg_ref, o_ref, m_sc, acc_sc, sa_sc, sb_sc, *, fixed_shift):
    i = pl.program_id(2)
    t = TQ_MLA
    _flash_init(m_sc, acc_sc)

    def produce(u, buf, diagonal=False):
        kstart = pl.multiple_of(u * t, t)
        for hh in range(2):
            sl = slice(LANES * hh, LANES * (hh + 1))
            s = lax.dot_general(q_ref[0, :, sl], k_ref[0, pl.ds(kstart, t), sl], _NT,
                                preferred_element_type=F32)
            if diagonal:
                row = lax.broadcasted_iota(jnp.int32, (t, t), 0)
                col = lax.broadcasted_iota(jnp.int32, (t, t), 1)
                s = jnp.where(col <= row, s, NEG_BIG)
            buf[hh] = s

    def consume(u, buf):
        v = _with_ones(v_ref[0, pl.ds(pl.multiple_of(u * t, t), t), :])
        for hh in range(2):
            _flash_update(hh, buf[hh], v, m_sc, acc_sc, fixed_shift)

    _pipelined(i + 1, produce, consume, (sa_sc, sb_sc),
               produce_last=functools.partial(produce, diagonal=True))
    _flash_finish(o_ref, sg_ref, acc_sc)


def _mla_call(qm, km, vm, sgm, fixed_shift):
    b, s, _ = qm.shape
    t = TQ_MLA
    return pl.pallas_call(
        functools.partial(_mla_kernel, fixed_shift=fixed_shift),
        out_shape=jax.ShapeDtypeStruct((b, s, MLA_WIDTH), BF16),
        grid=(b, N_PAIRS, s // t),
        in_specs=[pl.BlockSpec((1, t, 2 * LANES), lambda bi, p, i: (bi, i, p)),
                  pl.BlockSpec((1, s, 2 * LANES), lambda bi, p, i: (bi, 0, p)),
                  pl.BlockSpec((1, s, LANES), lambda bi, p, i: (bi, 0, p)),
                  pl.BlockSpec((1, t, LANES), lambda bi, p, i: (bi, i, p))],
        out_specs=pl.BlockSpec((1, t, LANES), lambda bi, p, i: (bi, i, p)),
        scratch_shapes=[pltpu.VMEM((2, t, LANES), F32), pltpu.VMEM((2, t, 2 * LANES), F32),
                        pltpu.VMEM((2, t, t), F32), pltpu.VMEM((2, t, t), F32)],
        compiler_params=pltpu.CompilerParams(
            dimension_semantics=("arbitrary", "arbitrary", "arbitrary"),
            vmem_limit_bytes=VMEM_LIMIT),
        name="mla_attention",
    )(qm, km, vm, sgm)


def _moba_kernel(qx_ref, k_ref, v_ref, bt_ref, sg_ref, o_ref, m_sc, acc_sc, sa_sc, sb_sc, *,
                 fixed_shift):
    i = pl.program_id(2)
    t = MOBA_BLOCK
    w = MOBA_TILE
    nsub = w // t
    _flash_init(m_sc, acc_sc)

    def produce(u, buf):
        kstart = pl.multiple_of(u * w, w)
        for hh in range(2):
            sl = slice(LANES * hh, LANES * (hh + 1))
            s = lax.dot_general(qx_ref[0, :, sl], k_ref[0, pl.ds(kstart, w), sl], _NT,
                                preferred_element_type=F32)
            for qs in range(nsub):
                for ks in range(nsub):
                    d = jnp.maximum(nsub * (i - u) + (qs - ks), 0)
                    rows, cols = slice(t * qs, t * (qs + 1)), slice(t * ks, t * (ks + 1))
                    buf[hh, rows, cols] = s[rows, cols] + bt_ref[0, hh, d]

    def consume(u, buf):
        v = _with_ones(v_ref[0, pl.ds(pl.multiple_of(u * w, w), w), :])
        for hh in range(2):
            _flash_update(hh, buf[hh], v, m_sc, acc_sc, fixed_shift)

    _pipelined(i + 1, produce, consume, (sa_sc, sb_sc))
    _flash_finish(o_ref, sg_ref, acc_sc)


def _moba_call(qx, kb, vb, bias_tiles, sgb, fixed_shift):
    b, s, _ = vb.shape
    t = MOBA_TILE
    nb = s // MOBA_BLOCK
    tok = lambda w: pl.BlockSpec((1, t, w), lambda bi, p, i: (bi, i, p))
    return pl.pallas_call(
        functools.partial(_moba_kernel, fixed_shift=fixed_shift),
        out_shape=jax.ShapeDtypeStruct((b, s, MOBA_WIDTH), BF16),
        grid=(b, N_PAIRS, s // t),
        in_specs=[tok(2 * LANES),
                  pl.BlockSpec((1, s, 2 * LANES), lambda bi, p, i: (bi, 0, p)),
                  pl.BlockSpec((1, s, LANES), lambda bi, p, i: (bi, 0, p)),
                  pl.BlockSpec((1, 2, nb, MOBA_BLOCK, MOBA_BLOCK), lambda bi, p, i: (p, 0, 0, 0, 0),
                               pipeline_mode=pl.Buffered(1)),
                  tok(LANES)],
        out_specs=tok(LANES),
        scratch_shapes=[pltpu.VMEM((2, t, LANES), F32), pltpu.VMEM((2, t, 2 * LANES), F32),
                        pltpu.VMEM((2, t, t), F32), pltpu.VMEM((2, t, t), F32)],
        compiler_params=pltpu.CompilerParams(
            dimension_semantics=("arbitrary", "arbitrary", "arbitrary"),
            vmem_limit_bytes=VMEM_LIMIT),
        name="moba_attention",
    )(qx, kb, vb, bias_tiles, sgb)


def _proj_out_kernel(am_ref, ab_ref, w_ref, x_ref, mod_ref, o_ref):
    y = (jnp.dot(am_ref[0], w_ref[0:MLA_WIDTH, :], preferred_element_type=F32)
         + jnp.dot(ab_ref[0], w_ref[MLA_WIDTH:, :], preferred_element_type=F32))
    o_ref[0] = x_ref[0] + mod_ref[0, 2:3, :] * y


def _proj_out_call(a_mla, a_moba, w_out, x, mod3):
    b, s, d = x.shape
    tok = lambda w: pl.BlockSpec((1, TM, w), lambda bi, i: (bi, i, 0))
    return pl.pallas_call(
        _proj_out_kernel,
        out_shape=jax.ShapeDtypeStruct((b, s, d), F32),
        grid=(b, s // TM),
        in_specs=[tok(MLA_WIDTH), tok(MOBA_WIDTH), pl.BlockSpec(w_out.shape, lambda bi, i: (0, 0)),
                  tok(d), pl.BlockSpec((1, 3, d), lambda bi, i: (bi, 0, 0))],
        out_specs=tok(d),
        compiler_params=pltpu.CompilerParams(dimension_semantics=("arbitrary", "arbitrary"),
                                             vmem_limit_bytes=VMEM_LIMIT),
        name="proj_out",
    )(a_mla, a_moba, w_out, x, mod3)


def _to_head_lanes(a):
    a = jnp.concatenate([a, jnp.zeros(a.shape[:-1] + (1,), a.dtype)], axis=-1)
    return a[..., _LANE_SRC]


def _layer_weights(w_in, w_uq, w_ukv, q_g, k_g, bq_g, bk_g):
    o0 = Q_LORA
    o1 = o0 + KV_LORA
    o2 = o1 + ROPE_DIM
    k_rope = w_in[:, o1:o2]
    kr128 = jnp.zeros((D_MODEL, LANES), w_in.dtype)
    kr128 = kr128.at[:, 0:HALF_ROPE].set(k_rope[:, :HALF_ROPE])
    kr128 = kr128.at[:, LANES // 2:LANES // 2 + HALF_ROPE].set(k_rope[:, HALF_ROPE:])
    win = jnp.concatenate([w_in[:, :o1], kr128, w_in[:, o2:]], axis=1).astype(BF16)
    wuq = _to_head_lanes(w_uq.reshape(Q_LORA, H_MLA, QK_DIM)).reshape(Q_LORA, H_MLA * LANES)
    ukv = w_ukv.reshape(KV_LORA, H_MLA, NOPE_DIM + HEAD_DIM)
    k_nope = jnp.concatenate([ukv[..., :NOPE_DIM], jnp.zeros((KV_LORA, H_MLA, ROPE_DIM), ukv.dtype)], -1)
    wuk = _to_head_lanes(k_nope).reshape(KV_LORA, H_MLA * LANES)
    wuv = ukv[..., NOPE_DIM:].reshape(KV_LORA, MLA_WIDTH)
    pair = lambda g: jnp.concatenate([g, g]).reshape(1, LANES)
    return (win, wuq.astype(BF16), wuk.astype(BF16), wuv.astype(BF16),
            _to_head_lanes(q_g * MLA_QSCALE).reshape(1, LANES), _to_head_lanes(k_g).reshape(1, LANES),
            pair(bq_g), pair(bk_g))


def kernel(x, c, positions, norm_g, w_ada, b_ada, w_in, mla_q_norm_g, mla_w_uq, mla_kv_norm_g,
           mla_w_ukv, mla_q_g, mla_k_g, moba_q_g, moba_k_g, w_out, rel_bias):
    b, s, d = x.shape
    depth = w_in.shape[0]
    nb = s // MOBA_BLOCK
    assert d == D_MODEL and s % TM == 0 and s % TQ_MLA == 0 and TM % MOBA_BLOCK == 0 and b <= 8
    assert s % TG == 0 and TG % MOBA_BLOCK == 0 and s % MOBA_TILE == 0 and nb <= LANES // 4
    _check_far_tiles(s)

    c8 = jnp.pad(c, ((0, 8 - b), (0, 0)))
    mod = _mod_call(c8, w_ada, b_ada.reshape(depth, 1, 3 * d))[:, :b].reshape(depth, b, 3, d)

    inv_freq = ROPE_THETA ** (-jnp.arange(0, HALF_ROPE, dtype=F32) / HALF_ROPE)
    freq = jnp.zeros((LANES,), F32).at[0:HALF_ROPE].set(inv_freq)
    freq = freq.at[LANES // 2:LANES // 2 + HALF_ROPE].set(inv_freq).reshape(1, LANES)
    sign = jnp.zeros((LANES,), F32).at[0:HALF_ROPE].set(-1.0)
    sign = sign.at[LANES // 2:LANES // 2 + HALF_ROPE].set(1.0).reshape(1, LANES)
    rope_c, rope_s = _rope_call(positions.reshape(b * s, 1), freq, sign)

    bkt = _rel_bucket(jnp.arange(s, dtype=jnp.int32))
    thr = jnp.sum(bkt[None, :] < jnp.arange(N_BUCKETS, dtype=jnp.int32)[:, None], axis=1).astype(jnp.int32)
    ttab, vtab = _far_tables(bkt, rel_bias, s)
    bias_tiles = _bias_tile_call(thr, ttab, rel_bias.astype(F32).reshape(-1), vtab, nb)
    bias_tiles = bias_tiles.reshape(N_PAIRS, 2, nb, MOBA_BLOCK, MOBA_BLOCK)
    kpad = jnp.zeros((1, LANES), F32).at[0, SHIFT_LANE].set(1.0)
    amax = lambda a: jnp.max(jnp.abs(a.astype(F32)))

    def attention(fixed_shift, qm, km, vm, sgm, qx, kb, vb, sgb):
        return (_mla_call(qm, km, vm, sgm, fixed_shift),
                _moba_call(qx, kb, vb, bias_tiles, sgb, fixed_shift))

    for l in range(depth):
        win, wuq, wuk, wuv, qg, kg, bqg, bkg = _layer_weights(
            w_in[l], mla_w_uq[l], mla_w_ukv[l], mla_q_g[l], mla_k_g[l], moba_q_g[l], moba_k_g[l])
        bound_mla = QK_DIM * amax(mla_q_g[l]) * amax(mla_k_g[l]) * MLA_QSCALE * SHIFT_SLACK
        bound_moba = (HEAD_DIM * amax(moba_q_g[l]) * amax(moba_k_g[l]) * MOBA_QSCALE
                      + amax(rel_bias) * LOG2E) * SHIFT_SLACK
        fixed_ok = (bound_mla <= MAX_SHIFT) & (bound_moba <= MAX_SHIFT)
        qpad = jnp.zeros((1, LANES), F32).at[0, SHIFT_LANE].set(jnp.where(fixed_ok, -bound_mla, 0.0))
        qm, km, vm, sgm, qb, kb, vb, sgb, kmean = _proj_in_call(
            x, mod[l], norm_g[l].reshape(1, d), win, mla_q_norm_g[l].reshape(1, Q_LORA), wuq,
            mla_kv_norm_g[l].reshape(1, KV_LORA), wuk, wuv, qg, kg, bqg, bkg, rope_c, rope_s,
            qpad, kpad)
        qx = _moba_gate_call(jnp.where(fixed_ok, -bound_moba, 0.0).reshape(1), qb,
                             kmean.reshape(b, nb, MOBA_WIDTH))
        a_mla, a_moba = lax.cond(
            fixed_ok, functools.partial(attention, True), functools.partial(attention, False),
            qm, km, vm, sgm, qx, kb, vb, sgb)
        x = _proj_out_call(a_mla, a_moba, w_out[l].astype(BF16), x, mod[l])
    return x
```

```python
import functools
import math

import numpy as np
import jax
import jax.numpy as jnp
from jax import lax
from jax.experimental import pallas as pl
from jax.experimental.pallas import tpu as pltpu

F32 = jnp.float32
BF16 = jnp.bfloat16

D_MODEL = 1024
DEPTH = 2
HEAD_DIM = 64
H_MLA = 8
H_MOBA = 8
MLA_WIDTH = H_MLA * HEAD_DIM
MOBA_WIDTH = H_MOBA * HEAD_DIM
Q_LORA = 256
KV_LORA = 128
NOPE_DIM = 64
ROPE_DIM = 32
QK_DIM = NOPE_DIM + ROPE_DIM
ROPE_THETA = 10000.0
MOBA_BLOCK = 256
MOBA_TOPK = 3
N_BUCKETS = 32
REL_MAX_DIST = 4096
EPS = 1e-6

LANES = 128
HALF_ROPE = ROPE_DIM // 2
N_PAIRS = H_MLA // 2
LOG2E = math.log2(math.e)
MLA_QSCALE = QK_DIM ** -0.5 * LOG2E
MOBA_QSCALE = HEAD_DIM ** -0.5 * LOG2E
NEG_BIG = -1e30
FAR_THRESHOLDS = 2
NEAR_TILES = 3
NEVER = 1 << 20
PEN_OFF = -2.0 ** 127
SHIFT_LANE = QK_DIM
PEN_LANE = HEAD_DIM
MAX_SHIFT = 48.0
SHIFT_SLACK = 1.02

TM = 256
TM_OUT = 512
TG = 512
TQ_MLA = 512
MOBA_TILE = 2 * MOBA_BLOCK
VMEM_LIMIT = 56 * 1024 * 1024

_NT = (((1,), (1,)), ((), ()))


def _head_lane_source():
    src = np.full((LANES,), QK_DIM, np.int32)
    src[0:16] = NOPE_DIM + np.arange(16)
    src[16:64] = np.arange(48)
    src[64:80] = NOPE_DIM + HALF_ROPE + np.arange(16)
    src[80:96] = 48 + np.arange(16)
    return src


_LANE_SRC = _head_lane_source()


def _bucket_np(n, dtype):
    n = np.asarray(n)
    max_exact = N_BUCKETS // 2
    nf = np.maximum(n, 1).astype(dtype)
    large = max_exact + (np.log(nf / dtype(max_exact)) / dtype(math.log(REL_MAX_DIST / max_exact))
                         * dtype(N_BUCKETS - max_exact)).astype(np.int32)
    large = np.minimum(large, N_BUCKETS - 1)
    return np.where(n < max_exact, n, large)


def _check_far_tiles(seq):
    n = np.arange(seq + 2)
    for dtype in (np.float32, np.float64):
        b = _bucket_np(n, dtype)
        chg = np.concatenate([[0], (b[1:] != b[:-1]).astype(np.int64)])
        for d in range(NEAR_TILES, seq // MOBA_BLOCK):
            lo, hi = MOBA_BLOCK * (d - 1) + 1, MOBA_BLOCK * (d + 1) - 1
            assert chg[lo - 1:hi + 1].sum() <= FAR_THRESHOLDS, (d, dtype)


def _rel_bucket(dist):
    n = jnp.maximum(dist, 0)
    max_exact = N_BUCKETS // 2
    nf = jnp.maximum(n, 1).astype(F32)
    large = max_exact + (jnp.log(nf / max_exact) / math.log(REL_MAX_DIST / max_exact)
                         * (N_BUCKETS - max_exact)).astype(jnp.int32)
    large = jnp.minimum(large, N_BUCKETS - 1)
    return jnp.where(n < max_exact, n, large)


def _rms(x, g):
    return x * lax.rsqrt(jnp.mean(x * x, axis=-1, keepdims=True) + EPS) * g


def _silu(x):
    return x * jax.nn.sigmoid(x)


def _mod_kernel(c_ref, w_ref, b_ref, o_ref):
    c = c_ref[...]
    o_ref[0] = jnp.dot(_silu(c), w_ref[0], precision=lax.Precision.HIGHEST,
                       preferred_element_type=F32) + b_ref[0]


def _mod_call(c8, w_ada, b_ada):
    depth, d, d3 = w_ada.shape
    nchunk = d3 // d
    return pl.pallas_call(
        _mod_kernel,
        out_shape=jax.ShapeDtypeStruct((depth, 8, d3), F32),
        grid=(depth, nchunk),
        in_specs=[pl.BlockSpec((8, d), lambda l, j: (0, 0)),
                  pl.BlockSpec((1, d, d), lambda l, j: (l, 0, j)),
                  pl.BlockSpec((1, 1, d), lambda l, j: (l, 0, j))],
        out_specs=pl.BlockSpec((1, 8, d), lambda l, j: (l, 0, j)),
        compiler_params=pltpu.CompilerParams(dimension_semantics=("arbitrary", "arbitrary"),
                                             vmem_limit_bytes=VMEM_LIMIT),
        name="adaln_mod",
    )(c8, w_ada, b_ada)


def _rope_kernel(pos_ref, freq_ref, sign_ref, c_ref, s_ref):
    ang = pos_ref[...].astype(F32) * freq_ref[...]
    c_ref[...] = jnp.cos(ang)
    s_ref[...] = jnp.sin(ang) * sign_ref[...]


def _rope_call(pos_col, freq, sign):
    n = pos_col.shape[0]
    tr = 1024
    return pl.pallas_call(
        _rope_kernel,
        out_shape=(jax.ShapeDtypeStruct((n, LANES), F32), jax.ShapeDtypeStruct((n, LANES), F32)),
        grid=(n // tr,),
        in_specs=[pl.BlockSpec((tr, 1), lambda i: (i, 0)),
                  pl.BlockSpec((1, LANES), lambda i: (0, 0)),
                  pl.BlockSpec((1, LANES), lambda i: (0, 0))],
        out_specs=(pl.BlockSpec((tr, LANES), lambda i: (i, 0)),
                   pl.BlockSpec((tr, LANES), lambda i: (i, 0))),
        compiler_params=pltpu.CompilerParams(dimension_semantics=("arbitrary",)),
        name="rope_tables",
    )(pos_col, freq, sign)


def _bias_tile_kernel(thr_ref, ttab_ref, rb_ref, vtab_ref, o_ref):
    h = pl.program_id(0)
    nb = o_ref.shape[1]
    shape = (MOBA_BLOCK, MOBA_BLOCK)
    rc = lax.broadcasted_iota(jnp.int32, shape, 0) - lax.broadcasted_iota(jnp.int32, shape, 1)

    for d in range(NEAR_TILES):
        dist = rc + d * MOBA_BLOCK
        val = jnp.full(shape, rb_ref[h], F32)
        for b in range(1, N_BUCKETS):
            val = jnp.where(dist >= thr_ref[b], rb_ref[b * H_MOBA + h], val)
        o_ref[0, d] = jnp.where(dist >= 0, val * LOG2E, NEG_BIG)

    def far(d, carry):
        base = (d * H_MOBA + h) * (FAR_THRESHOLDS + 1)
        val = jnp.full(shape, vtab_ref[base], F32)
        for k in range(FAR_THRESHOLDS):
            val = jnp.where(rc >= ttab_ref[d * FAR_THRESHOLDS + k], vtab_ref[base + k + 1], val)
        o_ref[0, d] = val
        return carry

    lax.fori_loop(NEAR_TILES, nb, far, 0)


def _bias_tile_call(thr, ttab, rb_flat, vtab, nb):
    smem = pl.BlockSpec(memory_space=pltpu.SMEM)
    return pl.pallas_call(
        _bias_tile_kernel,
        out_shape=jax.ShapeDtypeStruct((H_MOBA, nb, MOBA_BLOCK, MOBA_BLOCK), F32),
        grid_spec=pltpu.PrefetchScalarGridSpec(
            num_scalar_prefetch=2,
            grid=(H_MOBA,),
            in_specs=[smem, smem],
            out_specs=pl.BlockSpec((1, nb, MOBA_BLOCK, MOBA_BLOCK), lambda h, *_: (h, 0, 0, 0)),
        ),
        compiler_params=pltpu.CompilerParams(dimension_semantics=("arbitrary",),
                                             vmem_limit_bytes=VMEM_LIMIT),
        name="moba_bias_tiles",
    )(thr, ttab, rb_flat, vtab)


def _far_tables(bkt, rel_bias, seq):
    nb = seq // MOBA_BLOCK
    n = jnp.arange(seq, dtype=jnp.int32)
    chg = jnp.concatenate([jnp.zeros((1,), bool), bkt[1:] != bkt[:-1]])
    d = jnp.arange(nb, dtype=jnp.int32)[:, None]
    lo = jnp.maximum(MOBA_BLOCK * (d - 1) + 1, 0)
    hi = MOBA_BLOCK * (d + 1) - 1
    inside = chg[None, :] & (n[None, :] > lo) & (n[None, :] <= hi)
    cum = jnp.cumsum(inside.astype(jnp.int32), axis=1)
    idx = [bkt[jnp.minimum(lo[:, 0], seq - 1)]]
    thr = []
    for k in range(1, FAR_THRESHOLDS + 1):
        t = jnp.min(jnp.where(inside & (cum == k), n[None, :], NEVER), axis=1)
        thr.append(t - MOBA_BLOCK * d[:, 0])
        idx.append(bkt[jnp.minimum(t, seq - 1)])
    ttab = jnp.stack(thr, axis=1).reshape(-1).astype(jnp.int32)
    vals = rel_bias.astype(F32)[jnp.stack(idx, axis=1)] * LOG2E
    vtab = vals.transpose(0, 2, 1).reshape(-1)
    return ttab, vtab


def _mla_head(xh, g, rc, rs):
    ms = jnp.sum(xh * xh, axis=-1, keepdims=True) * (1.0 / QK_DIM)
    xn = xh * lax.rsqrt(ms + EPS) * g
    return xn * rc + pltpu.roll(xn, LANES // 2, 1) * rs


def _moba_pair_norm(x, g, lo_half):
    x2 = x * x
    s_lo = jnp.sum(jnp.where(lo_half, x2, 0.0), axis=-1, keepdims=True)
    s_hi = jnp.sum(jnp.where(lo_half, 0.0, x2), axis=-1, keepdims=True)
    ms = jnp.where(lo_half, s_lo, s_hi) * (1.0 / HEAD_DIM)
    return x * lax.rsqrt(ms + EPS) * g


def _proj_in_kernel(x_ref, mod_ref, ng_ref, win_ref, qng_ref, wuq_ref, kvng_ref, wuk_ref, wuv_ref,
                    qg_ref, kg_ref, bqg_ref, bkg_ref, rc_ref, rs_ref, qpad_ref, kpad_ref,
                    qm_ref, km_ref, vm_ref, sgm_ref, qb_ref, kb_ref, vb_ref, sgb_ref, kmean_ref):
    tile = pl.program_id(1)
    x = x_ref[0]
    h = _rms(x, ng_ref[...] * (1.0 + mod_ref[0, 1:2, :])) + mod_ref[0, 0:1, :]
    hb = h.astype(BF16)

    def proj(lo, hi):
        return jnp.dot(hb, win_ref[:, lo:hi], preferred_element_type=F32)

    rc = rc_ref[...]
    rs = rs_ref[...]

    cqn = _rms(proj(0, 256), qng_ref[...]).astype(BF16)
    q = jnp.dot(cqn, wuq_ref[...], preferred_element_type=F32)
    qg = qg_ref[...]
    qpad = qpad_ref[...]
    kpad = kpad_ref[...]
    for hh in range(H_MLA):
        sl = slice(LANES * hh, LANES * (hh + 1))
        qm_ref[0, :, sl] = (_mla_head(q[:, sl], qg, rc, rs) + qpad).astype(BF16)

    ckvn = _rms(proj(256, 384), kvng_ref[...]).astype(BF16)
    kn = jnp.dot(ckvn, wuk_ref[...], preferred_element_type=F32)
    kr = proj(384, 512)
    kg = kg_ref[...]
    for hh in range(H_MLA):
        sl = slice(LANES * hh, LANES * (hh + 1))
        km_ref[0, :, sl] = (_mla_head(kn[:, sl] + kr, kg, rc, rs) + kpad).astype(BF16)
    vm_ref[0] = jnp.dot(ckvn, wuv_ref[...], preferred_element_type=F32).astype(BF16)
    sgm_ref[0] = _silu(proj(512, 1024)).astype(BF16)

    lane = lax.broadcasted_iota(jnp.int32, (TM, LANES), 1)
    lo_half = lane < HEAD_DIM
    blk_id = tile * (TM // MOBA_BLOCK) + jnp.right_shift(
        lax.broadcasted_iota(jnp.int32, (TM, LANES), 0), MOBA_BLOCK.bit_length() - 1)
    onehot = (lane == PEN_LANE + blk_id).astype(F32)
    qbz = proj(1024, 1536)
    kbz = proj(1536, 2048)
    bqg = bqg_ref[...]
    bkg = bkg_ref[...]
    for p in range(N_PAIRS):
        sl = slice(LANES * p, LANES * (p + 1))
        qb_ref[0, :, sl] = _moba_pair_norm(qbz[:, sl], bqg, lo_half)
        kn_p = _moba_pair_norm(kbz[:, sl], bkg, lo_half)
        kb_ref[0, :, LANES * 2 * p:LANES * (2 * p + 1)] = jnp.where(lo_half, kn_p, onehot).astype(BF16)
        kb_ref[0, :, LANES * (2 * p + 1):LANES * (2 * p + 2)] = jnp.where(
            lo_half, pltpu.roll(kn_p, HEAD_DIM, 1), onehot).astype(BF16)
        for blk in range(TM // MOBA_BLOCK):
            rows = kn_p[MOBA_BLOCK * blk:MOBA_BLOCK * (blk + 1)]
            kmean_ref[0, 0, blk:blk + 1, sl] = jnp.sum(rows, axis=0, keepdims=True) * (1.0 / MOBA_BLOCK)
    vb_ref[0] = proj(2048, 2560).astype(BF16)
    sgb_ref[0] = _silu(proj(2560, 3072)).astype(BF16)


def _proj_in_call(x, mod3, ng, win, qng, wuq, kvng, wuk, wuv, qg, kg, bqg, bkg, rope_c, rope_s,
                  qpad, kpad):
    b, s, d = x.shape
    nt = s // TM
    const2 = lambda bi, i: (0, 0)
    tok = lambda w: pl.BlockSpec((1, TM, w), lambda bi, i: (bi, i, 0))
    full = lambda a: pl.BlockSpec(a.shape, const2)
    rope_spec = pl.BlockSpec((TM, LANES), lambda bi, i: (bi * nt + i, 0))
    out_shape = (
        jax.ShapeDtypeStruct((b, s, H_MLA * LANES), BF16),
        jax.ShapeDtypeStruct((b, s, H_MLA * LANES), BF16),
        jax.ShapeDtypeStruct((b, s, MLA_WIDTH), BF16),
        jax.ShapeDtypeStruct((b, s, MLA_WIDTH), BF16),
        jax.ShapeDtypeStruct((b, s, MOBA_WIDTH), F32),
        jax.ShapeDtypeStruct((b, s, H_MOBA * LANES), BF16),
        jax.ShapeDtypeStruct((b, s, MOBA_WIDTH), BF16),
        jax.ShapeDtypeStruct((b, s, MOBA_WIDTH), BF16),
        jax.ShapeDtypeStruct((b, nt, TM // MOBA_BLOCK, MOBA_WIDTH), F32),
    )
    out_specs = (tok(H_MLA * LANES), tok(H_MLA * LANES), tok(MLA_WIDTH), tok(MLA_WIDTH),
                 tok(MOBA_WIDTH), tok(H_MOBA * LANES), tok(MOBA_WIDTH), tok(MOBA_WIDTH),
                 pl.BlockSpec((1, 1, TM // MOBA_BLOCK, MOBA_WIDTH), lambda bi, i: (bi, i, 0, 0)))
    return pl.pallas_call(
        _proj_in_kernel,
        out_shape=out_shape,
        grid=(b, nt),
        in_specs=[tok(d), pl.BlockSpec((1, 3, d), lambda bi, i: (bi, 0, 0)), full(ng), full(win),
                  full(qng), full(wuq), full(kvng), full(wuk), full(wuv),
                  full(qg), full(kg), full(bqg), full(bkg), rope_spec, rope_spec,
                  full(qpad), full(kpad)],
        out_specs=out_specs,
        compiler_params=pltpu.CompilerParams(dimension_semantics=("arbitrary", "arbitrary"),
                                             vmem_limit_bytes=VMEM_LIMIT),
        name="proj_in",
    )(x, mod3, ng, win, qng, wuq, kvng, wuk, wuv, qg, kg, bqg, bkg, rope_c, rope_s, qpad, kpad)


def _moba_gate_kernel(neg_shift_ref, q_ref, kmean_ref, qx_ref):
    nb = kmean_ref.shape[1]
    km = kmean_ref[0]
    q = q_ref[0]
    lo_k = lax.broadcasted_iota(jnp.int32, (nb, LANES), 1) < HEAD_DIM
    km2 = jnp.concatenate([jnp.where(lo_k, km, 0.0), jnp.where(lo_k, 0.0, km)], axis=0)
    g2 = lax.dot_general(km2, q, _NT, precision=lax.Precision.HIGHEST, preferred_element_type=F32)
    blk = lax.broadcasted_iota(jnp.int32, (nb, TG), 0)
    blkf = blk.astype(F32)
    own = pl.program_id(2) * (TG // MOBA_BLOCK) + jnp.right_shift(
        lax.broadcasted_iota(jnp.int32, (nb, TG), 1), MOBA_BLOCK.bit_length() - 1)
    neg_shift = neg_shift_ref[0]
    pens = []
    for hh in range(2):
        rem = jnp.where(blk < own, g2[nb * hh:nb * (hh + 1)], -jnp.inf)
        on = blk == own
        for k in range(MOBA_TOPK):
            mx = jnp.max(rem, axis=0, keepdims=True)
            idx = jnp.min(jnp.where(rem == mx, blkf, float(nb)), axis=0, keepdims=True)
            hit = blkf == idx
            on = on | (hit & (own > k))
            rem = jnp.where(hit, -jnp.inf, rem)
        pens.append(jnp.where(on, neg_shift, PEN_OFF).astype(BF16))
    r = lax.broadcasted_iota(jnp.int32, (2 * nb, 2 * LANES), 0)
    to_lane = (lax.broadcasted_iota(jnp.int32, (2 * nb, 2 * LANES), 1)
               == jnp.where(r < nb, PEN_LANE + r, LANES + PEN_LANE + r - nb)).astype(BF16)
    pen_lanes = lax.dot_general(jnp.concatenate(pens, axis=0), to_lane, (((0,), (0,)), ((), ())),
                                preferred_element_type=F32)
    lo_q = lax.broadcasted_iota(jnp.int32, (TG, LANES), 1) < HEAD_DIM
    qs = q * MOBA_QSCALE
    qs2 = jnp.concatenate([jnp.where(lo_q, qs, 0.0),
                           pltpu.roll(jnp.where(lo_q, 0.0, qs), HEAD_DIM, 1)], axis=1)
    qx_ref[0] = (qs2 + pen_lanes).astype(BF16)


def _moba_gate_call(neg_shift, qb, kmean):
    b, s, _ = qb.shape
    nb = kmean.shape[1]
    return pl.pallas_call(
        _moba_gate_kernel,
        out_shape=jax.ShapeDtypeStruct((b, s, H_MOBA * LANES), BF16),
        grid=(b, N_PAIRS, s // TG),
        in_specs=[pl.BlockSpec(memory_space=pltpu.SMEM),
                  pl.BlockSpec((1, TG, LANES), lambda bi, p, i: (bi, i, p)),
                  pl.BlockSpec((1, nb, LANES), lambda bi, p, i: (bi, 0, p))],
        out_specs=pl.BlockSpec((1, TG, 2 * LANES), lambda bi, p, i: (bi, i, p)),
        compiler_params=pltpu.CompilerParams(
            dimension_semantics=("arbitrary", "arbitrary", "arbitrary"),
            vmem_limit_bytes=VMEM_LIMIT),
        name="moba_gate",
    )(neg_shift, qb, kmean)


def _flash_update(hh, s, v_ext, m_sc, acc_sc, first, keep, fixed_shift):
    if fixed_shift:
        acc_sc[hh] = acc_sc[hh] * keep + jnp.dot(jnp.exp2(s).astype(BF16), v_ext,
                                                 preferred_element_type=F32)
        return
    m_prev = jnp.where(first == 1, NEG_BIG, m_sc[hh])
    m_new = jnp.maximum(m_prev, jnp.max(s, axis=-1, keepdims=True))
    alpha = jnp.exp2(m_prev - m_new)
    p = jnp.concatenate([jnp.exp2(s[:, LANES * c:LANES * (c + 1)] - m_new)
                         for c in range(s.shape[1] // LANES)], axis=1).astype(BF16)
    pv = jnp.dot(p, v_ext, preferred_element_type=F32)
    acc_sc[hh] = jnp.concatenate([alpha, alpha], axis=1) * acc_sc[hh] + pv
    m_sc[hh] = m_new


def _with_ones(v):
    return jnp.concatenate([v, jnp.ones(v.shape, v.dtype)], axis=1)


def _flash_init(m_sc, acc_sc):
    m_sc[...] = jnp.full(m_sc.shape, NEG_BIG, F32)
    acc_sc[...] = jnp.zeros(acc_sc.shape, F32)


def _flash_finish(o_ref, sg_ref, rows, acc_sc):
    lo_half = lax.broadcasted_iota(jnp.int32, (acc_sc.shape[1], LANES), 1) < HEAD_DIM
    o = jnp.where(lo_half, acc_sc[0, :, :LANES] / acc_sc[0, :, LANES:],
                  acc_sc[1, :, :LANES] / acc_sc[1, :, LANES:])
    o_ref[0, rows, :] = (o * sg_ref[0, rows, :].astype(F32)).astype(o_ref.dtype)


def _causal_steps(n_tiles):
    pairs = [(qi, u) for qi in range(n_tiles) for u in range(qi + 1)]
    return (jnp.asarray([p[0] for p in pairs], jnp.int32), jnp.asarray([p[1] for p in pairs], jnp.int32),
            jnp.asarray([float(p[1] > 0) for p in pairs], F32))


def _pipelined(n_steps, produce, consume, bufs):
    a, b = bufs
    produce(0, a)
    doubles = (n_steps - 1) // 2

    def body(it, carry):
        n = 2 * it
        produce(n + 1, b)
        consume(n, a)
        produce(n + 2, a)
        consume(n + 1, b)
        return carry

    lax.fori_loop(0, doubles, body, 0)
    if n_steps - 1 == 2 * doubles:
        consume(n_steps - 1, a)
    else:
        produce(n_steps - 1, b)
        consume(n_steps - 2, a)
        consume(n_steps - 1, b)


def _mla_kernel(qtab_ref, utab_ref, keep_ref, q_ref, k_ref, v_ref, sg_ref, o_ref,
                m_sc, acc_sc, sa_sc, sb_sc, mask_sc, *, fixed_shift, n_steps):
    t = TQ_MLA
    _flash_init(m_sc, acc_sc)
    row = lax.broadcasted_iota(jnp.int32, (t, t), 0)
    col = lax.broadcasted_iota(jnp.int32, (t, t), 1)
    mask_sc[0] = jnp.zeros((t, t), F32)
    mask_sc[1] = jnp.where(col <= row, 0.0, NEG_BIG)

    def produce(n, buf):
        qi, u = qtab_ref[n], utab_ref[n]
        qrows = pl.ds(pl.multiple_of(qi * t, t), t)
        krows = pl.ds(pl.multiple_of(u * t, t), t)
        mask = mask_sc[(u == qi).astype(jnp.int32)]
        for hh in range(2):
            sl = slice(LANES * hh, LANES * (hh + 1))
            buf[hh] = lax.dot_general(q_ref[0, qrows, sl], k_ref[0, krows, sl], _NT,
                                      preferred_element_type=F32) + mask

    def consume(n, buf):
        qi, u = qtab_ref[n], utab_ref[n]
        v = _with_ones(v_ref[0, pl.ds(pl.multiple_of(u * t, t), t), :])
        first = (u == 0).astype(jnp.int32)
        for hh in range(2):
            _flash_update(hh, buf[hh], v, m_sc, acc_sc, first, keep_ref[n], fixed_shift)
        _flash_finish(o_ref, sg_ref, pl.ds(pl.multiple_of(qi * t, t), t), acc_sc)

    _pipelined(n_steps, produce, consume, (sa_sc, sb_sc))


def _mla_call(qm, km, vm, sgm, fixed_shift):
    b, s, _ = qm.shape
    t = TQ_MLA
    qtab, utab, keep = _causal_steps(s // t)
    seq = lambda w: pl.BlockSpec((1, s, w), lambda p, bi, *_: (bi, 0, p))
    return pl.pallas_call(
        functools.partial(_mla_kernel, fixed_shift=fixed_shift, n_steps=int(qtab.shape[0])),
        out_shape=jax.ShapeDtypeStruct((b, s, MLA_WIDTH), BF16),
        grid_spec=pltpu.PrefetchScalarGridSpec(
            num_scalar_prefetch=2,
            grid=(N_PAIRS, b),
            in_specs=[pl.BlockSpec(memory_space=pltpu.SMEM),
                      seq(2 * LANES), seq(2 * LANES), seq(LANES), seq(LANES)],
            out_specs=seq(LANES),
            scratch_shapes=[pltpu.VMEM((2, t, LANES), F32), pltpu.VMEM((2, t, 2 * LANES), F32),
                            pltpu.VMEM((2, t, t), F32), pltpu.VMEM((2, t, t), F32),
                            pltpu.VMEM((2, t, t), F32)],
        ),
        compiler_params=pltpu.CompilerParams(dimension_semantics=("arbitrary", "arbitrary"),
                                             vmem_limit_bytes=VMEM_LIMIT),
        name="mla_attention",
    )(qtab, utab, keep, qm, km, vm, sgm)


def _moba_kernel(qtab_ref, utab_ref, keep_ref, qx_ref, k_ref, v_ref, bt_ref, sg_ref, o_ref,
                 m_sc, acc_sc, sa_sc, sb_sc, *, fixed_shift, n_steps):
    t = MOBA_BLOCK
    w = MOBA_TILE
    nsub = w // t
    _flash_init(m_sc, acc_sc)

    def produce(n, buf):
        qi, u = qtab_ref[n], utab_ref[n]
        qrows = pl.ds(pl.multiple_of(qi * w, w), w)
        krows = pl.ds(pl.multiple_of(u * w, w), w)
        for hh in range(2):
            sl = slice(LANES * hh, LANES * (hh + 1))
            s = lax.dot_general(qx_ref[0, qrows, sl], k_ref[0, krows, sl], _NT,
                                preferred_element_type=F32)
            for qs in range(nsub):
                for ks in range(nsub):
                    d = jnp.maximum(nsub * (qi - u) + (qs - ks), 0)
                    rows, cols = slice(t * qs, t * (qs + 1)), slice(t * ks, t * (ks + 1))
                    buf[hh, rows, cols] = s[rows, cols] + bt_ref[0, hh, d]

    def consume(n, buf):
        qi, u = qtab_ref[n], utab_ref[n]
        v = _with_ones(v_ref[0, pl.ds(pl.multiple_of(u * w, w), w), :])
        first = (u == 0).astype(jnp.int32)
        for hh in range(2):
            _flash_update(hh, buf[hh], v, m_sc, acc_sc, first, keep_ref[n], fixed_shift)
        _flash_finish(o_ref, sg_ref, pl.ds(pl.multiple_of(qi * w, w), w), acc_sc)

    _pipelined(n_steps, produce, consume, (sa_sc, sb_sc))


def _moba_call(qx, kb, vb, bias_tiles, sgb, fixed_shift):
    b, s, _ = vb.shape
    t = MOBA_TILE
    nb = s // MOBA_BLOCK
    qtab, utab, keep = _causal_steps(s // t)
    seq = lambda w: pl.BlockSpec((1, s, w), lambda p, bi, *_: (bi, 0, p))
    return pl.pallas_call(
        functools.partial(_moba_kernel, fixed_shift=fixed_shift, n_steps=int(qtab.shape[0])),
        out_shape=jax.ShapeDtypeStruct((b, s, MOBA_WIDTH), BF16),
        grid_spec=pltpu.PrefetchScalarGridSpec(
            num_scalar_prefetch=2,
            grid=(N_PAIRS, b),
            in_specs=[pl.BlockSpec(memory_space=pltpu.SMEM),
                      seq(2 * LANES), seq(2 * LANES), seq(LANES),
                      pl.BlockSpec((1, 2, nb, MOBA_BLOCK, MOBA_BLOCK), lambda p, bi, *_: (p, 0, 0, 0, 0),
                                   pipeline_mode=pl.Buffered(1)),
                      seq(LANES)],
            out_specs=seq(LANES),
            scratch_shapes=[pltpu.VMEM((2, t, LANES), F32), pltpu.VMEM((2, t, 2 * LANES), F32),
                            pltpu.VMEM((2, t, t), F32), pltpu.VMEM((2, t, t), F32)],
        ),
        compiler_params=pltpu.CompilerParams(dimension_semantics=("arbitrary", "arbitrary"),
                                             vmem_limit_bytes=VMEM_LIMIT),
        name="moba_attention",
    )(qtab, utab, keep, qx, kb, vb, bias_tiles, sgb)


def _proj_out_kernel(am_ref, ab_ref, w_ref, x_ref, mod_ref, o_ref):
    y = (jnp.dot(am_ref[0], w_ref[0:MLA_WIDTH, :], preferred_element_type=F32)
         + jnp.dot(ab_ref[0], w_ref[MLA_WIDTH:, :], preferred_element_type=F32))
    o_ref[0] = x_ref[0] + mod_ref[0, 2:3, :] * y


def _proj_out_call(a_mla, a_moba, w_out, x, mod3):
    b, s, d = x.shape
    tok = lambda w: pl.BlockSpec((1, TM_OUT, w), lambda bi, i: (bi, i, 0))
    return pl.pallas_call(
        _proj_out_kernel,
        out_shape=jax.ShapeDtypeStruct((b, s, d), F32),
        grid=(b, s // TM_OUT),
        in_specs=[tok(MLA_WIDTH), tok(MOBA_WIDTH), pl.BlockSpec(w_out.shape, lambda bi, i: (0, 0)),
                  tok(d), pl.BlockSpec((1, 3, d), lambda bi, i: (bi, 0, 0))],
        out_specs=tok(d),
        compiler_params=pltpu.CompilerParams(dimension_semantics=("arbitrary", "arbitrary"),
                                             vmem_limit_bytes=VMEM_LIMIT),
        name="proj_out",
    )(a_mla, a_moba, w_out, x, mod3)


def _to_head_lanes(a):
    a = jnp.concatenate([a, jnp.zeros(a.shape[:-1] + (1,), a.dtype)], axis=-1)
    return a[..., _LANE_SRC]


def _layer_weights(w_in, w_uq, w_ukv, q_g, k_g, bq_g, bk_g):
    o0 = Q_LORA
    o1 = o0 + KV_LORA
    o2 = o1 + ROPE_DIM
    k_rope = w_in[:, o1:o2]
    kr128 = jnp.zeros((D_MODEL, LANES), w_in.dtype)
    kr128 = kr128.at[:, 0:HALF_ROPE].set(k_rope[:, :HALF_ROPE])
    kr128 = kr128.at[:, LANES // 2:LANES // 2 + HALF_ROPE].set(k_rope[:, HALF_ROPE:])
    win = jnp.concatenate([w_in[:, :o1], kr128, w_in[:, o2:]], axis=1).astype(BF16)
    wuq = _to_head_lanes(w_uq.reshape(Q_LORA, H_MLA, QK_DIM)).reshape(Q_LORA, H_MLA * LANES)
    ukv = w_ukv.reshape(KV_LORA, H_MLA, NOPE_DIM + HEAD_DIM)
    k_nope = jnp.concatenate([ukv[..., :NOPE_DIM], jnp.zeros((KV_LORA, H_MLA, ROPE_DIM), ukv.dtype)], -1)
    wuk = _to_head_lanes(k_nope).reshape(KV_LORA, H_MLA * LANES)
    wuv = ukv[..., NOPE_DIM:].reshape(KV_LORA, MLA_WIDTH)
    pair = lambda g: jnp.concatenate([g, g]).reshape(1, LANES)
    return (win, wuq.astype(BF16), wuk.astype(BF16), wuv.astype(BF16),
            _to_head_lanes(q_g * MLA_QSCALE).reshape(1, LANES), _to_head_lanes(k_g).reshape(1, LANES),
            pair(bq_g), pair(bk_g))


def kernel(x, c, positions, norm_g, w_ada, b_ada, w_in, mla_q_norm_g, mla_w_uq, mla_kv_norm_g,
           mla_w_ukv, mla_q_g, mla_k_g, moba_q_g, moba_k_g, w_out, rel_bias):
    b, s, d = x.shape
    depth = w_in.shape[0]
    nb = s // MOBA_BLOCK
    assert d == D_MODEL and s % TM == 0 and s % TQ_MLA == 0 and TM % MOBA_BLOCK == 0 and b <= 8
    assert s % TG == 0 and TG % MOBA_BLOCK == 0 and s % MOBA_TILE == 0 and nb <= LANES // 4
    _check_far_tiles(s)

    c8 = jnp.pad(c, ((0, 8 - b), (0, 0)))
    mod = _mod_call(c8, w_ada, b_ada.reshape(depth, 1, 3 * d))[:, :b].reshape(depth, b, 3, d)

    inv_freq = ROPE_THETA ** (-jnp.arange(0, HALF_ROPE, dtype=F32) / HALF_ROPE)
    freq = jnp.zeros((LANES,), F32).at[0:HALF_ROPE].set(inv_freq)
    freq = freq.at[LANES // 2:LANES // 2 + HALF_ROPE].set(inv_freq).reshape(1, LANES)
    sign = jnp.zeros((LANES,), F32).at[0:HALF_ROPE].set(-1.0)
    sign = sign.at[LANES // 2:LANES // 2 + HALF_ROPE].set(1.0).reshape(1, LANES)
    rope_c, rope_s = _rope_call(positions.reshape(b * s, 1), freq, sign)

    bkt = _rel_bucket(jnp.arange(s, dtype=jnp.int32))
    thr = jnp.sum(bkt[None, :] < jnp.arange(N_BUCKETS, dtype=jnp.int32)[:, None], axis=1).astype(jnp.int32)
    ttab, vtab = _far_tables(bkt, rel_bias, s)
    bias_tiles = _bias_tile_call(thr, ttab, rel_bias.astype(F32).reshape(-1), vtab, nb)
    bias_tiles = bias_tiles.reshape(N_PAIRS, 2, nb, MOBA_BLOCK, MOBA_BLOCK)
    kpad = jnp.zeros((1, LANES), F32).at[0, SHIFT_LANE].set(1.0)
    amax = lambda a: jnp.max(jnp.abs(a.astype(F32)))

    def attention(fixed_shift, qm, km, vm, sgm, qx, kb, vb, sgb):
        return (_mla_call(qm, km, vm, sgm, fixed_shift),
                _moba_call(qx, kb, vb, bias_tiles, sgb, fixed_shift))

    for l in range(depth):
        win, wuq, wuk, wuv, qg, kg, bqg, bkg = _layer_weights(
            w_in[l], mla_w_uq[l], mla_w_ukv[l], mla_q_g[l], mla_k_g[l], moba_q_g[l], moba_k_g[l])
        bound_mla = QK_DIM * amax(mla_q_g[l]) * amax(mla_k_g[l]) * MLA_QSCALE * SHIFT_SLACK
        bound_moba = (HEAD_DIM * amax(moba_q_g[l]) * amax(moba_k_g[l]) * MOBA_QSCALE
                      + amax(rel_bias) * LOG2E) * SHIFT_SLACK
        fixed_ok = (bound_mla <= MAX_SHIFT) & (bound_moba <= MAX_SHIFT)
        qpad = jnp.zeros((1, LANES), F32).at[0, SHIFT_LANE].set(jnp.where(fixed_ok, -bound_mla, 0.0))
        qm, km, vm, sgm, qb, kb, vb, sgb, kmean = _proj_in_call(
            x, mod[l], norm_g[l].reshape(1, d), win, mla_q_norm_g[l].reshape(1, Q_LORA), wuq,
            mla_kv_norm_g[l].reshape(1, KV_LORA), wuk, wuv, qg, kg, bqg, bkg, rope_c, rope_s,
            qpad, kpad)
        qx = _moba_gate_call(jnp.where(fixed_ok, -bound_moba, 0.0).reshape(1), qb,
                             kmean.reshape(b, nb, MOBA_WIDTH))
        a_mla, a_moba = lax.cond(
            fixed_ok, functools.partial(attention, True), functools.partial(attention, False),
            qm, km, vm, sgm, qx, kb, vb, sgb)
        x = _proj_out_call(a_mla, a_moba, w_out[l].astype(BF16), x, mod[l])
    return x
```

```python
import functools
import math

import numpy as np
import jax
import jax.numpy as jnp
from jax import lax
from jax.experimental import pallas as pl
from jax.experimental.pallas import tpu as pltpu

F32 = jnp.float32
BF16 = jnp.bfloat16

D_MODEL = 1024
DEPTH = 2
HEAD_DIM = 64
H_MLA = 8
H_MOBA = 8
MLA_WIDTH = H_MLA * HEAD_DIM
MOBA_WIDTH = H_MOBA * HEAD_DIM
Q_LORA = 256
KV_LORA = 128
NOPE_DIM = 64
ROPE_DIM = 32
QK_DIM = NOPE_DIM + ROPE_DIM
ROPE_THETA = 10000.0
MOBA_BLOCK = 256
MOBA_TOPK = 3
N_BUCKETS = 32
REL_MAX_DIST = 4096
EPS = 1e-6

LANES = 128
HALF_ROPE = ROPE_DIM // 2
N_PAIRS = H_MLA // 2
LOG2E = math.log2(math.e)
MLA_QSCALE = QK_DIM ** -0.5 * LOG2E
MOBA_QSCALE = HEAD_DIM ** -0.5 * LOG2E
NEG_BIG = -1e30
FAR_THRESHOLDS = 2
NEAR_TILES = 3
NEVER = 1 << 20
PEN_OFF = -2.0 ** 127
SHIFT_LANE = QK_DIM
PEN_LANE = HEAD_DIM
MAX_SHIFT = 48.0
SHIFT_SLACK = 1.02

TM = 256
TM_OUT = 512
TG = 512
TQ_MLA = 512
MOBA_TILE = 2 * MOBA_BLOCK
STEPS_PER_ITER = 4
VMEM_LIMIT = 56 * 1024 * 1024

_NT = (((1,), (1,)), ((), ()))


def _head_lane_source():
    src = np.full((LANES,), QK_DIM, np.int32)
    src[0:16] = NOPE_DIM + np.arange(16)
    src[16:64] = np.arange(48)
    src[64:80] = NOPE_DIM + HALF_ROPE + np.arange(16)
    src[80:96] = 48 + np.arange(16)
    return src


_LANE_SRC = _head_lane_source()


def _bucket_np(n, dtype):
    n = np.asarray(n)
    max_exact = N_BUCKETS // 2
    nf = np.maximum(n, 1).astype(dtype)
    large = max_exact + (np.log(nf / dtype(max_exact)) / dtype(math.log(REL_MAX_DIST / max_exact))
                         * dtype(N_BUCKETS - max_exact)).astype(np.int32)
    large = np.minimum(large, N_BUCKETS - 1)
    return np.where(n < max_exact, n, large)


def _check_far_tiles(seq):
    n = np.arange(seq + 2)
    for dtype in (np.float32, np.float64):
        b = _bucket_np(n, dtype)
        chg = np.concatenate([[0], (b[1:] != b[:-1]).astype(np.int64)])
        for d in range(NEAR_TILES, seq // MOBA_BLOCK):
            lo, hi = MOBA_BLOCK * (d - 1) + 1, MOBA_BLOCK * (d + 1) - 1
            assert chg[lo - 1:hi + 1].sum() <= FAR_THRESHOLDS, (d, dtype)


def _rel_bucket(dist):
    n = jnp.maximum(dist, 0)
    max_exact = N_BUCKETS // 2
    nf = jnp.maximum(n, 1).astype(F32)
    large = max_exact + (jnp.log(nf / max_exact) / math.log(REL_MAX_DIST / max_exact)
                         * (N_BUCKETS - max_exact)).astype(jnp.int32)
    large = jnp.minimum(large, N_BUCKETS - 1)
    return jnp.where(n < max_exact, n, large)


def _rms(x, g):
    return x * lax.rsqrt(jnp.mean(x * x, axis=-1, keepdims=True) + EPS) * g


def _silu(x):
    return x * jax.nn.sigmoid(x)


def _mod_kernel(c_ref, w_ref, b_ref, o_ref):
    c = c_ref[...]
    o_ref[0] = jnp.dot(_silu(c), w_ref[0], precision=lax.Precision.HIGHEST,
                       preferred_element_type=F32) + b_ref[0]


def _mod_call(c8, w_ada, b_ada):
    depth, d, d3 = w_ada.shape
    nchunk = d3 // d
    return pl.pallas_call(
        _mod_kernel,
        out_shape=jax.ShapeDtypeStruct((depth, 8, d3), F32),
        grid=(depth, nchunk),
        in_specs=[pl.BlockSpec((8, d), lambda l, j: (0, 0)),
                  pl.BlockSpec((1, d, d), lambda l, j: (l, 0, j)),
                  pl.BlockSpec((1, 1, d), lambda l, j: (l, 0, j))],
        out_specs=pl.BlockSpec((1, 8, d), lambda l, j: (l, 0, j)),
        compiler_params=pltpu.CompilerParams(dimension_semantics=("arbitrary", "arbitrary"),
                                             vmem_limit_bytes=VMEM_LIMIT),
        name="adaln_mod",
    )(c8, w_ada, b_ada)


def _rope_kernel(pos_ref, freq_ref, sign_ref, c_ref, s_ref):
    ang = pos_ref[...].astype(F32) * freq_ref[...]
    c_ref[...] = jnp.cos(ang)
    s_ref[...] = jnp.sin(ang) * sign_ref[...]


def _rope_call(pos_col, freq, sign):
    n = pos_col.shape[0]
    tr = 1024
    return pl.pallas_call(
        _rope_kernel,
        out_shape=(jax.ShapeDtypeStruct((n, LANES), F32), jax.ShapeDtypeStruct((n, LANES), F32)),
        grid=(n // tr,),
        in_specs=[pl.BlockSpec((tr, 1), lambda i: (i, 0)),
                  pl.BlockSpec((1, LANES), lambda i: (0, 0)),
                  pl.BlockSpec((1, LANES), lambda i: (0, 0))],
        out_specs=(pl.BlockSpec((tr, LANES), lambda i: (i, 0)),
                   pl.BlockSpec((tr, LANES), lambda i: (i, 0))),
        compiler_params=pltpu.CompilerParams(dimension_semantics=("arbitrary",)),
        name="rope_tables",
    )(pos_col, freq, sign)


def _bias_tile_kernel(thr_ref, ttab_ref, rb_ref, vtab_ref, o_ref):
    h = pl.program_id(0)
    nb = o_ref.shape[1]
    shape = (MOBA_BLOCK, MOBA_BLOCK)
    rc = lax.broadcasted_iota(jnp.int32, shape, 0) - lax.broadcasted_iota(jnp.int32, shape, 1)

    for d in range(NEAR_TILES):
        dist = rc + d * MOBA_BLOCK
        val = jnp.full(shape, rb_ref[h], F32)
        for b in range(1, N_BUCKETS):
            val = jnp.where(dist >= thr_ref[b], rb_ref[b * H_MOBA + h], val)
        o_ref[0, d] = jnp.where(dist >= 0, val * LOG2E, NEG_BIG)

    def far(d, carry):
        base = (d * H_MOBA + h) * (FAR_THRESHOLDS + 1)
        val = jnp.full(shape, vtab_ref[base], F32)
        for k in range(FAR_THRESHOLDS):
            val = jnp.where(rc >= ttab_ref[d * FAR_THRESHOLDS + k], vtab_ref[base + k + 1], val)
        o_ref[0, d] = val
        return carry

    lax.fori_loop(NEAR_TILES, nb, far, 0)


def _bias_tile_call(thr, ttab, rb_flat, vtab, nb):
    smem = pl.BlockSpec(memory_space=pltpu.SMEM)
    return pl.pallas_call(
        _bias_tile_kernel,
        out_shape=jax.ShapeDtypeStruct((H_MOBA, nb, MOBA_BLOCK, MOBA_BLOCK), F32),
        grid_spec=pltpu.PrefetchScalarGridSpec(
            num_scalar_prefetch=2,
            grid=(H_MOBA,),
            in_specs=[smem, smem],
            out_specs=pl.BlockSpec((1, nb, MOBA_BLOCK, MOBA_BLOCK), lambda h, *_: (h, 0, 0, 0)),
        ),
        compiler_params=pltpu.CompilerParams(dimension_semantics=("arbitrary",),
                                             vmem_limit_bytes=VMEM_LIMIT),
        name="moba_bias_tiles",
    )(thr, ttab, rb_flat, vtab)


def _far_tables(bkt, rel_bias, seq):
    nb = seq // MOBA_BLOCK
    n = jnp.arange(seq, dtype=jnp.int32)
    chg = jnp.concatenate([jnp.zeros((1,), bool), bkt[1:] != bkt[:-1]])
    d = jnp.arange(nb, dtype=jnp.int32)[:, None]
    lo = jnp.maximum(MOBA_BLOCK * (d - 1) + 1, 0)
    hi = MOBA_BLOCK * (d + 1) - 1
    inside = chg[None, :] & (n[None, :] > lo) & (n[None, :] <= hi)
    cum = jnp.cumsum(inside.astype(jnp.int32), axis=1)
    idx = [bkt[jnp.minimum(lo[:, 0], seq - 1)]]
    thr = []
    for k in range(1, FAR_THRESHOLDS + 1):
        t = jnp.min(jnp.where(inside & (cum == k), n[None, :], NEVER), axis=1)
        thr.append(t - MOBA_BLOCK * d[:, 0])
        idx.append(bkt[jnp.minimum(t, seq - 1)])
    ttab = jnp.stack(thr, axis=1).reshape(-1).astype(jnp.int32)
    vals = rel_bias.astype(F32)[jnp.stack(idx, axis=1)] * LOG2E
    vtab = vals.transpose(0, 2, 1).reshape(-1)
    return ttab, vtab


def _mla_head(xh, g, rc, rs):
    ms = jnp.sum(xh * xh, axis=-1, keepdims=True) * (1.0 / QK_DIM)
    xn = xh * lax.rsqrt(ms + EPS) * g
    return xn * rc + pltpu.roll(xn, LANES // 2, 1) * rs


def _moba_pair_norm(x, g, lo_half):
    x2 = x * x
    s_lo = jnp.sum(jnp.where(lo_half, x2, 0.0), axis=-1, keepdims=True)
    s_hi = jnp.sum(jnp.where(lo_half, 0.0, x2), axis=-1, keepdims=True)
    ms = jnp.where(lo_half, s_lo, s_hi) * (1.0 / HEAD_DIM)
    return x * lax.rsqrt(ms + EPS) * g


def _proj_in_kernel(x_ref, mod_ref, ng_ref, win_ref, qng_ref, wuq_ref, kvng_ref, wuk_ref, wuv_ref,
                    qg_ref, kg_ref, bqg_ref, bkg_ref, rc_ref, rs_ref, qpad_ref, kpad_ref,
                    qm_ref, km_ref, vm_ref, sgm_ref, qb_ref, kb_ref, vb_ref, sgb_ref, kmean_ref):
    tile = pl.program_id(1)
    x = x_ref[0]
    h = _rms(x, ng_ref[...] * (1.0 + mod_ref[0, 1:2, :])) + mod_ref[0, 0:1, :]
    hb = h.astype(BF16)

    def proj(lo, hi):
        return jnp.dot(hb, win_ref[:, lo:hi], preferred_element_type=F32)

    rc = rc_ref[...]
    rs = rs_ref[...]

    cqn = _rms(proj(0, 256), qng_ref[...]).astype(BF16)
    q = jnp.dot(cqn, wuq_ref[...], preferred_element_type=F32)
    qg = qg_ref[...]
    qpad = qpad_ref[...]
    kpad = kpad_ref[...]
    for hh in range(H_MLA):
        sl = slice(LANES * hh, LANES * (hh + 1))
        qm_ref[0, :, sl] = (_mla_head(q[:, sl], qg, rc, rs) + qpad).astype(BF16)

    ckvn = _rms(proj(256, 384), kvng_ref[...]).astype(BF16)
    kn = jnp.dot(ckvn, wuk_ref[...], preferred_element_type=F32)
    kr = proj(384, 512)
    kg = kg_ref[...]
    for hh in range(H_MLA):
        sl = slice(LANES * hh, LANES * (hh + 1))
        km_ref[0, :, sl] = (_mla_head(kn[:, sl] + kr, kg, rc, rs) + kpad).astype(BF16)
    vm_ref[0] = jnp.dot(ckvn, wuv_ref[...], preferred_element_type=F32).astype(BF16)
    sgm_ref[0] = _silu(proj(512, 1024)).astype(BF16)

    lane = lax.broadcasted_iota(jnp.int32, (TM, LANES), 1)
    lo_half = lane < HEAD_DIM
    blk_id = tile * (TM // MOBA_BLOCK) + jnp.right_shift(
        lax.broadcasted_iota(jnp.int32, (TM, LANES), 0), MOBA_BLOCK.bit_length() - 1)
    onehot = (lane == PEN_LANE + blk_id).astype(F32)
    qbz = proj(1024, 1536)
    kbz = proj(1536, 2048)
    bqg = bqg_ref[...]
    bkg = bkg_ref[...]
    for p in range(N_PAIRS):
        sl = slice(LANES * p, LANES * (p + 1))
        qb_ref[0, :, sl] = _moba_pair_norm(qbz[:, sl], bqg, lo_half)
        kn_p = _moba_pair_norm(kbz[:, sl], bkg, lo_half)
        kb_ref[0, :, LANES * 2 * p:LANES * (2 * p + 1)] = jnp.where(lo_half, kn_p, onehot).astype(BF16)
        kb_ref[0, :, LANES * (2 * p + 1):LANES * (2 * p + 2)] = jnp.where(
            lo_half, pltpu.roll(kn_p, HEAD_DIM, 1), onehot).astype(BF16)
        for blk in range(TM // MOBA_BLOCK):
            rows = kn_p[MOBA_BLOCK * blk:MOBA_BLOCK * (blk + 1)]
            kmean_ref[0, 0, blk:blk + 1, sl] = jnp.sum(rows, axis=0, keepdims=True) * (1.0 / MOBA_BLOCK)
    vb_ref[0] = proj(2048, 2560).astype(BF16)
    sgb_ref[0] = _silu(proj(2560, 3072)).astype(BF16)


def _proj_in_call(x, mod3, ng, win, qng, wuq, kvng, wuk, wuv, qg, kg, bqg, bkg, rope_c, rope_s,
                  qpad, kpad):
    b, s, d = x.shape
    nt = s // TM
    const2 = lambda bi, i: (0, 0)
    tok = lambda w: pl.BlockSpec((1, TM, w), lambda bi, i: (bi, i, 0))
    full = lambda a: pl.BlockSpec(a.shape, const2)
    rope_spec = pl.BlockSpec((TM, LANES), lambda bi, i: (bi * nt + i, 0))
    out_shape = (
        jax.ShapeDtypeStruct((b, s, H_MLA * LANES), BF16),
        jax.ShapeDtypeStruct((b, s, H_MLA * LANES), BF16),
        jax.ShapeDtypeStruct((b, s, MLA_WIDTH), BF16),
        jax.ShapeDtypeStruct((b, s, MLA_WIDTH), BF16),
        jax.ShapeDtypeStruct((b, s, MOBA_WIDTH), F32),
        jax.ShapeDtypeStruct((b, s, H_MOBA * LANES), BF16),
        jax.ShapeDtypeStruct((b, s, MOBA_WIDTH), BF16),
        jax.ShapeDtypeStruct((b, s, MOBA_WIDTH), BF16),
        jax.ShapeDtypeStruct((b, nt, TM // MOBA_BLOCK, MOBA_WIDTH), F32),
    )
    out_specs = (tok(H_MLA * LANES), tok(H_MLA * LANES), tok(MLA_WIDTH), tok(MLA_WIDTH),
                 tok(MOBA_WIDTH), tok(H_MOBA * LANES), tok(MOBA_WIDTH), tok(MOBA_WIDTH),
                 pl.BlockSpec((1, 1, TM // MOBA_BLOCK, MOBA_WIDTH), lambda bi, i: (bi, i, 0, 0)))
    return pl.pallas_call(
        _proj_in_kernel,
        out_shape=out_shape,
        grid=(b, nt),
        in_specs=[tok(d), pl.BlockSpec((1, 3, d), lambda bi, i: (bi, 0, 0)), full(ng), full(win),
                  full(qng), full(wuq), full(kvng), full(wuk), full(wuv),
                  full(qg), full(kg), full(bqg), full(bkg), rope_spec, rope_spec,
                  full(qpad), full(kpad)],
        out_specs=out_specs,
        compiler_params=pltpu.CompilerParams(dimension_semantics=("arbitrary", "arbitrary"),
                                             vmem_limit_bytes=VMEM_LIMIT),
        name="proj_in",
    )(x, mod3, ng, win, qng, wuq, kvng, wuk, wuv, qg, kg, bqg, bkg, rope_c, rope_s, qpad, kpad)


def _moba_gate_kernel(neg_shift_ref, q_ref, kmean_ref, qx_ref):
    nb = kmean_ref.shape[1]
    lo_k = lax.broadcasted_iota(jnp.int32, (nb, LANES), 1) < HEAD_DIM
    lo_q = lax.broadcasted_iota(jnp.int32, (TG, LANES), 1) < HEAD_DIM
    blk = lax.broadcasted_iota(jnp.int32, (nb, TG), 0)
    blkf = blk.astype(F32)
    own = pl.program_id(1) * (TG // MOBA_BLOCK) + jnp.right_shift(
        lax.broadcasted_iota(jnp.int32, (nb, TG), 1), MOBA_BLOCK.bit_length() - 1)
    neg_shift = neg_shift_ref[0]
    r = lax.broadcasted_iota(jnp.int32, (2 * nb, 2 * LANES), 0)
    to_lane = (lax.broadcasted_iota(jnp.int32, (2 * nb, 2 * LANES), 1)
               == jnp.where(r < nb, PEN_LANE + r, LANES + PEN_LANE + r - nb)).astype(BF16)
    for p in range(N_PAIRS):
        sl = slice(LANES * p, LANES * (p + 1))
        km = kmean_ref[0, :, sl]
        q = q_ref[0, :, sl]
        km2 = jnp.concatenate([jnp.where(lo_k, km, 0.0), jnp.where(lo_k, 0.0, km)], axis=0)
        g2 = lax.dot_general(km2, q, _NT, precision=lax.Precision.HIGHEST,
                             preferred_element_type=F32)
        pens = []
        for hh in range(2):
            rem = jnp.where(blk < own, g2[nb * hh:nb * (hh + 1)], -jnp.inf)
            on = blk == own
            for k in range(MOBA_TOPK):
                mx = jnp.max(rem, axis=0, keepdims=True)
                idx = jnp.min(jnp.where(rem == mx, blkf, float(nb)), axis=0, keepdims=True)
                hit = blkf == idx
                on = on | (hit & (own > k))
                rem = jnp.where(hit, -jnp.inf, rem)
            pens.append(jnp.where(on, neg_shift, PEN_OFF).astype(BF16))
        pen_lanes = lax.dot_general(jnp.concatenate(pens, axis=0), to_lane, (((0,), (0,)), ((), ())),
                                    preferred_element_type=F32)
        qs = q * MOBA_QSCALE
        qs2 = jnp.concatenate([jnp.where(lo_q, qs, 0.0),
                               pltpu.roll(jnp.where(lo_q, 0.0, qs), HEAD_DIM, 1)], axis=1)
        qx_ref[0, :, 2 * LANES * p:2 * LANES * (p + 1)] = (qs2 + pen_lanes).astype(BF16)


def _moba_gate_call(neg_shift, qb, kmean):
    b, s, _ = qb.shape
    nb = kmean.shape[1]
    return pl.pallas_call(
        _moba_gate_kernel,
        out_shape=jax.ShapeDtypeStruct((b, s, H_MOBA * LANES), BF16),
        grid=(b, s // TG),
        in_specs=[pl.BlockSpec(memory_space=pltpu.SMEM),
                  pl.BlockSpec((1, TG, MOBA_WIDTH), lambda bi, i: (bi, i, 0)),
                  pl.BlockSpec((1, nb, MOBA_WIDTH), lambda bi, i: (bi, 0, 0))],
        out_specs=pl.BlockSpec((1, TG, H_MOBA * LANES), lambda bi, i: (bi, i, 0)),
        compiler_params=pltpu.CompilerParams(dimension_semantics=("arbitrary", "arbitrary"),
                                             vmem_limit_bytes=VMEM_LIMIT),
        name="moba_gate",
    )(neg_shift, qb, kmean)


def _flash_update(hh, s, v_ext, m_sc, acc_sc, first, keep, fixed_shift):
    if fixed_shift:
        acc_sc[hh] = acc_sc[hh] * keep + jnp.dot(jnp.exp2(s).astype(BF16), v_ext,
                                                 preferred_element_type=F32)
        return
    m_prev = jnp.where(first == 1, NEG_BIG, m_sc[hh])
    m_new = jnp.maximum(m_prev, jnp.max(s, axis=-1, keepdims=True))
    alpha = jnp.exp2(m_prev - m_new)
    p = jnp.concatenate([jnp.exp2(s[:, LANES * c:LANES * (c + 1)] - m_new)
                         for c in range(s.shape[1] // LANES)], axis=1).astype(BF16)
    pv = jnp.dot(p, v_ext, preferred_element_type=F32)
    acc_sc[hh] = jnp.concatenate([alpha, alpha], axis=1) * acc_sc[hh] + pv
    m_sc[hh] = m_new


def _with_ones(v):
    return jnp.concatenate([v, jnp.ones(v.shape, v.dtype)], axis=1)


def _flash_init(m_sc, acc_sc):
    m_sc[...] = jnp.full(m_sc.shape, NEG_BIG, F32)
    acc_sc[...] = jnp.zeros(acc_sc.shape, F32)


def _flash_finish(o_ref, sg_ref, rows, acc_sc):
    lo_half = lax.broadcasted_iota(jnp.int32, (acc_sc.shape[1], LANES), 1) < HEAD_DIM
    o = jnp.where(lo_half, acc_sc[0, :, :LANES] / acc_sc[0, :, LANES:],
                  acc_sc[1, :, :LANES] / acc_sc[1, :, LANES:])
    o_ref[0, rows, :] = (o * sg_ref[0, rows, :].astype(F32)).astype(o_ref.dtype)


def _causal_steps(n_tiles):
    pairs = [(qi, u) for qi in range(n_tiles) for u in range(qi + 1)]
    return (jnp.asarray([p[0] for p in pairs], jnp.int32), jnp.asarray([p[1] for p in pairs], jnp.int32),
            jnp.asarray([float(p[1] > 0) for p in pairs], F32))


def _pipelined(n_steps, produce, consume, bufs):
    produce(0, bufs[0])
    iters = (n_steps - 1) // STEPS_PER_ITER

    def body(it, carry):
        for j in range(STEPS_PER_ITER):
            n = STEPS_PER_ITER * it + j
            produce(n + 1, bufs[(j + 1) % 2])
            consume(n, bufs[j % 2])
        return carry

    lax.fori_loop(0, iters, body, 0)
    for n in range(STEPS_PER_ITER * iters, n_steps):
        if n + 1 < n_steps:
            produce(n + 1, bufs[(n + 1) % 2])
        consume(n, bufs[n % 2])


def _mla_kernel(qtab_ref, utab_ref, keep_ref, q_ref, k_ref, v_ref, sg_ref, o_ref,
                m_sc, acc_sc, sa_sc, sb_sc, mask_sc, *, fixed_shift, n_steps):
    t = TQ_MLA
    _flash_init(m_sc, acc_sc)
    row = lax.broadcasted_iota(jnp.int32, (t, t), 0)
    col = lax.broadcasted_iota(jnp.int32, (t, t), 1)
    mask_sc[0] = jnp.zeros((t, t), F32)
    mask_sc[1] = jnp.where(col <= row, 0.0, NEG_BIG)

    def produce(n, buf):
        qi, u = qtab_ref[n], utab_ref[n]
        qrows = pl.ds(pl.multiple_of(qi * t, t), t)
        krows = pl.ds(pl.multiple_of(u * t, t), t)
        diag = (u == qi).astype(jnp.int32)
        for hh in range(2):
            sl = slice(LANES * hh, LANES * (hh + 1))
            buf[hh] = lax.dot_general(q_ref[0, qrows, sl], k_ref[0, krows, sl], _NT,
                                      preferred_element_type=F32) + mask_sc[diag]

    def consume(n, buf):
        qi, u = qtab_ref[n], utab_ref[n]
        v = _with_ones(v_ref[0, pl.ds(pl.multiple_of(u * t, t), t), :])
        first = (u == 0).astype(jnp.int32)
        for hh in range(2):
            _flash_update(hh, buf[hh], v, m_sc, acc_sc, first, keep_ref[n], fixed_shift)
        _flash_finish(o_ref, sg_ref, pl.ds(pl.multiple_of(qi * t, t), t), acc_sc)

    _pipelined(n_steps, produce, consume, (sa_sc, sb_sc))


def _mla_call(qm, km, vm, sgm, fixed_shift):
    b, s, _ = qm.shape
    t = TQ_MLA
    qtab, utab, keep = _causal_steps(s // t)
    seq = lambda w: pl.BlockSpec((1, s, w), lambda p, bi, *_: (bi, 0, p))
    return pl.pallas_call(
        functools.partial(_mla_kernel, fixed_shift=fixed_shift, n_steps=int(qtab.shape[0])),
        out_shape=jax.ShapeDtypeStruct((b, s, MLA_WIDTH), BF16),
        grid_spec=pltpu.PrefetchScalarGridSpec(
            num_scalar_prefetch=2,
            grid=(N_PAIRS, b),
            in_specs=[pl.BlockSpec(memory_space=pltpu.SMEM),
                      seq(2 * LANES), seq(2 * LANES), seq(LANES), seq(LANES)],
            out_specs=seq(LANES),
            scratch_shapes=[pltpu.VMEM((2, t, LANES), F32), pltpu.VMEM((2, t, 2 * LANES), F32),
                            pltpu.VMEM((2, t, t), F32), pltpu.VMEM((2, t, t), F32),
                            pltpu.VMEM((2, t, t), F32)],
        ),
        compiler_params=pltpu.CompilerParams(dimension_semantics=("arbitrary", "arbitrary"),
                                             vmem_limit_bytes=VMEM_LIMIT),
        name="mla_attention",
    )(qtab, utab, keep, qm, km, vm, sgm)


def _moba_kernel(qtab_ref, utab_ref, keep_ref, qx_ref, k_ref, v_ref, bt_ref, sg_ref, o_ref,
                 m_sc, acc_sc, sa_sc, sb_sc, *, fixed_shift, n_steps):
    t = MOBA_BLOCK
    w = MOBA_TILE
    nsub = w // t
    _flash_init(m_sc, acc_sc)

    def produce(n, buf):
        qi, u = qtab_ref[n], utab_ref[n]
        qrows = pl.ds(pl.multiple_of(qi * w, w), w)
        krows = pl.ds(pl.multiple_of(u * w, w), w)
        for hh in range(2):
            sl = slice(LANES * hh, LANES * (hh + 1))
            s = lax.dot_general(qx_ref[0, qrows, sl], k_ref[0, krows, sl], _NT,
                                preferred_element_type=F32)
            for qs in range(nsub):
                for ks in range(nsub):
                    d = jnp.maximum(nsub * (qi - u) + (qs - ks), 0)
                    rows, cols = slice(t * qs, t * (qs + 1)), slice(t * ks, t * (ks + 1))
                    buf[hh, rows, cols] = s[rows, cols] + bt_ref[0, hh, d]

    def consume(n, buf):
        qi, u = qtab_ref[n], utab_ref[n]
        v = _with_ones(v_ref[0, pl.ds(pl.multiple_of(u * w, w), w), :])
        first = (u == 0).astype(jnp.int32)
        for hh in range(2):
            _flash_update(hh, buf[hh], v, m_sc, acc_sc, first, keep_ref[n], fixed_shift)
        _flash_finish(o_ref, sg_ref, pl.ds(pl.multiple_of(qi * w, w), w), acc_sc)

    _pipelined(n_steps, produce, consume, (sa_sc, sb_sc))


def _moba_call(qx, kb, vb, bias_tiles, sgb, fixed_shift):
    b, s, _ = vb.shape
    t = MOBA_TILE
    nb = s // MOBA_BLOCK
    qtab, utab, keep = _causal_steps(s // t)
    seq = lambda w: pl.BlockSpec((1, s, w), lambda p, bi, *_: (bi, 0, p))
    return pl.pallas_call(
        functools.partial(_moba_kernel, fixed_shift=fixed_shift, n_steps=int(qtab.shape[0])),
        out_shape=jax.ShapeDtypeStruct((b, s, MOBA_WIDTH), BF16),
        grid_spec=pltpu.PrefetchScalarGridSpec(
            num_scalar_prefetch=2,
            grid=(N_PAIRS, b),
            in_specs=[pl.BlockSpec(memory_space=pltpu.SMEM),
                      seq(2 * LANES), seq(2 * LANES), seq(LANES),
                      pl.BlockSpec((1, 2, nb, MOBA_BLOCK, MOBA_BLOCK), lambda p, bi, *_: (p, 0, 0, 0, 0),
                                   pipeline_mode=pl.Buffered(1)),
                      seq(LANES)],
            out_specs=seq(LANES),
            scratch_shapes=[pltpu.VMEM((2, t, LANES), F32), pltpu.VMEM((2, t, 2 * LANES), F32),
                            pltpu.VMEM((2, t, t), F32), pltpu.VMEM((2, t, t), F32)],
        ),
        compiler_params=pltpu.CompilerParams(dimension_semantics=("arbitrary", "arbitrary"),
                                             vmem_limit_bytes=VMEM_LIMIT),
        name="moba_attention",
    )(qtab, utab, keep, qx, kb, vb, bias_tiles, sgb)


def _proj_out_kernel(am_ref, ab_ref, w_ref, x_ref, mod_ref, o_ref):
    y = (jnp.dot(am_ref[0], w_ref[0:MLA_WIDTH, :], preferred_element_type=F32)
         + jnp.dot(ab_ref[0], w_ref[MLA_WIDTH:, :], preferred_element_type=F32))
    o_ref[0] = x_ref[0] + mod_ref[0, 2:3, :] * y


def _proj_out_call(a_mla, a_moba, w_out, x, mod3):
    b, s, d = x.shape
    tok = lambda w: pl.BlockSpec((1, TM_OUT, w), lambda bi, i: (bi, i, 0))
    return pl.pallas_call(
        _proj_out_kernel,
        out_shape=jax.ShapeDtypeStruct((b, s, d), F32),
        grid=(b, s // TM_OUT),
        in_specs=[tok(MLA_WIDTH), tok(MOBA_WIDTH), pl.BlockSpec(w_out.shape, lambda bi, i: (0, 0)),
                  tok(d), pl.BlockSpec((1, 3, d), lambda bi, i: (bi, 0, 0))],
        out_specs=tok(d),
        compiler_params=pltpu.CompilerParams(dimension_semantics=("arbitrary", "arbitrary"),
                                             vmem_limit_bytes=VMEM_LIMIT),
        name="proj_out",
    )(a_mla, a_moba, w_out, x, mod3)


def _to_head_lanes(a):
    a = jnp.concatenate([a, jnp.zeros(a.shape[:-1] + (1,), a.dtype)], axis=-1)
    return a[..., _LANE_SRC]


def _layer_weights(w_in, w_uq, w_ukv, q_g, k_g, bq_g, bk_g):
    o0 = Q_LORA
    o1 = o0 + KV_LORA
    o2 = o1 + ROPE_DIM
    k_rope = w_in[:, o1:o2]
    kr128 = jnp.zeros((D_MODEL, LANES), w_in.dtype)
    kr128 = kr128.at[:, 0:HALF_ROPE].set(k_rope[:, :HALF_ROPE])
    kr128 = kr128.at[:, LANES // 2:LANES // 2 + HALF_ROPE].set(k_rope[:, HALF_ROPE:])
    win = jnp.concatenate([w_in[:, :o1], kr128, w_in[:, o2:]], axis=1).astype(BF16)
    wuq = _to_head_lanes(w_uq.reshape(Q_LORA, H_MLA, QK_DIM)).reshape(Q_LORA, H_MLA * LANES)
    ukv = w_ukv.reshape(KV_LORA, H_MLA, NOPE_DIM + HEAD_DIM)
    k_nope = jnp.concatenate([ukv[..., :NOPE_DIM], jnp.zeros((KV_LORA, H_MLA, ROPE_DIM), ukv.dtype)], -1)
    wuk = _to_head_lanes(k_nope).reshape(KV_LORA, H_MLA * LANES)
    wuv = ukv[..., NOPE_DIM:].reshape(KV_LORA, MLA_WIDTH)
    pair = lambda g: jnp.concatenate([g, g]).reshape(1, LANES)
    return (win, wuq.astype(BF16), wuk.astype(BF16), wuv.astype(BF16),
            _to_head_lanes(q_g * MLA_QSCALE).reshape(1, LANES), _to_head_lanes(k_g).reshape(1, LANES),
            pair(bq_g), pair(bk_g))


def kernel(x, c, positions, norm_g, w_ada, b_ada, w_in, mla_q_norm_g, mla_w_uq, mla_kv_norm_g,
           mla_w_ukv, mla_q_g, mla_k_g, moba_q_g, moba_k_g, w_out, rel_bias):
    b, s, d = x.shape
    depth = w_in.shape[0]
    nb = s // MOBA_BLOCK
    assert d == D_MODEL and s % TM == 0 and s % TQ_MLA == 0 and TM % MOBA_BLOCK == 0 and b <= 8
    assert s % TG == 0 and TG % MOBA_BLOCK == 0 and s % MOBA_TILE == 0 and nb <= LANES // 4
    _check_far_tiles(s)

    c8 = jnp.pad(c, ((0, 8 - b), (0, 0)))
    mod = _mod_call(c8, w_ada, b_ada.reshape(depth, 1, 3 * d))[:, :b].reshape(depth, b, 3, d)

    inv_freq = ROPE_THETA ** (-jnp.arange(0, HALF_ROPE, dtype=F32) / HALF_ROPE)
    freq = jnp.zeros((LANES,), F32).at[0:HALF_ROPE].set(inv_freq)
    freq = freq.at[LANES // 2:LANES // 2 + HALF_ROPE].set(inv_freq).reshape(1, LANES)
    sign = jnp.zeros((LANES,), F32).at[0:HALF_ROPE].set(-1.0)
    sign = sign.at[LANES // 2:LANES // 2 + HALF_ROPE].set(1.0).reshape(1, LANES)
    rope_c, rope_s = _rope_call(positions.reshape(b * s, 1), freq, sign)

    bkt = _rel_bucket(jnp.arange(s, dtype=jnp.int32))
    thr = jnp.sum(bkt[None, :] < jnp.arange(N_BUCKETS, dtype=jnp.int32)[:, None], axis=1).astype(jnp.int32)
    ttab, vtab = _far_tables(bkt, rel_bias, s)
    bias_tiles = _bias_tile_call(thr, ttab, rel_bias.astype(F32).reshape(-1), vtab, nb)
    bias_tiles = bias_tiles.reshape(N_PAIRS, 2, nb, MOBA_BLOCK, MOBA_BLOCK)
    kpad = jnp.zeros((1, LANES), F32).at[0, SHIFT_LANE].set(1.0)
    amax = lambda a: jnp.max(jnp.abs(a.astype(F32)))

    def attention(fixed_shift, qm, km, vm, sgm, qx, kb, vb, sgb):
        return (_mla_call(qm, km, vm, sgm, fixed_shift),
                _moba_call(qx, kb, vb, bias_tiles, sgb, fixed_shift))

    for l in range(depth):
        win, wuq, wuk, wuv, qg, kg, bqg, bkg = _layer_weights(
            w_in[l], mla_w_uq[l], mla_w_ukv[l], mla_q_g[l], mla_k_g[l], moba_q_g[l], moba_k_g[l])
        bound_mla = QK_DIM * amax(mla_q_g[l]) * amax(mla_k_g[l]) * MLA_QSCALE * SHIFT_SLACK
        bound_moba = (HEAD_DIM * amax(moba_q_g[l]) * amax(moba_k_g[l]) * MOBA_QSCALE
                      + amax(rel_bias) * LOG2E) * SHIFT_SLACK
        fixed_ok = (bound_mla <= MAX_SHIFT) & (bound_moba <= MAX_SHIFT)
        qpad = jnp.zeros((1, LANES), F32).at[0, SHIFT_LANE].set(jnp.where(fixed_ok, -bound_mla, 0.0))
        qm, km, vm, sgm, qb, kb, vb, sgb, kmean = _proj_in_call(
            x, mod[l], norm_g[l].reshape(1, d), win, mla_q_norm_g[l].reshape(1, Q_LORA), wuq,
            mla_kv_norm_g[l].reshape(1, KV_LORA), wuk, wuv, qg, kg, bqg, bkg, rope_c, rope_s,
            qpad, kpad)
        qx = _moba_gate_call(jnp.where(fixed_ok, -bound_moba, 0.0).reshape(1), qb,
                             kmean.reshape(b, nb, MOBA_WIDTH))
        a_mla, a_moba = lax.cond(
            fixed_ok, functools.partial(attention, True), functools.partial(attention, False),
            qm, km, vm, sgm, qx, kb, vb, sgb)
        x = _proj_out_call(a_mla, a_moba, w_out[l].astype(BF16), x, mod[l])
    return x
```

```python
import functools
import math

import numpy as np
import jax
import jax.numpy as jnp
from jax import lax
from jax.experimental import pallas as pl
from jax.experimental.pallas import tpu as pltpu

F32 = jnp.float32
BF16 = jnp.bfloat16

D_MODEL = 1024
DEPTH = 2
HEAD_DIM = 64
H_MLA = 8
H_MOBA = 8
MLA_WIDTH = H_MLA * HEAD_DIM
MOBA_WIDTH = H_MOBA * HEAD_DIM
Q_LORA = 256
KV_LORA = 128
NOPE_DIM = 64
ROPE_DIM = 32
QK_DIM = NOPE_DIM + ROPE_DIM
ROPE_THETA = 10000.0
MOBA_BLOCK = 256
MOBA_TOPK = 3
N_BUCKETS = 32
REL_MAX_DIST = 4096
EPS = 1e-6

LANES = 128
HALF_ROPE = ROPE_DIM // 2
N_PAIRS = H_MLA // 2
LOG2E = math.log2(math.e)
MLA_QSCALE = QK_DIM ** -0.5 * LOG2E
MOBA_QSCALE = HEAD_DIM ** -0.5 * LOG2E
NEG_BIG = -1e30
FAR_THRESHOLDS = 2
NEAR_TILES = 3
NEVER = 1 << 20
PEN_OFF = -2.0 ** 127
SHIFT_LANE = QK_DIM
PEN_LANE = HEAD_DIM
MAX_SHIFT = 48.0
SHIFT_SLACK = 1.02

TM = 256
TM_OUT = 512
TG = 512
TQ_MLA = 512
MOBA_TILE = 2 * MOBA_BLOCK
STEPS_PER_ITER = 8
VMEM_LIMIT = 56 * 1024 * 1024

_NT = (((1,), (1,)), ((), ()))


def _head_lane_source():
    src = np.full((LANES,), QK_DIM, np.int32)
    src[0:16] = NOPE_DIM + np.arange(16)
    src[16:64] = np.arange(48)
    src[64:80] = NOPE_DIM + HALF_ROPE + np.arange(16)
    src[80:96] = 48 + np.arange(16)
    return src


_LANE_SRC = _head_lane_source()


def _bucket_np(n, dtype):
    n = np.asarray(n)
    max_exact = N_BUCKETS // 2
    nf = np.maximum(n, 1).astype(dtype)
    large = max_exact + (np.log(nf / dtype(max_exact)) / dtype(math.log(REL_MAX_DIST / max_exact))
                         * dtype(N_BUCKETS - max_exact)).astype(np.int32)
    large = np.minimum(large, N_BUCKETS - 1)
    return np.where(n < max_exact, n, large)


def _check_far_tiles(seq):
    n = np.arange(seq + 2)
    for dtype in (np.float32, np.float64):
        b = _bucket_np(n, dtype)
        chg = np.concatenate([[0], (b[1:] != b[:-1]).astype(np.int64)])
        for d in range(NEAR_TILES, seq // MOBA_BLOCK):
            lo, hi = MOBA_BLOCK * (d - 1) + 1, MOBA_BLOCK * (d + 1) - 1
            assert chg[lo - 1:hi + 1].sum() <= FAR_THRESHOLDS, (d, dtype)


def _rel_bucket(dist):
    n = jnp.maximum(dist, 0)
    max_exact = N_BUCKETS // 2
    nf = jnp.maximum(n, 1).astype(F32)
    large = max_exact + (jnp.log(nf / max_exact) / math.log(REL_MAX_DIST / max_exact)
                         * (N_BUCKETS - max_exact)).astype(jnp.int32)
    large = jnp.minimum(large, N_BUCKETS - 1)
    return jnp.where(n < max_exact, n, large)


def _rms(x, g):
    return x * lax.rsqrt(jnp.mean(x * x, axis=-1, keepdims=True) + EPS) * g


def _silu(x):
    return x * jax.nn.sigmoid(x)


def _mod_kernel(c_ref, w_ref, b_ref, o_ref):
    c = c_ref[...]
    o_ref[0] = jnp.dot(_silu(c), w_ref[0], precision=lax.Precision.HIGHEST,
                       preferred_element_type=F32) + b_ref[0]


def _mod_call(c8, w_ada, b_ada):
    depth, d, d3 = w_ada.shape
    nchunk = d3 // d
    return pl.pallas_call(
        _mod_kernel,
        out_shape=jax.ShapeDtypeStruct((depth, 8, d3), F32),
        grid=(depth, nchunk),
        in_specs=[pl.BlockSpec((8, d), lambda l, j: (0, 0)),
                  pl.BlockSpec((1, d, d), lambda l, j: (l, 0, j)),
                  pl.BlockSpec((1, 1, d), lambda l, j: (l, 0, j))],
        out_specs=pl.BlockSpec((1, 8, d), lambda l, j: (l, 0, j)),
        compiler_params=pltpu.CompilerParams(dimension_semantics=("arbitrary", "arbitrary"),
                                             vmem_limit_bytes=VMEM_LIMIT),
        name="adaln_mod",
    )(c8, w_ada, b_ada)


def _rope_kernel(pos_ref, freq_ref, sign_ref, c_ref, s_ref):
    ang = pos_ref[...].astype(F32) * freq_ref[...]
    c_ref[...] = jnp.cos(ang)
    s_ref[...] = jnp.sin(ang) * sign_ref[...]


def _rope_call(pos_col, freq, sign):
    n = pos_col.shape[0]
    tr = 1024
    return pl.pallas_call(
        _rope_kernel,
        out_shape=(jax.ShapeDtypeStruct((n, LANES), F32), jax.ShapeDtypeStruct((n, LANES), F32)),
        grid=(n // tr,),
        in_specs=[pl.BlockSpec((tr, 1), lambda i: (i, 0)),
                  pl.BlockSpec((1, LANES), lambda i: (0, 0)),
                  pl.BlockSpec((1, LANES), lambda i: (0, 0))],
        out_specs=(pl.BlockSpec((tr, LANES), lambda i: (i, 0)),
                   pl.BlockSpec((tr, LANES), lambda i: (i, 0))),
        compiler_params=pltpu.CompilerParams(dimension_semantics=("arbitrary",)),
        name="rope_tables",
    )(pos_col, freq, sign)


def _bias_tile_kernel(thr_ref, ttab_ref, rb_ref, vtab_ref, o_ref):
    h = pl.program_id(0)
    nb = o_ref.shape[1]
    shape = (MOBA_BLOCK, MOBA_BLOCK)
    rc = lax.broadcasted_iota(jnp.int32, shape, 0) - lax.broadcasted_iota(jnp.int32, shape, 1)

    for d in range(NEAR_TILES):
        dist = rc + d * MOBA_BLOCK
        val = jnp.full(shape, rb_ref[h], F32)
        for b in range(1, N_BUCKETS):
            val = jnp.where(dist >= thr_ref[b], rb_ref[b * H_MOBA + h], val)
        o_ref[0, d] = jnp.where(dist >= 0, val * LOG2E, NEG_BIG)

    def far(d, carry):
        base = (d * H_MOBA + h) * (FAR_THRESHOLDS + 1)
        val = jnp.full(shape, vtab_ref[base], F32)
        for k in range(FAR_THRESHOLDS):
            val = jnp.where(rc >= ttab_ref[d * FAR_THRESHOLDS + k], vtab_ref[base + k + 1], val)
        o_ref[0, d] = val
        return carry

    lax.fori_loop(NEAR_TILES, nb, far, 0)


def _bias_tile_call(thr, ttab, rb_flat, vtab, nb):
    smem = pl.BlockSpec(memory_space=pltpu.SMEM)
    return pl.pallas_call(
        _bias_tile_kernel,
        out_shape=jax.ShapeDtypeStruct((H_MOBA, nb, MOBA_BLOCK, MOBA_BLOCK), F32),
        grid_spec=pltpu.PrefetchScalarGridSpec(
            num_scalar_prefetch=2,
            grid=(H_MOBA,),
            in_specs=[smem, smem],
            out_specs=pl.BlockSpec((1, nb, MOBA_BLOCK, MOBA_BLOCK), lambda h, *_: (h, 0, 0, 0)),
        ),
        compiler_params=pltpu.CompilerParams(dimension_semantics=("arbitrary",),
                                             vmem_limit_bytes=VMEM_LIMIT),
        name="moba_bias_tiles",
    )(thr, ttab, rb_flat, vtab)


def _far_tables(bkt, rel_bias, seq):
    nb = seq // MOBA_BLOCK
    n = jnp.arange(seq, dtype=jnp.int32)
    chg = jnp.concatenate([jnp.zeros((1,), bool), bkt[1:] != bkt[:-1]])
    d = jnp.arange(nb, dtype=jnp.int32)[:, None]
    lo = jnp.maximum(MOBA_BLOCK * (d - 1) + 1, 0)
    hi = MOBA_BLOCK * (d + 1) - 1
    inside = chg[None, :] & (n[None, :] > lo) & (n[None, :] <= hi)
    cum = jnp.cumsum(inside.astype(jnp.int32), axis=1)
    idx = [bkt[jnp.minimum(lo[:, 0], seq - 1)]]
    thr = []
    for k in range(1, FAR_THRESHOLDS + 1):
        t = jnp.min(jnp.where(inside & (cum == k), n[None, :], NEVER), axis=1)
        thr.append(t - MOBA_BLOCK * d[:, 0])
        idx.append(bkt[jnp.minimum(t, seq - 1)])
    ttab = jnp.stack(thr, axis=1).reshape(-1).astype(jnp.int32)
    vals = rel_bias.astype(F32)[jnp.stack(idx, axis=1)] * LOG2E
    vtab = vals.transpose(0, 2, 1).reshape(-1)
    return ttab, vtab


def _mla_head(xh, g, rc, rs):
    ms = jnp.sum(xh * xh, axis=-1, keepdims=True) * (1.0 / QK_DIM)
    xn = xh * lax.rsqrt(ms + EPS) * g
    return xn * rc + pltpu.roll(xn, LANES // 2, 1) * rs


def _moba_pair_norm(x, g, lo_half):
    x2 = x * x
    s_lo = jnp.sum(jnp.where(lo_half, x2, 0.0), axis=-1, keepdims=True)
    s_hi = jnp.sum(jnp.where(lo_half, 0.0, x2), axis=-1, keepdims=True)
    ms = jnp.where(lo_half, s_lo, s_hi) * (1.0 / HEAD_DIM)
    return x * lax.rsqrt(ms + EPS) * g


def _proj_in_kernel(x_ref, mod_ref, ng_ref, win_ref, qng_ref, wuq_ref, kvng_ref, wuk_ref, wuv_ref,
                    qg_ref, kg_ref, bqg_ref, bkg_ref, rc_ref, rs_ref, qpad_ref, kpad_ref,
                    qm_ref, km_ref, vm_ref, gm_ref, qb_ref, kb_ref, vb_ref, gb_ref, kmean_ref):
    tile = pl.program_id(1)
    x = x_ref[0]
    h = _rms(x, ng_ref[...] * (1.0 + mod_ref[0, 1:2, :])) + mod_ref[0, 0:1, :]
    hb = h.astype(BF16)

    def proj(lo, hi):
        return jnp.dot(hb, win_ref[:, lo:hi], preferred_element_type=F32)

    rc = rc_ref[...]
    rs = rs_ref[...]

    cqn = _rms(proj(0, 256), qng_ref[...]).astype(BF16)
    q = jnp.dot(cqn, wuq_ref[...], preferred_element_type=F32)
    qg = qg_ref[...]
    qpad = qpad_ref[...]
    kpad = kpad_ref[...]
    for hh in range(H_MLA):
        sl = slice(LANES * hh, LANES * (hh + 1))
        qm_ref[0, :, sl] = (_mla_head(q[:, sl], qg, rc, rs) + qpad).astype(BF16)

    ckvn = _rms(proj(256, 384), kvng_ref[...]).astype(BF16)
    kn = jnp.dot(ckvn, wuk_ref[...], preferred_element_type=F32)
    kr = proj(384, 512)
    kg = kg_ref[...]
    for hh in range(H_MLA):
        sl = slice(LANES * hh, LANES * (hh + 1))
        km_ref[0, :, sl] = (_mla_head(kn[:, sl] + kr, kg, rc, rs) + kpad).astype(BF16)
    vm_ref[0] = jnp.dot(ckvn, wuv_ref[...], preferred_element_type=F32).astype(BF16)
    gm_ref[0] = proj(512, 1024).astype(BF16)

    lane = lax.broadcasted_iota(jnp.int32, (TM, LANES), 1)
    lo_half = lane < HEAD_DIM
    blk_id = tile * (TM // MOBA_BLOCK) + jnp.right_shift(
        lax.broadcasted_iota(jnp.int32, (TM, LANES), 0), MOBA_BLOCK.bit_length() - 1)
    onehot = (lane == PEN_LANE + blk_id).astype(F32)
    qbz = proj(1024, 1536)
    kbz = proj(1536, 2048)
    bqg = bqg_ref[...]
    bkg = bkg_ref[...]
    for p in range(N_PAIRS):
        sl = slice(LANES * p, LANES * (p + 1))
        qb_ref[0, :, sl] = _moba_pair_norm(qbz[:, sl], bqg, lo_half)
        kn_p = _moba_pair_norm(kbz[:, sl], bkg, lo_half)
        kb_ref[0, :, LANES * 2 * p:LANES * (2 * p + 1)] = jnp.where(lo_half, kn_p, onehot).astype(BF16)
        kb_ref[0, :, LANES * (2 * p + 1):LANES * (2 * p + 2)] = jnp.where(
            lo_half, pltpu.roll(kn_p, HEAD_DIM, 1), onehot).astype(BF16)
        for blk in range(TM // MOBA_BLOCK):
            rows = kn_p[MOBA_BLOCK * blk:MOBA_BLOCK * (blk + 1)]
            kmean_ref[0, 0, blk:blk + 1, sl] = jnp.sum(rows, axis=0, keepdims=True) * (1.0 / MOBA_BLOCK)
    vb_ref[0] = proj(2048, 2560).astype(BF16)
    gb_ref[0] = proj(2560, 3072).astype(BF16)


def _proj_in_call(x, mod3, ng, win, qng, wuq, kvng, wuk, wuv, qg, kg, bqg, bkg, rope_c, rope_s,
                  qpad, kpad):
    b, s, d = x.shape
    nt = s // TM
    const2 = lambda bi, i: (0, 0)
    tok = lambda w: pl.BlockSpec((1, TM, w), lambda bi, i: (bi, i, 0))
    full = lambda a: pl.BlockSpec(a.shape, const2)
    rope_spec = pl.BlockSpec((TM, LANES), lambda bi, i: (bi * nt + i, 0))
    out_shape = (
        jax.ShapeDtypeStruct((b, s, H_MLA * LANES), BF16),
        jax.ShapeDtypeStruct((b, s, H_MLA * LANES), BF16),
        jax.ShapeDtypeStruct((b, s, MLA_WIDTH), BF16),
        jax.ShapeDtypeStruct((b, s, MLA_WIDTH), BF16),
        jax.ShapeDtypeStruct((b, s, MOBA_WIDTH), F32),
        jax.ShapeDtypeStruct((b, s, H_MOBA * LANES), BF16),
        jax.ShapeDtypeStruct((b, s, MOBA_WIDTH), BF16),
        jax.ShapeDtypeStruct((b, s, MOBA_WIDTH), BF16),
        jax.ShapeDtypeStruct((b, nt, TM // MOBA_BLOCK, MOBA_WIDTH), F32),
    )
    out_specs = (tok(H_MLA * LANES), tok(H_MLA * LANES), tok(MLA_WIDTH), tok(MLA_WIDTH),
                 tok(MOBA_WIDTH), tok(H_MOBA * LANES), tok(MOBA_WIDTH), tok(MOBA_WIDTH),
                 pl.BlockSpec((1, 1, TM // MOBA_BLOCK, MOBA_WIDTH), lambda bi, i: (bi, i, 0, 0)))
    return pl.pallas_call(
        _proj_in_kernel,
        out_shape=out_shape,
        grid=(b, nt),
        in_specs=[tok(d), pl.BlockSpec((1, 3, d), lambda bi, i: (bi, 0, 0)), full(ng), full(win),
                  full(qng), full(wuq), full(kvng), full(wuk), full(wuv),
                  full(qg), full(kg), full(bqg), full(bkg), rope_spec, rope_spec,
                  full(qpad), full(kpad)],
        out_specs=out_specs,
        compiler_params=pltpu.CompilerParams(dimension_semantics=("arbitrary", "arbitrary"),
                                             vmem_limit_bytes=VMEM_LIMIT),
        name="proj_in",
    )(x, mod3, ng, win, qng, wuq, kvng, wuk, wuv, qg, kg, bqg, bkg, rope_c, rope_s, qpad, kpad)


def _moba_gate_kernel(neg_shift_ref, q_ref, kmean_ref, qx_ref):
    nb = kmean_ref.shape[1]
    lo_k = lax.broadcasted_iota(jnp.int32, (nb, LANES), 1) < HEAD_DIM
    lo_q = lax.broadcasted_iota(jnp.int32, (TG, LANES), 1) < HEAD_DIM
    blk = lax.broadcasted_iota(jnp.int32, (nb, TG), 0)
    blkf = blk.astype(F32)
    own = pl.program_id(1) * (TG // MOBA_BLOCK) + jnp.right_shift(
        lax.broadcasted_iota(jnp.int32, (nb, TG), 1), MOBA_BLOCK.bit_length() - 1)
    neg_shift = neg_shift_ref[0]
    r = lax.broadcasted_iota(jnp.int32, (2 * nb, 2 * LANES), 0)
    to_lane = (lax.broadcasted_iota(jnp.int32, (2 * nb, 2 * LANES), 1)
               == jnp.where(r < nb, PEN_LANE + r, LANES + PEN_LANE + r - nb)).astype(BF16)
    for p in range(N_PAIRS):
        sl = slice(LANES * p, LANES * (p + 1))
        km = kmean_ref[0, :, sl]
        q = q_ref[0, :, sl]
        km2 = jnp.concatenate([jnp.where(lo_k, km, 0.0), jnp.where(lo_k, 0.0, km)], axis=0)
        g2 = lax.dot_general(km2, q, _NT, precision=lax.Precision.HIGHEST,
                             preferred_element_type=F32)
        pens = []
        for hh in range(2):
            rem = jnp.where(blk < own, g2[nb * hh:nb * (hh + 1)], -jnp.inf)
            on = blk == own
            for k in range(MOBA_TOPK):
                mx = jnp.max(rem, axis=0, keepdims=True)
                idx = jnp.min(jnp.where(rem == mx, blkf, float(nb)), axis=0, keepdims=True)
                hit = blkf == idx
                on = on | (hit & (own > k))
                rem = jnp.where(hit, -jnp.inf, rem)
            pens.append(jnp.where(on, neg_shift, PEN_OFF).astype(BF16))
        pen_lanes = lax.dot_general(jnp.concatenate(pens, axis=0), to_lane, (((0,), (0,)), ((), ())),
                                    preferred_element_type=F32)
        qs = q * MOBA_QSCALE
        qs2 = jnp.concatenate([jnp.where(lo_q, qs, 0.0),
                               pltpu.roll(jnp.where(lo_q, 0.0, qs), HEAD_DIM, 1)], axis=1)
        qx_ref[0, :, 2 * LANES * p:2 * LANES * (p + 1)] = (qs2 + pen_lanes).astype(BF16)


def _moba_gate_call(neg_shift, qb, kmean):
    b, s, _ = qb.shape
    nb = kmean.shape[1]
    return pl.pallas_call(
        _moba_gate_kernel,
        out_shape=jax.ShapeDtypeStruct((b, s, H_MOBA * LANES), BF16),
        grid=(b, s // TG),
        in_specs=[pl.BlockSpec(memory_space=pltpu.SMEM),
                  pl.BlockSpec((1, TG, MOBA_WIDTH), lambda bi, i: (bi, i, 0)),
                  pl.BlockSpec((1, nb, MOBA_WIDTH), lambda bi, i: (bi, 0, 0))],
        out_specs=pl.BlockSpec((1, TG, H_MOBA * LANES), lambda bi, i: (bi, i, 0)),
        compiler_params=pltpu.CompilerParams(dimension_semantics=("arbitrary", "arbitrary"),
                                             vmem_limit_bytes=VMEM_LIMIT),
        name="moba_gate",
    )(neg_shift, qb, kmean)


def _flash_update(hh, s, v_ext, m_sc, acc_sc, first, keep, fixed_shift):
    if fixed_shift:
        acc_sc[hh] = acc_sc[hh] * keep + jnp.dot(jnp.exp2(s).astype(BF16), v_ext,
                                                 preferred_element_type=F32)
        return
    m_prev = jnp.where(first == 1, NEG_BIG, m_sc[hh])
    m_new = jnp.maximum(m_prev, jnp.max(s, axis=-1, keepdims=True))
    alpha = jnp.exp2(m_prev - m_new)
    p = jnp.concatenate([jnp.exp2(s[:, LANES * c:LANES * (c + 1)] - m_new)
                         for c in range(s.shape[1] // LANES)], axis=1).astype(BF16)
    pv = jnp.dot(p, v_ext, preferred_element_type=F32)
    acc_sc[hh] = jnp.concatenate([alpha, alpha], axis=1) * acc_sc[hh] + pv
    m_sc[hh] = m_new


def _with_ones(v):
    return jnp.concatenate([v, jnp.ones(v.shape, v.dtype)], axis=1)


def _flash_init(m_sc, acc_sc):
    m_sc[...] = jnp.full(m_sc.shape, NEG_BIG, F32)
    acc_sc[...] = jnp.zeros(acc_sc.shape, F32)


def _flash_finish(o_ref, rows, acc_sc):
    lo_half = lax.broadcasted_iota(jnp.int32, (acc_sc.shape[1], LANES), 1) < HEAD_DIM
    o = jnp.where(lo_half, acc_sc[0, :, :LANES] / acc_sc[0, :, LANES:],
                  acc_sc[1, :, :LANES] / acc_sc[1, :, LANES:])
    o_ref[0, rows, :] = o.astype(o_ref.dtype)


def _causal_steps(n_tiles):
    pairs = [(qi, u) for qi in range(n_tiles) for u in range(qi + 1)]
    return (jnp.asarray([p[0] for p in pairs], jnp.int32), jnp.asarray([p[1] for p in pairs], jnp.int32),
            jnp.asarray([float(p[1] > 0) for p in pairs], F32))


def _pipelined(n_steps, produce, consume, bufs):
    produce(0, bufs[0])
    iters = (n_steps - 1) // STEPS_PER_ITER

    def body(it, carry):
        for j in range(STEPS_PER_ITER):
            n = STEPS_PER_ITER * it + j
            produce(n + 1, bufs[(j + 1) % 2])
            consume(n, bufs[j % 2])
        return carry

    lax.fori_loop(0, iters, body, 0)
    for n in range(STEPS_PER_ITER * iters, n_steps):
        if n + 1 < n_steps:
            produce(n + 1, bufs[(n + 1) % 2])
        consume(n, bufs[n % 2])


def _mla_kernel(qtab_ref, utab_ref, keep_ref, q_ref, k_ref, v_ref, o_ref,
                m_sc, acc_sc, sa_sc, sb_sc, mask_sc, *, fixed_shift, n_steps):
    t = TQ_MLA
    _flash_init(m_sc, acc_sc)
    row = lax.broadcasted_iota(jnp.int32, (t, t), 0)
    col = lax.broadcasted_iota(jnp.int32, (t, t), 1)
    mask_sc[0] = jnp.zeros((t, t), F32)
    mask_sc[1] = jnp.where(col <= row, 0.0, NEG_BIG)

    def produce(n, buf):
        qi, u = qtab_ref[n], utab_ref[n]
        qrows = pl.ds(pl.multiple_of(qi * t, t), t)
        krows = pl.ds(pl.multiple_of(u * t, t), t)
        diag = (u == qi).astype(jnp.int32)
        for hh in range(2):
            sl = slice(LANES * hh, LANES * (hh + 1))
            buf[hh] = lax.dot_general(q_ref[0, qrows, sl], k_ref[0, krows, sl], _NT,
                                      preferred_element_type=F32) + mask_sc[diag]

    def consume(n, buf):
        qi, u = qtab_ref[n], utab_ref[n]
        v = _with_ones(v_ref[0, pl.ds(pl.multiple_of(u * t, t), t), :])
        first = (u == 0).astype(jnp.int32)
        for hh in range(2):
            _flash_update(hh, buf[hh], v, m_sc, acc_sc, first, keep_ref[n], fixed_shift)
        _flash_finish(o_ref, pl.ds(pl.multiple_of(qi * t, t), t), acc_sc)

    _pipelined(n_steps, produce, consume, (sa_sc, sb_sc))


def _mla_call(qm, km, vm, fixed_shift):
    b, s, _ = qm.shape
    t = TQ_MLA
    qtab, utab, keep = _causal_steps(s // t)
    seq = lambda w: pl.BlockSpec((1, s, w), lambda p, bi, *_: (bi, 0, p))
    return pl.pallas_call(
        functools.partial(_mla_kernel, fixed_shift=fixed_shift, n_steps=int(qtab.shape[0])),
        out_shape=jax.ShapeDtypeStruct((b, s, MLA_WIDTH), BF16),
        grid_spec=pltpu.PrefetchScalarGridSpec(
            num_scalar_prefetch=2,
            grid=(N_PAIRS, b),
            in_specs=[pl.BlockSpec(memory_space=pltpu.SMEM),
                      seq(2 * LANES), seq(2 * LANES), seq(LANES)],
            out_specs=seq(LANES),
            scratch_shapes=[pltpu.VMEM((2, t, LANES), F32), pltpu.VMEM((2, t, 2 * LANES), F32),
                            pltpu.VMEM((2, t, t), F32), pltpu.VMEM((2, t, t), F32),
                            pltpu.VMEM((2, t, t), F32)],
        ),
        compiler_params=pltpu.CompilerParams(dimension_semantics=("arbitrary", "arbitrary"),
                                             vmem_limit_bytes=VMEM_LIMIT),
        name="mla_attention",
    )(qtab, utab, keep, qm, km, vm)


def _moba_kernel(qtab_ref, utab_ref, keep_ref, qx_ref, k_ref, v_ref, bt_ref, o_ref,
                 m_sc, acc_sc, sa_sc, sb_sc, *, fixed_shift, n_steps):
    t = MOBA_BLOCK
    w = MOBA_TILE
    nsub = w // t
    _flash_init(m_sc, acc_sc)

    def produce(n, buf):
        qi, u = qtab_ref[n], utab_ref[n]
        qrows = pl.ds(pl.multiple_of(qi * w, w), w)
        krows = pl.ds(pl.multiple_of(u * w, w), w)
        for hh in range(2):
            sl = slice(LANES * hh, LANES * (hh + 1))
            s = lax.dot_general(qx_ref[0, qrows, sl], k_ref[0, krows, sl], _NT,
                                preferred_element_type=F32)
            for qs in range(nsub):
                for ks in range(nsub):
                    d = jnp.maximum(nsub * (qi - u) + (qs - ks), 0)
                    rows, cols = slice(t * qs, t * (qs + 1)), slice(t * ks, t * (ks + 1))
                    buf[hh, rows, cols] = s[rows, cols] + bt_ref[0, hh, d]

    def consume(n, buf):
        qi, u = qtab_ref[n], utab_ref[n]
        v = _with_ones(v_ref[0, pl.ds(pl.multiple_of(u * w, w), w), :])
        first = (u == 0).astype(jnp.int32)
        for hh in range(2):
            _flash_update(hh, buf[hh], v, m_sc, acc_sc, first, keep_ref[n], fixed_shift)
        _flash_finish(o_ref, pl.ds(pl.multiple_of(qi * w, w), w), acc_sc)

    _pipelined(n_steps, produce, consume, (sa_sc, sb_sc))


def _moba_call(qx, kb, vb, bias_tiles, fixed_shift):
    b, s, _ = vb.shape
    t = MOBA_TILE
    nb = s // MOBA_BLOCK
    qtab, utab, keep = _causal_steps(s // t)
    seq = lambda w: pl.BlockSpec((1, s, w), lambda p, bi, *_: (bi, 0, p))
    return pl.pallas_call(
        functools.partial(_moba_kernel, fixed_shift=fixed_shift, n_steps=int(qtab.shape[0])),
        out_shape=jax.ShapeDtypeStruct((b, s, MOBA_WIDTH), BF16),
        grid_spec=pltpu.PrefetchScalarGridSpec(
            num_scalar_prefetch=2,
            grid=(N_PAIRS, b),
            in_specs=[pl.BlockSpec(memory_space=pltpu.SMEM),
                      seq(2 * LANES), seq(2 * LANES), seq(LANES),
                      pl.BlockSpec((1, 2, nb, MOBA_BLOCK, MOBA_BLOCK), lambda p, bi, *_: (p, 0, 0, 0, 0),
                                   pipeline_mode=pl.Buffered(1))],
            out_specs=seq(LANES),
            scratch_shapes=[pltpu.VMEM((2, t, LANES), F32), pltpu.VMEM((2, t, 2 * LANES), F32),
                            pltpu.VMEM((2, t, t), F32), pltpu.VMEM((2, t, t), F32)],
        ),
        compiler_params=pltpu.CompilerParams(dimension_semantics=("arbitrary", "arbitrary"),
                                             vmem_limit_bytes=VMEM_LIMIT),
        name="moba_attention",
    )(qtab, utab, keep, qx, kb, vb, bias_tiles)


def _proj_out_kernel(om_ref, gm_ref, ob_ref, gb_ref, w_ref, x_ref, mod_ref, o_ref):
    am = (_silu(gm_ref[0].astype(F32)) * om_ref[0].astype(F32)).astype(BF16)
    ab = (_silu(gb_ref[0].astype(F32)) * ob_ref[0].astype(F32)).astype(BF16)
    y = (jnp.dot(am, w_ref[0:MLA_WIDTH, :], preferred_element_type=F32)
         + jnp.dot(ab, w_ref[MLA_WIDTH:, :], preferred_element_type=F32))
    o_ref[0] = x_ref[0] + mod_ref[0, 2:3, :] * y


def _proj_out_call(o_mla, g_mla, o_moba, g_moba, w_out, x, mod3):
    b, s, d = x.shape
    tok = lambda w: pl.BlockSpec((1, TM_OUT, w), lambda bi, i: (bi, i, 0))
    return pl.pallas_call(
        _proj_out_kernel,
        out_shape=jax.ShapeDtypeStruct((b, s, d), F32),
        grid=(b, s // TM_OUT),
        in_specs=[tok(MLA_WIDTH), tok(MLA_WIDTH), tok(MOBA_WIDTH), tok(MOBA_WIDTH),
                  pl.BlockSpec(w_out.shape, lambda bi, i: (0, 0)),
                  tok(d), pl.BlockSpec((1, 3, d), lambda bi, i: (bi, 0, 0))],
        out_specs=tok(d),
        compiler_params=pltpu.CompilerParams(dimension_semantics=("arbitrary", "arbitrary"),
                                             vmem_limit_bytes=VMEM_LIMIT),
        name="proj_out",
    )(o_mla, g_mla, o_moba, g_moba, w_out, x, mod3)


def _to_head_lanes(a):
    a = jnp.concatenate([a, jnp.zeros(a.shape[:-1] + (1,), a.dtype)], axis=-1)
    return a[..., _LANE_SRC]


def _layer_weights(w_in, w_uq, w_ukv, q_g, k_g, bq_g, bk_g):
    o0 = Q_LORA
    o1 = o0 + KV_LORA
    o2 = o1 + ROPE_DIM
    k_rope = w_in[:, o1:o2]
    kr128 = jnp.zeros((D_MODEL, LANES), w_in.dtype)
    kr128 = kr128.at[:, 0:HALF_ROPE].set(k_rope[:, :HALF_ROPE])
    kr128 = kr128.at[:, LANES // 2:LANES // 2 + HALF_ROPE].set(k_rope[:, HALF_ROPE:])
    win = jnp.concatenate([w_in[:, :o1], kr128, w_in[:, o2:]], axis=1).astype(BF16)
    wuq = _to_head_lanes(w_uq.reshape(Q_LORA, H_MLA, QK_DIM)).reshape(Q_LORA, H_MLA * LANES)
    ukv = w_ukv.reshape(KV_LORA, H_MLA, NOPE_DIM + HEAD_DIM)
    k_nope = jnp.concatenate([ukv[..., :NOPE_DIM], jnp.zeros((KV_LORA, H_MLA, ROPE_DIM), ukv.dtype)], -1)
    wuk = _to_head_lanes(k_nope).reshape(KV_LORA, H_MLA * LANES)
    wuv = ukv[..., NOPE_DIM:].reshape(KV_LORA, MLA_WIDTH)
    pair = lambda g: jnp.concatenate([g, g]).reshape(1, LANES)
    return (win, wuq.astype(BF16), wuk.astype(BF16), wuv.astype(BF16),
            _to_head_lanes(q_g * MLA_QSCALE).reshape(1, LANES), _to_head_lanes(k_g).reshape(1, LANES),
            pair(bq_g), pair(bk_g))


def kernel(x, c, positions, norm_g, w_ada, b_ada, w_in, mla_q_norm_g, mla_w_uq, mla_kv_norm_g,
           mla_w_ukv, mla_q_g, mla_k_g, moba_q_g, moba_k_g, w_out, rel_bias):
    b, s, d = x.shape
    depth = w_in.shape[0]
    nb = s // MOBA_BLOCK
    assert d == D_MODEL and s % TM == 0 and s % TQ_MLA == 0 and TM % MOBA_BLOCK == 0 and b <= 8
    assert s % TG == 0 and TG % MOBA_BLOCK == 0 and s % MOBA_TILE == 0 and nb <= LANES // 4
    _check_far_tiles(s)

    c8 = jnp.pad(c, ((0, 8 - b), (0, 0)))
    mod = _mod_call(c8, w_ada, b_ada.reshape(depth, 1, 3 * d))[:, :b].reshape(depth, b, 3, d)

    inv_freq = ROPE_THETA ** (-jnp.arange(0, HALF_ROPE, dtype=F32) / HALF_ROPE)
    freq = jnp.zeros((LANES,), F32).at[0:HALF_ROPE].set(inv_freq)
    freq = freq.at[LANES // 2:LANES // 2 + HALF_ROPE].set(inv_freq).reshape(1, LANES)
    sign = jnp.zeros((LANES,), F32).at[0:HALF_ROPE].set(-1.0)
    sign = sign.at[LANES // 2:LANES // 2 + HALF_ROPE].set(1.0).reshape(1, LANES)
    rope_c, rope_s = _rope_call(positions.reshape(b * s, 1), freq, sign)

    bkt = _rel_bucket(jnp.arange(s, dtype=jnp.int32))
    thr = jnp.sum(bkt[None, :] < jnp.arange(N_BUCKETS, dtype=jnp.int32)[:, None], axis=1).astype(jnp.int32)
    ttab, vtab = _far_tables(bkt, rel_bias, s)
    bias_tiles = _bias_tile_call(thr, ttab, rel_bias.astype(F32).reshape(-1), vtab, nb)
    bias_tiles = bias_tiles.reshape(N_PAIRS, 2, nb, MOBA_BLOCK, MOBA_BLOCK)
    kpad = jnp.zeros((1, LANES), F32).at[0, SHIFT_LANE].set(1.0)
    amax = lambda a: jnp.max(jnp.abs(a.astype(F32)))

    def attention(fixed_shift, qm, km, vm, qx, kb, vb):
        return (_mla_call(qm, km, vm, fixed_shift),
                _moba_call(qx, kb, vb, bias_tiles, fixed_shift))

    for l in range(depth):
        win, wuq, wuk, wuv, qg, kg, bqg, bkg = _layer_weights(
            w_in[l], mla_w_uq[l], mla_w_ukv[l], mla_q_g[l], mla_k_g[l], moba_q_g[l], moba_k_g[l])
        bound_mla = QK_DIM * amax(mla_q_g[l]) * amax(mla_k_g[l]) * MLA_QSCALE * SHIFT_SLACK
        bound_moba = (HEAD_DIM * amax(moba_q_g[l]) * amax(moba_k_g[l]) * MOBA_QSCALE
                      + amax(rel_bias) * LOG2E) * SHIFT_SLACK
        fixed_ok = (bound_mla <= MAX_SHIFT) & (bound_moba <= MAX_SHIFT)
        qpad = jnp.zeros((1, LANES), F32).at[0, SHIFT_LANE].set(jnp.where(fixed_ok, -bound_mla, 0.0))
        qm, km, vm, gm, qb, kb, vb, gb, kmean = _proj_in_call(
            x, mod[l], norm_g[l].reshape(1, d), win, mla_q_norm_g[l].reshape(1, Q_LORA), wuq,
            mla_kv_norm_g[l].reshape(1, KV_LORA), wuk, wuv, qg, kg, bqg, bkg, rope_c, rope_s,
            qpad, kpad)
        qx = _moba_gate_call(jnp.where(fixed_ok, -bound_moba, 0.0).reshape(1), qb,
                             kmean.reshape(b, nb, MOBA_WIDTH))
        o_mla, o_moba = lax.cond(
            fixed_ok, functools.partial(attention, True), functools.partial(attention, False),
            qm, km, vm, qx, kb, vb)
        x = _proj_out_call(o_mla, gm, o_moba, gb, w_out[l].astype(BF16), x, mod[l])
    return x
```

```python
import functools
import math

import numpy as np
import jax
import jax.numpy as jnp
from jax import lax
from jax.experimental import pallas as pl
from jax.experimental.pallas import tpu as pltpu

F32 = jnp.float32
BF16 = jnp.bfloat16

D_MODEL = 1024
DEPTH = 2
HEAD_DIM = 64
H_MLA = 8
H_MOBA = 8
MLA_WIDTH = H_MLA * HEAD_DIM
MOBA_WIDTH = H_MOBA * HEAD_DIM
Q_LORA = 256
KV_LORA = 128
NOPE_DIM = 64
ROPE_DIM = 32
QK_DIM = NOPE_DIM + ROPE_DIM
ROPE_THETA = 10000.0
MOBA_BLOCK = 256
MOBA_TOPK = 3
N_BUCKETS = 32
REL_MAX_DIST = 4096
EPS = 1e-6

LANES = 128
HALF_ROPE = ROPE_DIM // 2
N_PAIRS = H_MLA // 2
LOG2E = math.log2(math.e)
MLA_QSCALE = QK_DIM ** -0.5 * LOG2E
MOBA_QSCALE = HEAD_DIM ** -0.5 * LOG2E
NEG_BIG = -1e30
FAR_THRESHOLDS = 2
NEAR_TILES = 3
NEVER = 1 << 20
PEN_OFF = -2.0 ** 127
SHIFT_LANE = QK_DIM
PEN_LANE = HEAD_DIM
MAX_SHIFT = 48.0
SHIFT_SLACK = 1.02

TM = 256
TM_OUT = 512
TG = 512
TQ_MLA = 512
MOBA_TILE = 2 * MOBA_BLOCK
STEPS_PER_ITER = 8
VMEM_LIMIT = 56 * 1024 * 1024

_NT = (((1,), (1,)), ((), ()))


def _head_lane_source():
    src = np.full((LANES,), QK_DIM, np.int32)
    src[0:16] = NOPE_DIM + np.arange(16)
    src[16:64] = np.arange(48)
    src[64:80] = NOPE_DIM + HALF_ROPE + np.arange(16)
    src[80:96] = 48 + np.arange(16)
    return src


_LANE_SRC = _head_lane_source()


def _bucket_np(n, dtype):
    n = np.asarray(n)
    max_exact = N_BUCKETS // 2
    nf = np.maximum(n, 1).astype(dtype)
    large = max_exact + (np.log(nf / dtype(max_exact)) / dtype(math.log(REL_MAX_DIST / max_exact))
                         * dtype(N_BUCKETS - max_exact)).astype(np.int32)
    large = np.minimum(large, N_BUCKETS - 1)
    return np.where(n < max_exact, n, large)


def _check_far_tiles(seq):
    n = np.arange(seq + 2)
    nb = seq // MOBA_BLOCK
    n_tiles = 0
    for dtype in (np.float32, np.float64):
        b = _bucket_np(n, dtype)
        chg = np.concatenate([[0], (b[1:] != b[:-1]).astype(np.int64)])
        for d in range(NEAR_TILES, nb):
            lo, hi = MOBA_BLOCK * (d - 1) + 1, MOBA_BLOCK * (d + 1) - 1
            assert chg[lo - 1:hi + 1].sum() <= FAR_THRESHOLDS, (d, dtype)
        saturated = [d for d in range(NEAR_TILES, nb)
                     if b[MOBA_BLOCK * (d - 1) - 1] == N_BUCKETS - 1]
        n_tiles = max(n_tiles, (saturated[0] if saturated else nb - 1) + 1)
    return min(n_tiles, nb)


def _rel_bucket(dist):
    n = jnp.maximum(dist, 0)
    max_exact = N_BUCKETS // 2
    nf = jnp.maximum(n, 1).astype(F32)
    large = max_exact + (jnp.log(nf / max_exact) / math.log(REL_MAX_DIST / max_exact)
                         * (N_BUCKETS - max_exact)).astype(jnp.int32)
    large = jnp.minimum(large, N_BUCKETS - 1)
    return jnp.where(n < max_exact, n, large)


def _rms(x, g):
    return x * lax.rsqrt(jnp.mean(x * x, axis=-1, keepdims=True) + EPS) * g


def _silu(x):
    return x * jax.nn.sigmoid(x)


def _mod_kernel(c_ref, w_ref, b_ref, o_ref):
    c = c_ref[...]
    o_ref[0] = jnp.dot(_silu(c), w_ref[0], precision=lax.Precision.HIGHEST,
                       preferred_element_type=F32) + b_ref[0]


def _mod_call(c8, w_ada, b_ada):
    depth, d, d3 = w_ada.shape
    nchunk = d3 // d
    return pl.pallas_call(
        _mod_kernel,
        out_shape=jax.ShapeDtypeStruct((depth, 8, d3), F32),
        grid=(depth, nchunk),
        in_specs=[pl.BlockSpec((8, d), lambda l, j: (0, 0)),
                  pl.BlockSpec((1, d, d), lambda l, j: (l, 0, j)),
                  pl.BlockSpec((1, 1, d), lambda l, j: (l, 0, j))],
        out_specs=pl.BlockSpec((1, 8, d), lambda l, j: (l, 0, j)),
        compiler_params=pltpu.CompilerParams(dimension_semantics=("arbitrary", "arbitrary"),
                                             vmem_limit_bytes=VMEM_LIMIT),
        name="adaln_mod",
    )(c8, w_ada, b_ada)


def _rope_kernel(pos_ref, freq_ref, sign_ref, c_ref, s_ref):
    ang = pos_ref[...].astype(F32) * freq_ref[...]
    c_ref[...] = jnp.cos(ang)
    s_ref[...] = jnp.sin(ang) * sign_ref[...]


def _rope_call(pos_col, freq, sign):
    n = pos_col.shape[0]
    tr = 1024
    return pl.pallas_call(
        _rope_kernel,
        out_shape=(jax.ShapeDtypeStruct((n, LANES), F32), jax.ShapeDtypeStruct((n, LANES), F32)),
        grid=(n // tr,),
        in_specs=[pl.BlockSpec((tr, 1), lambda i: (i, 0)),
                  pl.BlockSpec((1, LANES), lambda i: (0, 0)),
                  pl.BlockSpec((1, LANES), lambda i: (0, 0))],
        out_specs=(pl.BlockSpec((tr, LANES), lambda i: (i, 0)),
                   pl.BlockSpec((tr, LANES), lambda i: (i, 0))),
        compiler_params=pltpu.CompilerParams(dimension_semantics=("arbitrary",)),
        name="rope_tables",
    )(pos_col, freq, sign)


def _bias_tile_kernel(thr_ref, ttab_ref, rb_ref, vtab_ref, o_ref):
    h = pl.program_id(0)
    nb = o_ref.shape[1]
    shape = (MOBA_BLOCK, MOBA_BLOCK)
    rc = lax.broadcasted_iota(jnp.int32, shape, 0) - lax.broadcasted_iota(jnp.int32, shape, 1)

    for d in range(NEAR_TILES):
        dist = rc + d * MOBA_BLOCK
        val = jnp.full(shape, rb_ref[h], F32)
        for b in range(1, N_BUCKETS):
            val = jnp.where(dist >= thr_ref[b], rb_ref[b * H_MOBA + h], val)
        o_ref[0, d] = jnp.where(dist >= 0, val * LOG2E, NEG_BIG)

    def far(d, carry):
        base = (d * H_MOBA + h) * (FAR_THRESHOLDS + 1)
        val = jnp.full(shape, vtab_ref[base], F32)
        for k in range(FAR_THRESHOLDS):
            val = jnp.where(rc >= ttab_ref[d * FAR_THRESHOLDS + k], vtab_ref[base + k + 1], val)
        o_ref[0, d] = val
        return carry

    lax.fori_loop(NEAR_TILES, nb, far, 0)


def _bias_tile_call(thr, ttab, rb_flat, vtab, nb):
    smem = pl.BlockSpec(memory_space=pltpu.SMEM)
    return pl.pallas_call(
        _bias_tile_kernel,
        out_shape=jax.ShapeDtypeStruct((H_MOBA, nb, MOBA_BLOCK, MOBA_BLOCK), F32),
        grid_spec=pltpu.PrefetchScalarGridSpec(
            num_scalar_prefetch=2,
            grid=(H_MOBA,),
            in_specs=[smem, smem],
            out_specs=pl.BlockSpec((1, nb, MOBA_BLOCK, MOBA_BLOCK), lambda h, *_: (h, 0, 0, 0)),
        ),
        compiler_params=pltpu.CompilerParams(dimension_semantics=("arbitrary",),
                                             vmem_limit_bytes=VMEM_LIMIT),
        name="moba_bias_tiles",
    )(thr, ttab, rb_flat, vtab)


def _far_tables(thr, rel_bias, n_tiles):
    edge = thr[1:]
    bucket = jnp.arange(1, N_BUCKETS, dtype=jnp.int32)
    keep = jnp.concatenate([edge[1:] > edge[:-1], jnp.ones((1,), bool)])
    d = jnp.arange(n_tiles, dtype=jnp.int32)[:, None]
    lo = jnp.maximum(MOBA_BLOCK * (d - 1) + 1, 0)
    hi = MOBA_BLOCK * (d + 1) - 1
    inside = keep[None, :] & (edge[None, :] > lo) & (edge[None, :] <= hi)
    cum = jnp.cumsum(inside.astype(jnp.int32), axis=1)
    idx = [jnp.sum((edge[None, :] <= lo).astype(jnp.int32), axis=1)]
    tt = []
    for k in range(1, FAR_THRESHOLDS + 1):
        sel = inside & (cum == k)
        tt.append(jnp.min(jnp.where(sel, edge[None, :], NEVER), axis=1) - MOBA_BLOCK * d[:, 0])
        idx.append(jnp.min(jnp.where(sel, bucket[None, :], N_BUCKETS - 1), axis=1))
    ttab = jnp.stack(tt, axis=1).reshape(-1).astype(jnp.int32)
    vals = rel_bias.astype(F32)[jnp.stack(idx, axis=1)] * LOG2E
    vtab = vals.transpose(0, 2, 1).reshape(-1)
    return ttab, vtab


def _mla_head(xh, g, rc, rs):
    ms = jnp.sum(xh * xh, axis=-1, keepdims=True) * (1.0 / QK_DIM)
    xn = xh * lax.rsqrt(ms + EPS) * g
    return xn * rc + pltpu.roll(xn, LANES // 2, 1) * rs


def _moba_pair_norm(x, g, lo_half):
    x2 = x * x
    s_lo = jnp.sum(jnp.where(lo_half, x2, 0.0), axis=-1, keepdims=True)
    s_hi = jnp.sum(jnp.where(lo_half, 0.0, x2), axis=-1, keepdims=True)
    ms = jnp.where(lo_half, s_lo, s_hi) * (1.0 / HEAD_DIM)
    return x * lax.rsqrt(ms + EPS) * g


def _proj_in_kernel(x_ref, mod_ref, ng_ref, win_ref, qng_ref, wuq_ref, kvng_ref, wuk_ref, wuv_ref,
                    qg_ref, kg_ref, bqg_ref, bkg_ref, rc_ref, rs_ref, qpad_ref, kpad_ref,
                    qm_ref, km_ref, vm_ref, gm_ref, qb_ref, kb_ref, vb_ref, gb_ref, kmean_ref):
    tile = pl.program_id(1)
    x = x_ref[0]
    h = _rms(x, ng_ref[...] * (1.0 + mod_ref[0, 1:2, :])) + mod_ref[0, 0:1, :]
    hb = h.astype(BF16)

    def proj(lo, hi):
        return jnp.dot(hb, win_ref[:, lo:hi], preferred_element_type=F32)

    rc = rc_ref[...]
    rs = rs_ref[...]

    cqn = _rms(proj(0, 256), qng_ref[...]).astype(BF16)
    q = jnp.dot(cqn, wuq_ref[...], preferred_element_type=F32)
    qg = qg_ref[...]
    qpad = qpad_ref[...]
    kpad = kpad_ref[...]
    for hh in range(H_MLA):
        sl = slice(LANES * hh, LANES * (hh + 1))
        qm_ref[0, :, sl] = (_mla_head(q[:, sl], qg, rc, rs) + qpad).astype(BF16)

    ckvn = _rms(proj(256, 384), kvng_ref[...]).astype(BF16)
    kn = jnp.dot(ckvn, wuk_ref[...], preferred_element_type=F32)
    kr = proj(384, 512)
    kg = kg_ref[...]
    for hh in range(H_MLA):
        sl = slice(LANES * hh, LANES * (hh + 1))
        km_ref[0, :, sl] = (_mla_head(kn[:, sl] + kr, kg, rc, rs) + kpad).astype(BF16)
    vm_ref[0] = jnp.dot(ckvn, wuv_ref[...], preferred_element_type=F32).astype(BF16)
    gm_ref[0] = proj(512, 1024).astype(BF16)

    lane = lax.broadcasted_iota(jnp.int32, (TM, LANES), 1)
    lo_half = lane < HEAD_DIM
    blk_id = tile * (TM // MOBA_BLOCK) + jnp.right_shift(
        lax.broadcasted_iota(jnp.int32, (TM, LANES), 0), MOBA_BLOCK.bit_length() - 1)
    onehot = (lane == PEN_LANE + blk_id).astype(F32)
    qbz = proj(1024, 1536)
    kbz = proj(1536, 2048)
    bqg = bqg_ref[...]
    bkg = bkg_ref[...]
    for p in range(N_PAIRS):
        sl = slice(LANES * p, LANES * (p + 1))
        qb_ref[0, :, sl] = _moba_pair_norm(qbz[:, sl], bqg, lo_half)
        kn_p = _moba_pair_norm(kbz[:, sl], bkg, lo_half)
        kb_ref[0, :, LANES * 2 * p:LANES * (2 * p + 1)] = jnp.where(lo_half, kn_p, onehot).astype(BF16)
        kb_ref[0, :, LANES * (2 * p + 1):LANES * (2 * p + 2)] = jnp.where(
            lo_half, pltpu.roll(kn_p, HEAD_DIM, 1), onehot).astype(BF16)
        for blk in range(TM // MOBA_BLOCK):
            rows = kn_p[MOBA_BLOCK * blk:MOBA_BLOCK * (blk + 1)]
            kmean_ref[0, 0, blk:blk + 1, sl] = jnp.sum(rows, axis=0, keepdims=True) * (1.0 / MOBA_BLOCK)
    vb_ref[0] = proj(2048, 2560).astype(BF16)
    gb_ref[0] = proj(2560, 3072).astype(BF16)


def _proj_in_call(x, mod3, ng, win, qng, wuq, kvng, wuk, wuv, qg, kg, bqg, bkg, rope_c, rope_s,
                  qpad, kpad):
    b, s, d = x.shape
    nt = s // TM
    const2 = lambda bi, i: (0, 0)
    tok = lambda w: pl.BlockSpec((1, TM, w), lambda bi, i: (bi, i, 0))
    full = lambda a: pl.BlockSpec(a.shape, const2)
    rope_spec = pl.BlockSpec((TM, LANES), lambda bi, i: (bi * nt + i, 0))
    out_shape = (
        jax.ShapeDtypeStruct((b, s, H_MLA * LANES), BF16),
        jax.ShapeDtypeStruct((b, s, H_MLA * LANES), BF16),
        jax.ShapeDtypeStruct((b, s, MLA_WIDTH), BF16),
        jax.ShapeDtypeStruct((b, s, MLA_WIDTH), BF16),
        jax.ShapeDtypeStruct((b, s, MOBA_WIDTH), F32),
        jax.ShapeDtypeStruct((b, s, H_MOBA * LANES), BF16),
        jax.ShapeDtypeStruct((b, s, MOBA_WIDTH), BF16),
        jax.ShapeDtypeStruct((b, s, MOBA_WIDTH), BF16),
        jax.ShapeDtypeStruct((b, nt, TM // MOBA_BLOCK, MOBA_WIDTH), F32),
    )
    out_specs = (tok(H_MLA * LANES), tok(H_MLA * LANES), tok(MLA_WIDTH), tok(MLA_WIDTH),
                 tok(MOBA_WIDTH), tok(H_MOBA * LANES), tok(MOBA_WIDTH), tok(MOBA_WIDTH),
                 pl.BlockSpec((1, 1, TM // MOBA_BLOCK, MOBA_WIDTH), lambda bi, i: (bi, i, 0, 0)))
    return pl.pallas_call(
        _proj_in_kernel,
        out_shape=out_shape,
        grid=(b, nt),
        in_specs=[tok(d), pl.BlockSpec((1, 3, d), lambda bi, i: (bi, 0, 0)), full(ng), full(win),
                  full(qng), full(wuq), full(kvng), full(wuk), full(wuv),
                  full(qg), full(kg), full(bqg), full(bkg), rope_spec, rope_spec,
                  full(qpad), full(kpad)],
        out_specs=out_specs,
        compiler_params=pltpu.CompilerParams(dimension_semantics=("arbitrary", "arbitrary"),
                                             vmem_limit_bytes=VMEM_LIMIT),
        name="proj_in",
    )(x, mod3, ng, win, qng, wuq, kvng, wuk, wuv, qg, kg, bqg, bkg, rope_c, rope_s, qpad, kpad)


def _moba_gate_kernel(neg_shift_ref, q_ref, kmean_ref, qx_ref):
    nb = kmean_ref.shape[1]
    lo_k = lax.broadcasted_iota(jnp.int32, (nb, LANES), 1) < HEAD_DIM
    lo_q = lax.broadcasted_iota(jnp.int32, (TG, LANES), 1) < HEAD_DIM
    blk = lax.broadcasted_iota(jnp.int32, (nb, TG), 0)
    blkf = blk.astype(F32)
    own = pl.program_id(1) * (TG // MOBA_BLOCK) + jnp.right_shift(
        lax.broadcasted_iota(jnp.int32, (nb, TG), 1), MOBA_BLOCK.bit_length() - 1)
    neg_shift = neg_shift_ref[0]
    r = lax.broadcasted_iota(jnp.int32, (2 * nb, 2 * LANES), 0)
    to_lane = (lax.broadcasted_iota(jnp.int32, (2 * nb, 2 * LANES), 1)
               == jnp.where(r < nb, PEN_LANE + r, LANES + PEN_LANE + r - nb)).astype(BF16)
    for p in range(N_PAIRS):
        sl = slice(LANES * p, LANES * (p + 1))
        km = kmean_ref[0, :, sl]
        q = q_ref[0, :, sl]
        km2 = jnp.concatenate([jnp.where(lo_k, km, 0.0), jnp.where(lo_k, 0.0, km)], axis=0)
        g2 = lax.dot_general(km2, q, _NT, precision=lax.Precision.HIGHEST,
                             preferred_element_type=F32)
        pens = []
        for hh in range(2):
            past = blk < own
            rem = jnp.where(past, g2[nb * hh:nb * (hh + 1)], -jnp.inf)
            for k in range(MOBA_TOPK):
                mx = jnp.max(rem, axis=0, keepdims=True)
                idx = jnp.min(jnp.where(rem == mx, blkf, float(nb)), axis=0, keepdims=True)
                rem = jnp.where(blkf == idx, -jnp.inf, rem)
            on = (blk == own) | (past & (rem == -jnp.inf))
            pens.append(jnp.where(on, neg_shift, PEN_OFF).astype(BF16))
        pen_lanes = lax.dot_general(jnp.concatenate(pens, axis=0), to_lane, (((0,), (0,)), ((), ())),
                                    preferred_element_type=F32)
        qs = q * MOBA_QSCALE
        qs2 = jnp.concatenate([jnp.where(lo_q, qs, 0.0),
                               pltpu.roll(jnp.where(lo_q, 0.0, qs), HEAD_DIM, 1)], axis=1)
        qx_ref[0, :, 2 * LANES * p:2 * LANES * (p + 1)] = (qs2 + pen_lanes).astype(BF16)


def _moba_gate_call(neg_shift, qb, kmean):
    b, s, _ = qb.shape
    nb = kmean.shape[1]
    return pl.pallas_call(
        _moba_gate_kernel,
        out_shape=jax.ShapeDtypeStruct((b, s, H_MOBA * LANES), BF16),
        grid=(b, s // TG),
        in_specs=[pl.BlockSpec(memory_space=pltpu.SMEM),
                  pl.BlockSpec((1, TG, MOBA_WIDTH), lambda bi, i: (bi, i, 0)),
                  pl.BlockSpec((1, nb, MOBA_WIDTH), lambda bi, i: (bi, 0, 0))],
        out_specs=pl.BlockSpec((1, TG, H_MOBA * LANES), lambda bi, i: (bi, i, 0)),
        compiler_params=pltpu.CompilerParams(dimension_semantics=("arbitrary", "arbitrary"),
                                             vmem_limit_bytes=VMEM_LIMIT),
        name="moba_gate",
    )(neg_shift, qb, kmean)


def _flash_update(hh, s, v_ext, m_sc, acc_sc, first, keep, fixed_shift):
    if fixed_shift:
        acc_sc[hh] = acc_sc[hh] * keep + jnp.dot(jnp.exp2(s).astype(BF16), v_ext,
                                                 preferred_element_type=F32)
        return
    m_prev = jnp.where(first == 1, NEG_BIG, m_sc[hh])
    m_new = jnp.maximum(m_prev, jnp.max(s, axis=-1, keepdims=True))
    alpha = jnp.exp2(m_prev - m_new)
    p = jnp.concatenate([jnp.exp2(s[:, LANES * c:LANES * (c + 1)] - m_new)
                         for c in range(s.shape[1] // LANES)], axis=1).astype(BF16)
    pv = jnp.dot(p, v_ext, preferred_element_type=F32)
    acc_sc[hh] = jnp.concatenate([alpha, alpha], axis=1) * acc_sc[hh] + pv
    m_sc[hh] = m_new


def _with_ones(v):
    return jnp.concatenate([v, jnp.ones(v.shape, v.dtype)], axis=1)


def _flash_init(m_sc, acc_sc):
    m_sc[...] = jnp.full(m_sc.shape, NEG_BIG, F32)
    acc_sc[...] = jnp.zeros(acc_sc.shape, F32)


def _flash_finish(o_ref, rows, acc_sc):
    lo_half = lax.broadcasted_iota(jnp.int32, (acc_sc.shape[1], LANES), 1) < HEAD_DIM
    o = jnp.where(lo_half, acc_sc[0, :, :LANES] / acc_sc[0, :, LANES:],
                  acc_sc[1, :, :LANES] / acc_sc[1, :, LANES:])
    o_ref[0, rows, :] = o.astype(o_ref.dtype)


def _causal_steps(n_tiles):
    pairs = [(qi, u) for qi in range(n_tiles) for u in range(qi + 1)]
    return (jnp.asarray([p[0] for p in pairs], jnp.int32), jnp.asarray([p[1] for p in pairs], jnp.int32),
            jnp.asarray([float(p[1] > 0) for p in pairs], F32))


def _pipelined(n_steps, produce, consume, bufs):
    produce(0, bufs[0])
    iters = (n_steps - 1) // STEPS_PER_ITER

    def body(it, carry):
        for j in range(STEPS_PER_ITER):
            n = STEPS_PER_ITER * it + j
            produce(n + 1, bufs[(j + 1) % 2])
            consume(n, bufs[j % 2])
        return carry

    lax.fori_loop(0, iters, body, 0)
    for n in range(STEPS_PER_ITER * iters, n_steps):
        if n + 1 < n_steps:
            produce(n + 1, bufs[(n + 1) % 2])
        consume(n, bufs[n % 2])


def _mla_kernel(qtab_ref, utab_ref, keep_ref, q_ref, k_ref, v_ref, o_ref,
                m_sc, acc_sc, sa_sc, sb_sc, mask_sc, *, fixed_shift, n_steps):
    t = TQ_MLA
    _flash_init(m_sc, acc_sc)
    row = lax.broadcasted_iota(jnp.int32, (t, t), 0)
    col = lax.broadcasted_iota(jnp.int32, (t, t), 1)
    mask_sc[0] = jnp.zeros((t, t), F32)
    mask_sc[1] = jnp.where(col <= row, 0.0, NEG_BIG)

    def produce(n, buf):
        qi, u = qtab_ref[n], utab_ref[n]
        qrows = pl.ds(pl.multiple_of(qi * t, t), t)
        krows = pl.ds(pl.multiple_of(u * t, t), t)
        diag = (u == qi).astype(jnp.int32)
        for hh in range(2):
            sl = slice(LANES * hh, LANES * (hh + 1))
            buf[hh] = lax.dot_general(q_ref[0, qrows, sl], k_ref[0, krows, sl], _NT,
                                      preferred_element_type=F32) + mask_sc[diag]

    def consume(n, buf):
        qi, u = qtab_ref[n], utab_ref[n]
        v = _with_ones(v_ref[0, pl.ds(pl.multiple_of(u * t, t), t), :])
        first = (u == 0).astype(jnp.int32)
        for hh in range(2):
            _flash_update(hh, buf[hh], v, m_sc, acc_sc, first, keep_ref[n], fixed_shift)
        _flash_finish(o_ref, pl.ds(pl.multiple_of(qi * t, t), t), acc_sc)

    _pipelined(n_steps, produce, consume, (sa_sc, sb_sc))


def _mla_call(qm, km, vm, fixed_shift):
    b, s, _ = qm.shape
    t = TQ_MLA
    qtab, utab, keep = _causal_steps(s // t)
    seq = lambda w: pl.BlockSpec((1, s, w), lambda p, bi, *_: (bi, 0, p))
    return pl.pallas_call(
        functools.partial(_mla_kernel, fixed_shift=fixed_shift, n_steps=int(qtab.shape[0])),
        out_shape=jax.ShapeDtypeStruct((b, s, MLA_WIDTH), BF16),
        grid_spec=pltpu.PrefetchScalarGridSpec(
            num_scalar_prefetch=2,
            grid=(N_PAIRS, b),
            in_specs=[pl.BlockSpec(memory_space=pltpu.SMEM),
                      seq(2 * LANES), seq(2 * LANES), seq(LANES)],
            out_specs=seq(LANES),
            scratch_shapes=[pltpu.VMEM((2, t, LANES), F32), pltpu.VMEM((2, t, 2 * LANES), F32),
                            pltpu.VMEM((2, t, t), F32), pltpu.VMEM((2, t, t), F32),
                            pltpu.VMEM((2, t, t), F32)],
        ),
        compiler_params=pltpu.CompilerParams(dimension_semantics=("arbitrary", "arbitrary"),
                                             vmem_limit_bytes=VMEM_LIMIT),
        name="mla_attention",
    )(qtab, utab, keep, qm, km, vm)


def _moba_kernel(qtab_ref, utab_ref, keep_ref, qx_ref, k_ref, v_ref, bt_ref, o_ref,
                 m_sc, acc_sc, sa_sc, sb_sc, *, fixed_shift, n_steps):
    t = MOBA_BLOCK
    w = MOBA_TILE
    nsub = w // t
    _flash_init(m_sc, acc_sc)

    def produce(n, buf):
        qi, u = qtab_ref[n], utab_ref[n]
        qrows = pl.ds(pl.multiple_of(qi * w, w), w)
        krows = pl.ds(pl.multiple_of(u * w, w), w)
        for hh in range(2):
            sl = slice(LANES * hh, LANES * (hh + 1))
            s = lax.dot_general(qx_ref[0, qrows, sl], k_ref[0, krows, sl], _NT,
                                preferred_element_type=F32)
            for qs in range(nsub):
                for ks in range(nsub):
                    d = jnp.clip(nsub * (qi - u) + (qs - ks), 0, bt_ref.shape[2] - 1)
                    rows, cols = slice(t * qs, t * (qs + 1)), slice(t * ks, t * (ks + 1))
                    buf[hh, rows, cols] = s[rows, cols] + bt_ref[0, hh, d]

    def consume(n, buf):
        qi, u = qtab_ref[n], utab_ref[n]
        v = _with_ones(v_ref[0, pl.ds(pl.multiple_of(u * w, w), w), :])
        first = (u == 0).astype(jnp.int32)
        for hh in range(2):
            _flash_update(hh, buf[hh], v, m_sc, acc_sc, first, keep_ref[n], fixed_shift)
        _flash_finish(o_ref, pl.ds(pl.multiple_of(qi * w, w), w), acc_sc)

    _pipelined(n_steps, produce, consume, (sa_sc, sb_sc))


def _moba_call(qx, kb, vb, bias_tiles, fixed_shift):
    b, s, _ = vb.shape
    t = MOBA_TILE
    n_tiles = bias_tiles.shape[2]
    qtab, utab, keep = _causal_steps(s // t)
    seq = lambda w: pl.BlockSpec((1, s, w), lambda p, bi, *_: (bi, 0, p))
    return pl.pallas_call(
        functools.partial(_moba_kernel, fixed_shift=fixed_shift, n_steps=int(qtab.shape[0])),
        out_shape=jax.ShapeDtypeStruct((b, s, MOBA_WIDTH), BF16),
        grid_spec=pltpu.PrefetchScalarGridSpec(
            num_scalar_prefetch=2,
            grid=(N_PAIRS, b),
            in_specs=[pl.BlockSpec(memory_space=pltpu.SMEM),
                      seq(2 * LANES), seq(2 * LANES), seq(LANES),
                      pl.BlockSpec((1, 2, n_tiles, MOBA_BLOCK, MOBA_BLOCK),
                                   lambda p, bi, *_: (p, 0, 0, 0, 0))],
            out_specs=seq(LANES),
            scratch_shapes=[pltpu.VMEM((2, t, LANES), F32), pltpu.VMEM((2, t, 2 * LANES), F32),
                            pltpu.VMEM((2, t, t), F32), pltpu.VMEM((2, t, t), F32)],
        ),
        compiler_params=pltpu.CompilerParams(dimension_semantics=("arbitrary", "arbitrary"),
                                             vmem_limit_bytes=VMEM_LIMIT),
        name="moba_attention",
    )(qtab, utab, keep, qx, kb, vb, bias_tiles)


def _proj_out_kernel(om_ref, gm_ref, ob_ref, gb_ref, w_ref, x_ref, mod_ref, o_ref):
    am = (_silu(gm_ref[0].astype(F32)) * om_ref[0].astype(F32)).astype(BF16)
    ab = (_silu(gb_ref[0].astype(F32)) * ob_ref[0].astype(F32)).astype(BF16)
    y = (jnp.dot(am, w_ref[0:MLA_WIDTH, :], preferred_element_type=F32)
         + jnp.dot(ab, w_ref[MLA_WIDTH:, :], preferred_element_type=F32))
    o_ref[0] = x_ref[0] + mod_ref[0, 2:3, :] * y


def _proj_out_call(o_mla, g_mla, o_moba, g_moba, w_out, x, mod3):
    b, s, d = x.shape
    tok = lambda w: pl.BlockSpec((1, TM_OUT, w), lambda bi, i: (bi, i, 0))
    return pl.pallas_call(
        _proj_out_kernel,
        out_shape=jax.ShapeDtypeStruct((b, s, d), F32),
        grid=(b, s // TM_OUT),
        in_specs=[tok(MLA_WIDTH), tok(MLA_WIDTH), tok(MOBA_WIDTH), tok(MOBA_WIDTH),
                  pl.BlockSpec(w_out.shape, lambda bi, i: (0, 0)),
                  tok(d), pl.BlockSpec((1, 3, d), lambda bi, i: (bi, 0, 0))],
        out_specs=tok(d),
        compiler_params=pltpu.CompilerParams(dimension_semantics=("arbitrary", "arbitrary"),
                                             vmem_limit_bytes=VMEM_LIMIT),
        name="proj_out",
    )(o_mla, g_mla, o_moba, g_moba, w_out, x, mod3)


def _to_head_lanes(a):
    a = jnp.concatenate([a, jnp.zeros(a.shape[:-1] + (1,), a.dtype)], axis=-1)
    return a[..., _LANE_SRC]


def _layer_weights(w_in, w_uq, w_ukv, q_g, k_g, bq_g, bk_g):
    o0 = Q_LORA
    o1 = o0 + KV_LORA
    o2 = o1 + ROPE_DIM
    k_rope = w_in[:, o1:o2]
    kr128 = jnp.zeros((D_MODEL, LANES), w_in.dtype)
    kr128 = kr128.at[:, 0:HALF_ROPE].set(k_rope[:, :HALF_ROPE])
    kr128 = kr128.at[:, LANES // 2:LANES // 2 + HALF_ROPE].set(k_rope[:, HALF_ROPE:])
    win = jnp.concatenate([w_in[:, :o1], kr128, w_in[:, o2:]], axis=1).astype(BF16)
    wuq = _to_head_lanes(w_uq.reshape(Q_LORA, H_MLA, QK_DIM)).reshape(Q_LORA, H_MLA * LANES)
    ukv = w_ukv.reshape(KV_LORA, H_MLA, NOPE_DIM + HEAD_DIM)
    k_nope = jnp.concatenate([ukv[..., :NOPE_DIM], jnp.zeros((KV_LORA, H_MLA, ROPE_DIM), ukv.dtype)], -1)
    wuk = _to_head_lanes(k_nope).reshape(KV_LORA, H_MLA * LANES)
    wuv = ukv[..., NOPE_DIM:].reshape(KV_LORA, MLA_WIDTH)
    pair = lambda g: jnp.concatenate([g, g]).reshape(1, LANES)
    return (win, wuq.astype(BF16), wuk.astype(BF16), wuv.astype(BF16),
            _to_head_lanes(q_g * MLA_QSCALE).reshape(1, LANES), _to_head_lanes(k_g).reshape(1, LANES),
            pair(bq_g), pair(bk_g))


def kernel(x, c, positions, norm_g, w_ada, b_ada, w_in, mla_q_norm_g, mla_w_uq, mla_kv_norm_g,
           mla_w_ukv, mla_q_g, mla_k_g, moba_q_g, moba_k_g, w_out, rel_bias):
    b, s, d = x.shape
    depth = w_in.shape[0]
    nb = s // MOBA_BLOCK
    assert d == D_MODEL and s % TM == 0 and s % TQ_MLA == 0 and TM % MOBA_BLOCK == 0 and b <= 8
    assert s % TG == 0 and TG % MOBA_BLOCK == 0 and s % MOBA_TILE == 0 and nb <= LANES // 4
    n_tiles = _check_far_tiles(s)

    c8 = jnp.pad(c, ((0, 8 - b), (0, 0)))
    mod = _mod_call(c8, w_ada, b_ada.reshape(depth, 1, 3 * d))[:, :b].reshape(depth, b, 3, d)

    inv_freq = ROPE_THETA ** (-jnp.arange(0, HALF_ROPE, dtype=F32) / HALF_ROPE)
    freq = jnp.zeros((LANES,), F32).at[0:HALF_ROPE].set(inv_freq)
    freq = freq.at[LANES // 2:LANES // 2 + HALF_ROPE].set(inv_freq).reshape(1, LANES)
    sign = jnp.zeros((LANES,), F32).at[0:HALF_ROPE].set(-1.0)
    sign = sign.at[LANES // 2:LANES // 2 + HALF_ROPE].set(1.0).reshape(1, LANES)
    rope_c, rope_s = _rope_call(positions.reshape(b * s, 1), freq, sign)

    bkt = _rel_bucket(jnp.arange(s, dtype=jnp.int32))
    thr = jnp.sum(bkt[None, :] < jnp.arange(N_BUCKETS, dtype=jnp.int32)[:, None], axis=1).astype(jnp.int32)
    ttab, vtab = _far_tables(thr, rel_bias, n_tiles)
    bias_tiles = _bias_tile_call(thr, ttab, rel_bias.astype(F32).reshape(-1), vtab, n_tiles)
    bias_tiles = bias_tiles.reshape(N_PAIRS, 2, n_tiles, MOBA_BLOCK, MOBA_BLOCK)
    kpad = jnp.zeros((1, LANES), F32).at[0, SHIFT_LANE].set(1.0)
    amax = lambda a: jnp.max(jnp.abs(a.astype(F32)))

    def attention(fixed_shift, qm, km, vm, qx, kb, vb):
        return (_mla_call(qm, km, vm, fixed_shift),
                _moba_call(qx, kb, vb, bias_tiles, fixed_shift))

    for l in range(depth):
        win, wuq, wuk, wuv, qg, kg, bqg, bkg = _layer_weights(
            w_in[l], mla_w_uq[l], mla_w_ukv[l], mla_q_g[l], mla_k_g[l], moba_q_g[l], moba_k_g[l])
        bound_mla = QK_DIM * amax(mla_q_g[l]) * amax(mla_k_g[l]) * MLA_QSCALE * SHIFT_SLACK
        bound_moba = (HEAD_DIM * amax(moba_q_g[l]) * amax(moba_k_g[l]) * MOBA_QSCALE
                      + amax(rel_bias) * LOG2E) * SHIFT_SLACK
        fixed_ok = (bound_mla <= MAX_SHIFT) & (bound_moba <= MAX_SHIFT)
        qpad = jnp.zeros((1, LANES), F32).at[0, SHIFT_LANE].set(jnp.where(fixed_ok, -bound_mla, 0.0))
        qm, km, vm, gm, qb, kb, vb, gb, kmean = _proj_in_call(
            x, mod[l], norm_g[l].reshape(1, d), win, mla_q_norm_g[l].reshape(1, Q_LORA), wuq,
            mla_kv_norm_g[l].reshape(1, KV_LORA), wuk, wuv, qg, kg, bqg, bkg, rope_c, rope_s,
            qpad, kpad)
        qx = _moba_gate_call(jnp.where(fixed_ok, -bound_moba, 0.0).reshape(1), qb,
                             kmean.reshape(b, nb, MOBA_WIDTH))
        o_mla, o_moba = lax.cond(
            fixed_ok, functools.partial(attention, True), functools.partial(attention, False),
            qm, km, vm, qx, kb, vb)
        x = _proj_out_call(o_mla, gm, o_moba, gb, w_out[l].astype(BF16), x, mod[l])
    return x
```

```python
import functools
import math

import numpy as np
import jax
import jax.numpy as jnp
from jax import lax
from jax.experimental import pallas as pl
from jax.experimental.pallas import tpu as pltpu

F32 = jnp.float32
BF16 = jnp.bfloat16

D_MODEL = 1024
DEPTH = 2
HEAD_DIM = 64
H_MLA = 8
H_MOBA = 8
MLA_WIDTH = H_MLA * HEAD_DIM
MOBA_WIDTH = H_MOBA * HEAD_DIM
Q_LORA = 256
KV_LORA = 128
NOPE_DIM = 64
ROPE_DIM = 32
QK_DIM = NOPE_DIM + ROPE_DIM
ROPE_THETA = 10000.0
MOBA_BLOCK = 256
MOBA_TOPK = 3
N_BUCKETS = 32
REL_MAX_DIST = 4096
EPS = 1e-6

LANES = 128
HALF_ROPE = ROPE_DIM // 2
N_PAIRS = H_MLA // 2
LOG2E = math.log2(math.e)
MLA_QSCALE = QK_DIM ** -0.5 * LOG2E
MOBA_QSCALE = HEAD_DIM ** -0.5 * LOG2E
NEG_BIG = -1e30
FAR_THRESHOLDS = 2
NEAR_TILES = 3
NEVER = 1 << 20
PEN_OFF = -2.0 ** 127
SHIFT_LANE = QK_DIM
PEN_LANE = HEAD_DIM
MAX_SHIFT = 48.0
SHIFT_SLACK = 1.02

TM = 512
TM_OUT = 512
TG = 512
TQ_MLA = 512
MOBA_TILE = 2 * MOBA_BLOCK
STEPS_PER_ITER = 8
VMEM_LIMIT = 56 * 1024 * 1024

_NT = (((1,), (1,)), ((), ()))


def _head_lane_source():
    src = np.full((LANES,), QK_DIM, np.int32)
    src[0:16] = NOPE_DIM + np.arange(16)
    src[16:64] = np.arange(48)
    src[64:80] = NOPE_DIM + HALF_ROPE + np.arange(16)
    src[80:96] = 48 + np.arange(16)
    return src


_LANE_SRC = _head_lane_source()


def _bucket_np(n, dtype):
    n = np.asarray(n)
    max_exact = N_BUCKETS // 2
    nf = np.maximum(n, 1).astype(dtype)
    large = max_exact + (np.log(nf / dtype(max_exact)) / dtype(math.log(REL_MAX_DIST / max_exact))
                         * dtype(N_BUCKETS - max_exact)).astype(np.int32)
    large = np.minimum(large, N_BUCKETS - 1)
    return np.where(n < max_exact, n, large)


def _check_far_tiles(seq):
    n = np.arange(seq + 2)
    nb = seq // MOBA_BLOCK
    n_tiles = 0
    for dtype in (np.float32, np.float64):
        b = _bucket_np(n, dtype)
        chg = np.concatenate([[0], (b[1:] != b[:-1]).astype(np.int64)])
        for d in range(NEAR_TILES, nb):
            lo, hi = MOBA_BLOCK * (d - 1) + 1, MOBA_BLOCK * (d + 1) - 1
            assert chg[lo - 1:hi + 1].sum() <= FAR_THRESHOLDS, (d, dtype)
        saturated = [d for d in range(NEAR_TILES, nb)
                     if b[MOBA_BLOCK * (d - 1) - 1] == N_BUCKETS - 1]
        n_tiles = max(n_tiles, (saturated[0] if saturated else nb - 1) + 1)
    return min(n_tiles, nb)


def _rel_bucket(dist):
    n = jnp.maximum(dist, 0)
    max_exact = N_BUCKETS // 2
    nf = jnp.maximum(n, 1).astype(F32)
    large = max_exact + (jnp.log(nf / max_exact) / math.log(REL_MAX_DIST / max_exact)
                         * (N_BUCKETS - max_exact)).astype(jnp.int32)
    large = jnp.minimum(large, N_BUCKETS - 1)
    return jnp.where(n < max_exact, n, large)


def _rms(x, g):
    return x * lax.rsqrt(jnp.mean(x * x, axis=-1, keepdims=True) + EPS) * g


def _silu(x):
    return x * jax.nn.sigmoid(x)


def _mod_kernel(c_ref, w_ref, b_ref, o_ref):
    c = c_ref[...]
    o_ref[0] = jnp.dot(_silu(c), w_ref[0], precision=lax.Precision.HIGHEST,
                       preferred_element_type=F32) + b_ref[0]


def _mod_call(c8, w_ada, b_ada):
    depth, d, d3 = w_ada.shape
    nchunk = d3 // d
    return pl.pallas_call(
        _mod_kernel,
        out_shape=jax.ShapeDtypeStruct((depth, 8, d3), F32),
        grid=(depth, nchunk),
        in_specs=[pl.BlockSpec((8, d), lambda l, j: (0, 0)),
                  pl.BlockSpec((1, d, d), lambda l, j: (l, 0, j)),
                  pl.BlockSpec((1, 1, d), lambda l, j: (l, 0, j))],
        out_specs=pl.BlockSpec((1, 8, d), lambda l, j: (l, 0, j)),
        compiler_params=pltpu.CompilerParams(dimension_semantics=("arbitrary", "arbitrary"),
                                             vmem_limit_bytes=VMEM_LIMIT),
        name="adaln_mod",
    )(c8, w_ada, b_ada)


def _rope_kernel(pos_ref, freq_ref, sign_ref, c_ref, s_ref):
    ang = pos_ref[...].astype(F32) * freq_ref[...]
    c_ref[...] = jnp.cos(ang)
    s_ref[...] = jnp.sin(ang) * sign_ref[...]


def _rope_call(pos_col, freq, sign):
    n = pos_col.shape[0]
    tr = 1024
    return pl.pallas_call(
        _rope_kernel,
        out_shape=(jax.ShapeDtypeStruct((n, LANES), F32), jax.ShapeDtypeStruct((n, LANES), F32)),
        grid=(n // tr,),
        in_specs=[pl.BlockSpec((tr, 1), lambda i: (i, 0)),
                  pl.BlockSpec((1, LANES), lambda i: (0, 0)),
                  pl.BlockSpec((1, LANES), lambda i: (0, 0))],
        out_specs=(pl.BlockSpec((tr, LANES), lambda i: (i, 0)),
                   pl.BlockSpec((tr, LANES), lambda i: (i, 0))),
        compiler_params=pltpu.CompilerParams(dimension_semantics=("arbitrary",)),
        name="rope_tables",
    )(pos_col, freq, sign)


def _bias_tile_kernel(thr_ref, ttab_ref, rb_ref, vtab_ref, o_ref):
    h = pl.program_id(0)
    nb = o_ref.shape[1]
    shape = (MOBA_BLOCK, MOBA_BLOCK)
    rc = lax.broadcasted_iota(jnp.int32, shape, 0) - lax.broadcasted_iota(jnp.int32, shape, 1)

    for d in range(NEAR_TILES):
        dist = rc + d * MOBA_BLOCK
        val = jnp.full(shape, rb_ref[h], F32)
        for b in range(1, N_BUCKETS):
            val = jnp.where(dist >= thr_ref[b], rb_ref[b * H_MOBA + h], val)
        o_ref[0, d] = jnp.where(dist >= 0, val * LOG2E, NEG_BIG)

    def far(d, carry):
        base = (d * H_MOBA + h) * (FAR_THRESHOLDS + 1)
        val = jnp.full(shape, vtab_ref[base], F32)
        for k in range(FAR_THRESHOLDS):
            val = jnp.where(rc >= ttab_ref[d * FAR_THRESHOLDS + k], vtab_ref[base + k + 1], val)
        o_ref[0, d] = val
        return carry

    lax.fori_loop(NEAR_TILES, nb, far, 0)


def _bias_tile_call(thr, ttab, rb_flat, vtab, nb):
    smem = pl.BlockSpec(memory_space=pltpu.SMEM)
    return pl.pallas_call(
        _bias_tile_kernel,
        out_shape=jax.ShapeDtypeStruct((H_MOBA, nb, MOBA_BLOCK, MOBA_BLOCK), F32),
        grid_spec=pltpu.PrefetchScalarGridSpec(
            num_scalar_prefetch=2,
            grid=(H_MOBA,),
            in_specs=[smem, smem],
            out_specs=pl.BlockSpec((1, nb, MOBA_BLOCK, MOBA_BLOCK), lambda h, *_: (h, 0, 0, 0)),
        ),
        compiler_params=pltpu.CompilerParams(dimension_semantics=("arbitrary",),
                                             vmem_limit_bytes=VMEM_LIMIT),
        name="moba_bias_tiles",
    )(thr, ttab, rb_flat, vtab)


def _far_tables(thr, rel_bias, n_tiles):
    edge = thr[1:]
    bucket = jnp.arange(1, N_BUCKETS, dtype=jnp.int32)
    keep = jnp.concatenate([edge[1:] > edge[:-1], jnp.ones((1,), bool)])
    d = jnp.arange(n_tiles, dtype=jnp.int32)[:, None]
    lo = jnp.maximum(MOBA_BLOCK * (d - 1) + 1, 0)
    hi = MOBA_BLOCK * (d + 1) - 1
    inside = keep[None, :] & (edge[None, :] > lo) & (edge[None, :] <= hi)
    cum = jnp.cumsum(inside.astype(jnp.int32), axis=1)
    idx = [jnp.sum((edge[None, :] <= lo).astype(jnp.int32), axis=1)]
    tt = []
    for k in range(1, FAR_THRESHOLDS + 1):
        sel = inside & (cum == k)
        tt.append(jnp.min(jnp.where(sel, edge[None, :], NEVER), axis=1) - MOBA_BLOCK * d[:, 0])
        idx.append(jnp.min(jnp.where(sel, bucket[None, :], N_BUCKETS - 1), axis=1))
    ttab = jnp.stack(tt, axis=1).reshape(-1).astype(jnp.int32)
    vals = rel_bias.astype(F32)[jnp.stack(idx, axis=1)] * LOG2E
    vtab = vals.transpose(0, 2, 1).reshape(-1)
    return ttab, vtab


def _mla_head(xh, g, rc, rs):
    ms = jnp.sum(xh * xh, axis=-1, keepdims=True) * (1.0 / QK_DIM)
    xn = xh * lax.rsqrt(ms + EPS) * g
    return xn * rc + pltpu.roll(xn, LANES // 2, 1) * rs


def _moba_pair_norm(x, g, lo_half):
    x2 = x * x
    s_lo = jnp.sum(jnp.where(lo_half, x2, 0.0), axis=-1, keepdims=True)
    s_hi = jnp.sum(jnp.where(lo_half, 0.0, x2), axis=-1, keepdims=True)
    ms = jnp.where(lo_half, s_lo, s_hi) * (1.0 / HEAD_DIM)
    return x * lax.rsqrt(ms + EPS) * g


def _proj_in_kernel(x_ref, mod_ref, ng_ref, win_ref, qng_ref, wuq_ref, kvng_ref, wuk_ref, wuv_ref,
                    qg_ref, kg_ref, bqg_ref, bkg_ref, rc_ref, rs_ref, qpad_ref, kpad_ref,
                    qm_ref, km_ref, vm_ref, gm_ref, qb_ref, kb_ref, vb_ref, gb_ref, kmean_ref):
    gs = ng_ref[...] * (1.0 + mod_ref[0, 1:2, :])
    shift = mod_ref[0, 0:1, :]
    qg = qg_ref[...]
    kg = kg_ref[...]
    qpad = qpad_ref[...]
    kpad = kpad_ref[...]
    bqg = bqg_ref[...]
    bkg = bkg_ref[...]
    lane = lax.broadcasted_iota(jnp.int32, (MOBA_BLOCK, LANES), 1)
    lo_half = lane < HEAD_DIM

    for sub in range(TM // MOBA_BLOCK):
        rows = slice(MOBA_BLOCK * sub, MOBA_BLOCK * (sub + 1))
        hb = (_rms(x_ref[0, rows, :], gs) + shift).astype(BF16)

        def proj(lo, hi, hb=hb):
            return jnp.dot(hb, win_ref[:, lo:hi], preferred_element_type=F32)

        rc = rc_ref[rows, :]
        rs = rs_ref[rows, :]

        cqn = _rms(proj(0, 256), qng_ref[...]).astype(BF16)
        q = jnp.dot(cqn, wuq_ref[...], preferred_element_type=F32)
        for hh in range(H_MLA):
            sl = slice(LANES * hh, LANES * (hh + 1))
            qm_ref[0, rows, sl] = (_mla_head(q[:, sl], qg, rc, rs) + qpad).astype(BF16)

        ckvn = _rms(proj(256, 384), kvng_ref[...]).astype(BF16)
        kn = jnp.dot(ckvn, wuk_ref[...], preferred_element_type=F32)
        kr = proj(384, 512)
        for hh in range(H_MLA):
            sl = slice(LANES * hh, LANES * (hh + 1))
            km_ref[0, rows, sl] = (_mla_head(kn[:, sl] + kr, kg, rc, rs) + kpad).astype(BF16)
        vm_ref[0, rows, :] = jnp.dot(ckvn, wuv_ref[...], preferred_element_type=F32).astype(BF16)
        gm_ref[0, rows, :] = proj(512, 1024).astype(BF16)

        blk_id = pl.program_id(1) * (TM // MOBA_BLOCK) + sub
        onehot = (lane == PEN_LANE + blk_id).astype(F32)
        qbz = proj(1024, 1536)
        kbz = proj(1536, 2048)
        for p in range(N_PAIRS):
            sl = slice(LANES * p, LANES * (p + 1))
            qb_ref[0, rows, sl] = _moba_pair_norm(qbz[:, sl], bqg, lo_half)
            kn_p = _moba_pair_norm(kbz[:, sl], bkg, lo_half)
            kb_ref[0, rows, LANES * 2 * p:LANES * (2 * p + 1)] = jnp.where(
                lo_half, kn_p, onehot).astype(BF16)
            kb_ref[0, rows, LANES * (2 * p + 1):LANES * (2 * p + 2)] = jnp.where(
                lo_half, pltpu.roll(kn_p, HEAD_DIM, 1), onehot).astype(BF16)
            kmean_ref[0, 0, sub:sub + 1, sl] = jnp.sum(kn_p, axis=0, keepdims=True) * (1.0 / MOBA_BLOCK)
        vb_ref[0, rows, :] = proj(2048, 2560).astype(BF16)
        gb_ref[0, rows, :] = proj(2560, 3072).astype(BF16)


def _proj_in_call(x, mod3, ng, win, qng, wuq, kvng, wuk, wuv, qg, kg, bqg, bkg, rope_c, rope_s,
                  qpad, kpad):
    b, s, d = x.shape
    nt = s // TM
    const2 = lambda bi, i: (0, 0)
    tok = lambda w: pl.BlockSpec((1, TM, w), lambda bi, i: (bi, i, 0))
    full = lambda a: pl.BlockSpec(a.shape, const2)
    rope_spec = pl.BlockSpec((TM, LANES), lambda bi, i: (bi * nt + i, 0))
    out_shape = (
        jax.ShapeDtypeStruct((b, s, H_MLA * LANES), BF16),
        jax.ShapeDtypeStruct((b, s, H_MLA * LANES), BF16),
        jax.ShapeDtypeStruct((b, s, MLA_WIDTH), BF16),
        jax.ShapeDtypeStruct((b, s, MLA_WIDTH), BF16),
        jax.ShapeDtypeStruct((b, s, MOBA_WIDTH), F32),
        jax.ShapeDtypeStruct((b, s, H_MOBA * LANES), BF16),
        jax.ShapeDtypeStruct((b, s, MOBA_WIDTH), BF16),
        jax.ShapeDtypeStruct((b, s, MOBA_WIDTH), BF16),
        jax.ShapeDtypeStruct((b, nt, TM // MOBA_BLOCK, MOBA_WIDTH), F32),
    )
    out_specs = (tok(H_MLA * LANES), tok(H_MLA * LANES), tok(MLA_WIDTH), tok(MLA_WIDTH),
                 tok(MOBA_WIDTH), tok(H_MOBA * LANES), tok(MOBA_WIDTH), tok(MOBA_WIDTH),
                 pl.BlockSpec((1, 1, TM // MOBA_BLOCK, MOBA_WIDTH), lambda bi, i: (bi, i, 0, 0)))
    return pl.pallas_call(
        _proj_in_kernel,
        out_shape=out_shape,
        grid=(b, nt),
        in_specs=[tok(d), pl.BlockSpec((1, 3, d), lambda bi, i: (bi, 0, 0)), full(ng), full(win),
                  full(qng), full(wuq), full(kvng), full(wuk), full(wuv),
                  full(qg), full(kg), full(bqg), full(bkg), rope_spec, rope_spec,
                  full(qpad), full(kpad)],
        out_specs=out_specs,
        compiler_params=pltpu.CompilerParams(dimension_semantics=("arbitrary", "arbitrary"),
                                             vmem_limit_bytes=VMEM_LIMIT),
        name="proj_in",
    )(x, mod3, ng, win, qng, wuq, kvng, wuk, wuv, qg, kg, bqg, bkg, rope_c, rope_s, qpad, kpad)


def _moba_gate_kernel(neg_shift_ref, q_ref, kmean_ref, qx_ref):
    nb = kmean_ref.shape[1]
    nh = H_MOBA
    km = kmean_ref[0]
    q = q_ref[0]
    head_of_lane = jnp.right_shift(lax.broadcasted_iota(jnp.int32, (nb, MOBA_WIDTH), 1),
                                   HEAD_DIM.bit_length() - 1)
    km_heads = jnp.concatenate([jnp.where(head_of_lane == h, km, 0.0) for h in range(nh)], axis=0)
    g = lax.dot_general(km_heads, q, _NT, precision=lax.Precision.HIGHEST,
                        preferred_element_type=F32).reshape(nh, nb, TG)
    blk = lax.broadcasted_iota(jnp.int32, (nh, nb, TG), 1)
    blkf = blk.astype(F32)
    own = pl.program_id(1) * (TG // MOBA_BLOCK) + jnp.right_shift(
        lax.broadcasted_iota(jnp.int32, (nh, nb, TG), 2), MOBA_BLOCK.bit_length() - 1)
    past = blk < own
    rem = jnp.where(past, g, -jnp.inf)
    for k in range(MOBA_TOPK):
        mx = jnp.max(rem, axis=1, keepdims=True)
        idx = jnp.min(jnp.where(rem == mx, blkf, float(nb)), axis=1, keepdims=True)
        rem = jnp.where(blkf == idx, -jnp.inf, rem)
    on = (blk == own) | (past & (rem == -jnp.inf))
    pen = jnp.where(on, neg_shift_ref[0], PEN_OFF).astype(BF16).reshape(nh * nb, TG)
    r = lax.broadcasted_iota(jnp.int32, (nh * nb, nh * LANES), 0)
    to_lane = (lax.broadcasted_iota(jnp.int32, (nh * nb, nh * LANES), 1)
               == jnp.right_shift(r, nb.bit_length() - 1) * LANES + PEN_LANE
               + jnp.bitwise_and(r, nb - 1)).astype(BF16)
    pen_lanes = lax.dot_general(pen, to_lane, (((0,), (0,)), ((), ())),
                                preferred_element_type=F32)
    lo_q = lax.broadcasted_iota(jnp.int32, (TG, LANES), 1) < HEAD_DIM
    qs = q * MOBA_QSCALE
    tiles = []
    for p in range(N_PAIRS):
        qs_p = qs[:, LANES * p:LANES * (p + 1)]
        tiles += [jnp.where(lo_q, qs_p, 0.0), pltpu.roll(jnp.where(lo_q, 0.0, qs_p), HEAD_DIM, 1)]
    qx_ref[0] = (jnp.concatenate(tiles, axis=1) + pen_lanes).astype(BF16)


def _moba_gate_call(neg_shift, qb, kmean):
    b, s, _ = qb.shape
    nb = kmean.shape[1]
    return pl.pallas_call(
        _moba_gate_kernel,
        out_shape=jax.ShapeDtypeStruct((b, s, H_MOBA * LANES), BF16),
        grid=(b, s // TG),
        in_specs=[pl.BlockSpec(memory_space=pltpu.SMEM),
                  pl.BlockSpec((1, TG, MOBA_WIDTH), lambda bi, i: (bi, i, 0)),
                  pl.BlockSpec((1, nb, MOBA_WIDTH), lambda bi, i: (bi, 0, 0))],
        out_specs=pl.BlockSpec((1, TG, H_MOBA * LANES), lambda bi, i: (bi, i, 0)),
        compiler_params=pltpu.CompilerParams(dimension_semantics=("arbitrary", "arbitrary"),
                                             vmem_limit_bytes=VMEM_LIMIT),
        name="moba_gate",
    )(neg_shift, qb, kmean)


def _flash_update(hh, s, v_ext, m_sc, acc_sc, first, keep, fixed_shift):
    if fixed_shift:
        acc_sc[hh] = acc_sc[hh] * keep + jnp.dot(jnp.exp2(s).astype(BF16), v_ext,
                                                 preferred_element_type=F32)
        return
    m_prev = jnp.where(first == 1, NEG_BIG, m_sc[hh])
    m_new = jnp.maximum(m_prev, jnp.max(s, axis=-1, keepdims=True))
    alpha = jnp.exp2(m_prev - m_new)
    p = jnp.concatenate([jnp.exp2(s[:, LANES * c:LANES * (c + 1)] - m_new)
                         for c in range(s.shape[1] // LANES)], axis=1).astype(BF16)
    pv = jnp.dot(p, v_ext, preferred_element_type=F32)
    acc_sc[hh] = jnp.concatenate([alpha, alpha], axis=1) * acc_sc[hh] + pv
    m_sc[hh] = m_new


def _with_ones(v):
    return jnp.concatenate([v, jnp.ones(v.shape, v.dtype)], axis=1)


def _flash_init(m_sc, acc_sc):
    m_sc[...] = jnp.full(m_sc.shape, NEG_BIG, F32)
    acc_sc[...] = jnp.zeros(acc_sc.shape, F32)


def _flash_finish(o_ref, rows, acc_sc):
    lo_half = lax.broadcasted_iota(jnp.int32, (acc_sc.shape[1], LANES), 1) < HEAD_DIM
    o = jnp.where(lo_half, acc_sc[0, :, :LANES] / acc_sc[0, :, LANES:],
                  acc_sc[1, :, :LANES] / acc_sc[1, :, LANES:])
    o_ref[0, rows, :] = o.astype(o_ref.dtype)


def _causal_steps(n_tiles):
    pairs = [(qi, u) for qi in range(n_tiles) for u in range(qi + 1)]
    return (jnp.asarray([p[0] for p in pairs], jnp.int32), jnp.asarray([p[1] for p in pairs], jnp.int32),
            jnp.asarray([float(p[1] > 0) for p in pairs], F32))


def _pipelined(n_steps, produce, consume, bufs):
    produce(0, bufs[0])
    iters = (n_steps - 1) // STEPS_PER_ITER

    def body(it, carry):
        for j in range(STEPS_PER_ITER):
            n = STEPS_PER_ITER * it + j
            produce(n + 1, bufs[(j + 1) % 2])
            consume(n, bufs[j % 2])
        return carry

    lax.fori_loop(0, iters, body, 0)
    for n in range(STEPS_PER_ITER * iters, n_steps):
        if n + 1 < n_steps:
            produce(n + 1, bufs[(n + 1) % 2])
        consume(n, bufs[n % 2])


def _mla_kernel(qtab_ref, utab_ref, keep_ref, q_ref, k_ref, v_ref, o_ref,
                m_sc, acc_sc, sa_sc, sb_sc, mask_sc, *, fixed_shift, n_steps):
    t = TQ_MLA
    _flash_init(m_sc, acc_sc)
    row = lax.broadcasted_iota(jnp.int32, (t, t), 0)
    col = lax.broadcasted_iota(jnp.int32, (t, t), 1)
    mask_sc[0] = jnp.zeros((t, t), F32)
    mask_sc[1] = jnp.where(col <= row, 0.0, NEG_BIG)

    def produce(n, buf):
        qi, u = qtab_ref[n], utab_ref[n]
        qrows = pl.ds(pl.multiple_of(qi * t, t), t)
        krows = pl.ds(pl.multiple_of(u * t, t), t)
        diag = (u == qi).astype(jnp.int32)
        for hh in range(2):
            sl = slice(LANES * hh, LANES * (hh + 1))
            buf[hh] = lax.dot_general(q_ref[0, qrows, sl], k_ref[0, krows, sl], _NT,
                                      preferred_element_type=F32) + mask_sc[diag]

    def consume(n, buf):
        qi, u = qtab_ref[n], utab_ref[n]
        v = _with_ones(v_ref[0, pl.ds(pl.multiple_of(u * t, t), t), :])
        first = (u == 0).astype(jnp.int32)
        for hh in range(2):
            _flash_update(hh, buf[hh], v, m_sc, acc_sc, first, keep_ref[n], fixed_shift)
        _flash_finish(o_ref, pl.ds(pl.multiple_of(qi * t, t), t), acc_sc)

    _pipelined(n_steps, produce, consume, (sa_sc, sb_sc))


def _mla_call(qm, km, vm, fixed_shift):
    b, s, _ = qm.shape
    t = TQ_MLA
    qtab, utab, keep = _causal_steps(s // t)
    seq = lambda w: pl.BlockSpec((1, s, w), lambda p, bi, *_: (bi, 0, p))
    return pl.pallas_call(
        functools.partial(_mla_kernel, fixed_shift=fixed_shift, n_steps=int(qtab.shape[0])),
        out_shape=jax.ShapeDtypeStruct((b, s, MLA_WIDTH), BF16),
        grid_spec=pltpu.PrefetchScalarGridSpec(
            num_scalar_prefetch=2,
            grid=(N_PAIRS, b),
            in_specs=[pl.BlockSpec(memory_space=pltpu.SMEM),
                      seq(2 * LANES), seq(2 * LANES), seq(LANES)],
            out_specs=seq(LANES),
            scratch_shapes=[pltpu.VMEM((2, t, LANES), F32), pltpu.VMEM((2, t, 2 * LANES), F32),
                            pltpu.VMEM((2, t, t), F32), pltpu.VMEM((2, t, t), F32),
                            pltpu.VMEM((2, t, t), F32)],
        ),
        compiler_params=pltpu.CompilerParams(dimension_semantics=("arbitrary", "arbitrary"),
                                             vmem_limit_bytes=VMEM_LIMIT),
        name="mla_attention",
    )(qtab, utab, keep, qm, km, vm)


def _moba_kernel(qtab_ref, utab_ref, keep_ref, qx_ref, k_ref, v_ref, bt_ref, o_ref,
                 m_sc, acc_sc, sa_sc, sb_sc, *, fixed_shift, n_steps):
    t = MOBA_BLOCK
    w = MOBA_TILE
    nsub = w // t
    _flash_init(m_sc, acc_sc)

    def produce(n, buf):
        qi, u = qtab_ref[n], utab_ref[n]
        qrows = pl.ds(pl.multiple_of(qi * w, w), w)
        krows = pl.ds(pl.multiple_of(u * w, w), w)
        for hh in range(2):
            sl = slice(LANES * hh, LANES * (hh + 1))
            s = lax.dot_general(qx_ref[0, qrows, sl], k_ref[0, krows, sl], _NT,
                                preferred_element_type=F32)
            for qs in range(nsub):
                for ks in range(nsub):
                    d = jnp.clip(nsub * (qi - u) + (qs - ks), 0, bt_ref.shape[2] - 1)
                    rows, cols = slice(t * qs, t * (qs + 1)), slice(t * ks, t * (ks + 1))
                    buf[hh, rows, cols] = s[rows, cols] + bt_ref[0, hh, d]

    def consume(n, buf):
        qi, u = qtab_ref[n], utab_ref[n]
        v = _with_ones(v_ref[0, pl.ds(pl.multiple_of(u * w, w), w), :])
        first = (u == 0).astype(jnp.int32)
        for hh in range(2):
            _flash_update(hh, buf[hh], v, m_sc, acc_sc, first, keep_ref[n], fixed_shift)
        _flash_finish(o_ref, pl.ds(pl.multiple_of(qi * w, w), w), acc_sc)

    _pipelined(n_steps, produce, consume, (sa_sc, sb_sc))


def _moba_call(qx, kb, vb, bias_tiles, fixed_shift):
    b, s, _ = vb.shape
    t = MOBA_TILE
    n_tiles = bias_tiles.shape[2]
    qtab, utab, keep = _causal_steps(s // t)
    seq = lambda w: pl.BlockSpec((1, s, w), lambda p, bi, *_: (bi, 0, p))
    return pl.pallas_call(
        functools.partial(_moba_kernel, fixed_shift=fixed_shift, n_steps=int(qtab.shape[0])),
        out_shape=jax.ShapeDtypeStruct((b, s, MOBA_WIDTH), BF16),
        grid_spec=pltpu.PrefetchScalarGridSpec(
            num_scalar_prefetch=2,
            grid=(N_PAIRS, b),
            in_specs=[pl.BlockSpec(memory_space=pltpu.SMEM),
                      seq(2 * LANES), seq(2 * LANES), seq(LANES),
                      pl.BlockSpec((1, 2, n_tiles, MOBA_BLOCK, MOBA_BLOCK),
                                   lambda p, bi, *_: (p, 0, 0, 0, 0))],
            out_specs=seq(LANES),
            scratch_shapes=[pltpu.VMEM((2, t, LANES), F32), pltpu.VMEM((2, t, 2 * LANES), F32),
                            pltpu.VMEM((2, t, t), F32), pltpu.VMEM((2, t, t), F32)],
        ),
        compiler_params=pltpu.CompilerParams(dimension_semantics=("arbitrary", "arbitrary"),
                                             vmem_limit_bytes=VMEM_LIMIT),
        name="moba_attention",
    )(qtab, utab, keep, qx, kb, vb, bias_tiles)


def _proj_out_kernel(om_ref, gm_ref, ob_ref, gb_ref, w_ref, x_ref, mod_ref, o_ref):
    am = (_silu(gm_ref[0].astype(F32)) * om_ref[0].astype(F32)).astype(BF16)
    ab = (_silu(gb_ref[0].astype(F32)) * ob_ref[0].astype(F32)).astype(BF16)
    y = (jnp.dot(am, w_ref[0:MLA_WIDTH, :], preferred_element_type=F32)
         + jnp.dot(ab, w_ref[MLA_WIDTH:, :], preferred_element_type=F32))
    o_ref[0] = x_ref[0] + mod_ref[0, 2:3, :] * y


def _proj_out_call(o_mla, g_mla, o_moba, g_moba, w_out, x, mod3):
    b, s, d = x.shape
    tok = lambda w: pl.BlockSpec((1, TM_OUT, w), lambda bi, i: (bi, i, 0))
    return pl.pallas_call(
        _proj_out_kernel,
        out_shape=jax.ShapeDtypeStruct((b, s, d), F32),
        grid=(b, s // TM_OUT),
        in_specs=[tok(MLA_WIDTH), tok(MLA_WIDTH), tok(MOBA_WIDTH), tok(MOBA_WIDTH),
                  pl.BlockSpec(w_out.shape, lambda bi, i: (0, 0)),
                  tok(d), pl.BlockSpec((1, 3, d), lambda bi, i: (bi, 0, 0))],
        out_specs=tok(d),
        compiler_params=pltpu.CompilerParams(dimension_semantics=("arbitrary", "arbitrary"),
                                             vmem_limit_bytes=VMEM_LIMIT),
        name="proj_out",
    )(o_mla, g_mla, o_moba, g_moba, w_out, x, mod3)


def _to_head_lanes(a):
    a = jnp.concatenate([a, jnp.zeros(a.shape[:-1] + (1,), a.dtype)], axis=-1)
    return a[..., _LANE_SRC]


def _layer_weights(w_in, w_uq, w_ukv, q_g, k_g, bq_g, bk_g):
    o0 = Q_LORA
    o1 = o0 + KV_LORA
    o2 = o1 + ROPE_DIM
    k_rope = w_in[:, o1:o2]
    kr128 = jnp.zeros((D_MODEL, LANES), w_in.dtype)
    kr128 = kr128.at[:, 0:HALF_ROPE].set(k_rope[:, :HALF_ROPE])
    kr128 = kr128.at[:, LANES // 2:LANES // 2 + HALF_ROPE].set(k_rope[:, HALF_ROPE:])
    win = jnp.concatenate([w_in[:, :o1], kr128, w_in[:, o2:]], axis=1).astype(BF16)
    wuq = _to_head_lanes(w_uq.reshape(Q_LORA, H_MLA, QK_DIM)).reshape(Q_LORA, H_MLA * LANES)
    ukv = w_ukv.reshape(KV_LORA, H_MLA, NOPE_DIM + HEAD_DIM)
    k_nope = jnp.concatenate([ukv[..., :NOPE_DIM], jnp.zeros((KV_LORA, H_MLA, ROPE_DIM), ukv.dtype)], -1)
    wuk = _to_head_lanes(k_nope).reshape(KV_LORA, H_MLA * LANES)
    wuv = ukv[..., NOPE_DIM:].reshape(KV_LORA, MLA_WIDTH)
    pair = lambda g: jnp.concatenate([g, g]).reshape(1, LANES)
    return (win, wuq.astype(BF16), wuk.astype(BF16), wuv.astype(BF16),
            _to_head_lanes(q_g * MLA_QSCALE).reshape(1, LANES), _to_head_lanes(k_g).reshape(1, LANES),
            pair(bq_g), pair(bk_g))


def kernel(x, c, positions, norm_g, w_ada, b_ada, w_in, mla_q_norm_g, mla_w_uq, mla_kv_norm_g,
           mla_w_ukv, mla_q_g, mla_k_g, moba_q_g, moba_k_g, w_out, rel_bias):
    b, s, d = x.shape
    depth = w_in.shape[0]
    nb = s // MOBA_BLOCK
    assert d == D_MODEL and s % TM == 0 and s % TQ_MLA == 0 and TM % MOBA_BLOCK == 0 and b <= 8
    assert s % TG == 0 and TG % MOBA_BLOCK == 0 and s % MOBA_TILE == 0
    assert nb <= LANES // 4 and nb & (nb - 1) == 0
    n_tiles = _check_far_tiles(s)

    c8 = jnp.pad(c, ((0, 8 - b), (0, 0)))
    mod = _mod_call(c8, w_ada, b_ada.reshape(depth, 1, 3 * d))[:, :b].reshape(depth, b, 3, d)

    inv_freq = ROPE_THETA ** (-jnp.arange(0, HALF_ROPE, dtype=F32) / HALF_ROPE)
    freq = jnp.zeros((LANES,), F32).at[0:HALF_ROPE].set(inv_freq)
    freq = freq.at[LANES // 2:LANES // 2 + HALF_ROPE].set(inv_freq).reshape(1, LANES)
    sign = jnp.zeros((LANES,), F32).at[0:HALF_ROPE].set(-1.0)
    sign = sign.at[LANES // 2:LANES // 2 + HALF_ROPE].set(1.0).reshape(1, LANES)
    rope_c, rope_s = _rope_call(positions.reshape(b * s, 1), freq, sign)

    bkt = _rel_bucket(jnp.arange(s, dtype=jnp.int32))
    thr = jnp.sum(bkt[None, :] < jnp.arange(N_BUCKETS, dtype=jnp.int32)[:, None], axis=1).astype(jnp.int32)
    ttab, vtab = _far_tables(thr, rel_bias, n_tiles)
    bias_tiles = _bias_tile_call(thr, ttab, rel_bias.astype(F32).reshape(-1), vtab, n_tiles)
    bias_tiles = bias_tiles.reshape(N_PAIRS, 2, n_tiles, MOBA_BLOCK, MOBA_BLOCK)
    kpad = jnp.zeros((1, LANES), F32).at[0, SHIFT_LANE].set(1.0)
    amax = lambda a: jnp.max(jnp.abs(a.astype(F32)))

    def attention(fixed_shift, qm, km, vm, qx, kb, vb):
        return (_mla_call(qm, km, vm, fixed_shift),
                _moba_call(qx, kb, vb, bias_tiles, fixed_shift))

    layer_weights = jax.vmap(_layer_weights)(w_in, mla_w_uq, mla_w_ukv, mla_q_g, mla_k_g,
                                             moba_q_g, moba_k_g)
    w_out_bf16 = w_out.astype(BF16)
    for l in range(depth):
        win, wuq, wuk, wuv, qg, kg, bqg, bkg = (w[l] for w in layer_weights)
        bound_mla = QK_DIM * amax(mla_q_g[l]) * amax(mla_k_g[l]) * MLA_QSCALE * SHIFT_SLACK
        bound_moba = (HEAD_DIM * amax(moba_q_g[l]) * amax(moba_k_g[l]) * MOBA_QSCALE
                      + amax(rel_bias) * LOG2E) * SHIFT_SLACK
        fixed_ok = (bound_mla <= MAX_SHIFT) & (bound_moba <= MAX_SHIFT)
        qpad = jnp.zeros((1, LANES), F32).at[0, SHIFT_LANE].set(jnp.where(fixed_ok, -bound_mla, 0.0))
        qm, km, vm, gm, qb, kb, vb, gb, kmean = _proj_in_call(
            x, mod[l], norm_g[l].reshape(1, d), win, mla_q_norm_g[l].reshape(1, Q_LORA), wuq,
            mla_kv_norm_g[l].reshape(1, KV_LORA), wuk, wuv, qg, kg, bqg, bkg, rope_c, rope_s,
            qpad, kpad)
        qx = _moba_gate_call(jnp.where(fixed_ok, -bound_moba, 0.0).reshape(1), qb,
                             kmean.reshape(b, nb, MOBA_WIDTH))
        o_mla, o_moba = lax.cond(
            fixed_ok, functools.partial(attention, True), functools.partial(attention, False),
            qm, km, vm, qx, kb, vb)
        x = _proj_out_call(o_mla, gm, o_moba, gb, w_out_bf16[l], x, mod[l])
    return x
```

```python
import functools
import math

import numpy as np
import jax
import jax.numpy as jnp
from jax import lax
from jax.experimental import pallas as pl
from jax.experimental.pallas import tpu as pltpu

F32 = jnp.float32
BF16 = jnp.bfloat16

D_MODEL = 1024
DEPTH = 2
HEAD_DIM = 64
H_MLA = 8
H_MOBA = 8
MLA_WIDTH = H_MLA * HEAD_DIM
MOBA_WIDTH = H_MOBA * HEAD_DIM
Q_LORA = 256
KV_LORA = 128
NOPE_DIM = 64
ROPE_DIM = 32
QK_DIM = NOPE_DIM + ROPE_DIM
ROPE_THETA = 10000.0
MOBA_BLOCK = 256
MOBA_TOPK = 3
N_BUCKETS = 32
REL_MAX_DIST = 4096
EPS = 1e-6

LANES = 128
HALF_ROPE = ROPE_DIM // 2
N_PAIRS = H_MLA // 2
LOG2E = math.log2(math.e)
MLA_QSCALE = QK_DIM ** -0.5 * LOG2E
MOBA_QSCALE = HEAD_DIM ** -0.5 * LOG2E
NEG_BIG = -1e30
FAR_THRESHOLDS = 2
NEAR_TILES = 3
NEVER = 1 << 20
PEN_OFF = -2.0 ** 127
SHIFT_LANE = QK_DIM
PEN_LANE = HEAD_DIM
MAX_SHIFT = 48.0
SHIFT_SLACK = 1.02

TM = 512
TM_OUT = 1024
TG = 512
TQ_MLA = 512
MOBA_TILE = 2 * MOBA_BLOCK
STEPS_PER_ITER = 8
VMEM_LIMIT = 56 * 1024 * 1024

_NT = (((1,), (1,)), ((), ()))


def _head_lane_source():
    src = np.full((LANES,), QK_DIM, np.int32)
    src[0:16] = NOPE_DIM + np.arange(16)
    src[16:64] = np.arange(48)
    src[64:80] = NOPE_DIM + HALF_ROPE + np.arange(16)
    src[80:96] = 48 + np.arange(16)
    return src


_LANE_SRC = _head_lane_source()


def _bucket_np(n, dtype):
    n = np.asarray(n)
    max_exact = N_BUCKETS // 2
    nf = np.maximum(n, 1).astype(dtype)
    large = max_exact + (np.log(nf / dtype(max_exact)) / dtype(math.log(REL_MAX_DIST / max_exact))
                         * dtype(N_BUCKETS - max_exact)).astype(np.int32)
    large = np.minimum(large, N_BUCKETS - 1)
    return np.where(n < max_exact, n, large)


def _check_far_tiles(seq):
    n = np.arange(seq + 2)
    nb = seq // MOBA_BLOCK
    n_tiles = 0
    for dtype in (np.float32, np.float64):
        b = _bucket_np(n, dtype)
        chg = np.concatenate([[0], (b[1:] != b[:-1]).astype(np.int64)])
        for d in range(NEAR_TILES, nb):
            lo, hi = MOBA_BLOCK * (d - 1) + 1, MOBA_BLOCK * (d + 1) - 1
            assert chg[lo - 1:hi + 1].sum() <= FAR_THRESHOLDS, (d, dtype)
        saturated = [d for d in range(NEAR_TILES, nb)
                     if b[MOBA_BLOCK * (d - 1) - 1] == N_BUCKETS - 1]
        n_tiles = max(n_tiles, (saturated[0] if saturated else nb - 1) + 1)
    return min(n_tiles, nb)


def _rel_bucket(dist):
    n = jnp.maximum(dist, 0)
    max_exact = N_BUCKETS // 2
    nf = jnp.maximum(n, 1).astype(F32)
    large = max_exact + (jnp.log(nf / max_exact) / math.log(REL_MAX_DIST / max_exact)
                         * (N_BUCKETS - max_exact)).astype(jnp.int32)
    large = jnp.minimum(large, N_BUCKETS - 1)
    return jnp.where(n < max_exact, n, large)


def _rms(x, g):
    return x * lax.rsqrt(jnp.mean(x * x, axis=-1, keepdims=True) + EPS) * g


def _silu(x):
    return x * jax.nn.sigmoid(x)


def _mod_kernel(c_ref, w_ref, b_ref, o_ref):
    c = c_ref[...]
    o_ref[0] = jnp.dot(_silu(c), w_ref[0], precision=lax.Precision.HIGHEST,
                       preferred_element_type=F32) + b_ref[0]


def _mod_call(c8, w_ada, b_ada):
    depth, d, d3 = w_ada.shape
    nchunk = d3 // d
    return pl.pallas_call(
        _mod_kernel,
        out_shape=jax.ShapeDtypeStruct((depth, 8, d3), F32),
        grid=(depth, nchunk),
        in_specs=[pl.BlockSpec((8, d), lambda l, j: (0, 0)),
                  pl.BlockSpec((1, d, d), lambda l, j: (l, 0, j)),
                  pl.BlockSpec((1, 1, d), lambda l, j: (l, 0, j))],
        out_specs=pl.BlockSpec((1, 8, d), lambda l, j: (l, 0, j)),
        compiler_params=pltpu.CompilerParams(dimension_semantics=("arbitrary", "arbitrary"),
                                             vmem_limit_bytes=VMEM_LIMIT),
        name="adaln_mod",
    )(c8, w_ada, b_ada)


def _rope_kernel(pos_ref, freq_ref, sign_ref, c_ref, s_ref):
    ang = pos_ref[...].astype(F32) * freq_ref[...]
    c_ref[...] = jnp.cos(ang)
    s_ref[...] = jnp.sin(ang) * sign_ref[...]


def _rope_call(pos_col, freq, sign):
    n = pos_col.shape[0]
    tr = 1024
    return pl.pallas_call(
        _rope_kernel,
        out_shape=(jax.ShapeDtypeStruct((n, LANES), F32), jax.ShapeDtypeStruct((n, LANES), F32)),
        grid=(n // tr,),
        in_specs=[pl.BlockSpec((tr, 1), lambda i: (i, 0)),
                  pl.BlockSpec((1, LANES), lambda i: (0, 0)),
                  pl.BlockSpec((1, LANES), lambda i: (0, 0))],
        out_specs=(pl.BlockSpec((tr, LANES), lambda i: (i, 0)),
                   pl.BlockSpec((tr, LANES), lambda i: (i, 0))),
        compiler_params=pltpu.CompilerParams(dimension_semantics=("arbitrary",)),
        name="rope_tables",
    )(pos_col, freq, sign)


def _bias_tile_kernel(thr_ref, ttab_ref, rb_ref, vtab_ref, o_ref):
    h = pl.program_id(0)
    nb = o_ref.shape[1]
    shape = (MOBA_BLOCK, MOBA_BLOCK)
    rc = lax.broadcasted_iota(jnp.int32, shape, 0) - lax.broadcasted_iota(jnp.int32, shape, 1)

    for d in range(NEAR_TILES):
        dist = rc + d * MOBA_BLOCK
        val = jnp.full(shape, rb_ref[h], F32)
        for b in range(1, N_BUCKETS):
            val = jnp.where(dist >= thr_ref[b], rb_ref[b * H_MOBA + h], val)
        o_ref[0, d] = jnp.where(dist >= 0, val * LOG2E, NEG_BIG)

    def far(d, carry):
        base = (d * H_MOBA + h) * (FAR_THRESHOLDS + 1)
        val = jnp.full(shape, vtab_ref[base], F32)
        for k in range(FAR_THRESHOLDS):
            val = jnp.where(rc >= ttab_ref[d * FAR_THRESHOLDS + k], vtab_ref[base + k + 1], val)
        o_ref[0, d] = val
        return carry

    lax.fori_loop(NEAR_TILES, nb, far, 0)


def _bias_tile_call(thr, ttab, rb_flat, vtab, nb):
    smem = pl.BlockSpec(memory_space=pltpu.SMEM)
    return pl.pallas_call(
        _bias_tile_kernel,
        out_shape=jax.ShapeDtypeStruct((H_MOBA, nb, MOBA_BLOCK, MOBA_BLOCK), F32),
        grid_spec=pltpu.PrefetchScalarGridSpec(
            num_scalar_prefetch=2,
            grid=(H_MOBA,),
            in_specs=[smem, smem],
            out_specs=pl.BlockSpec((1, nb, MOBA_BLOCK, MOBA_BLOCK), lambda h, *_: (h, 0, 0, 0)),
        ),
        compiler_params=pltpu.CompilerParams(dimension_semantics=("arbitrary",),
                                             vmem_limit_bytes=VMEM_LIMIT),
        name="moba_bias_tiles",
    )(thr, ttab, rb_flat, vtab)


def _far_tables(thr, rel_bias, n_tiles):
    edge = thr[1:]
    bucket = jnp.arange(1, N_BUCKETS, dtype=jnp.int32)
    keep = jnp.concatenate([edge[1:] > edge[:-1], jnp.ones((1,), bool)])
    d = jnp.arange(n_tiles, dtype=jnp.int32)[:, None]
    lo = jnp.maximum(MOBA_BLOCK * (d - 1) + 1, 0)
    hi = MOBA_BLOCK * (d + 1) - 1
    inside = keep[None, :] & (edge[None, :] > lo) & (edge[None, :] <= hi)
    cum = jnp.cumsum(inside.astype(jnp.int32), axis=1)
    idx = [jnp.sum((edge[None, :] <= lo).astype(jnp.int32), axis=1)]
    tt = []
    for k in range(1, FAR_THRESHOLDS + 1):
        sel = inside & (cum == k)
        tt.append(jnp.min(jnp.where(sel, edge[None, :], NEVER), axis=1) - MOBA_BLOCK * d[:, 0])
        idx.append(jnp.min(jnp.where(sel, bucket[None, :], N_BUCKETS - 1), axis=1))
    ttab = jnp.stack(tt, axis=1).reshape(-1).astype(jnp.int32)
    vals = rel_bias.astype(F32)[jnp.stack(idx, axis=1)] * LOG2E
    vtab = vals.transpose(0, 2, 1).reshape(-1)
    return ttab, vtab


def _mla_head(xh, g, rc, rs):
    ms = jnp.sum(xh * xh, axis=-1, keepdims=True) * (1.0 / QK_DIM)
    xn = xh * lax.rsqrt(ms + EPS) * g
    return xn * rc + pltpu.roll(xn, LANES // 2, 1) * rs


def _moba_pair_norm(x, g, lo_half):
    x2 = x * x
    s_lo = jnp.sum(jnp.where(lo_half, x2, 0.0), axis=-1, keepdims=True)
    s_hi = jnp.sum(jnp.where(lo_half, 0.0, x2), axis=-1, keepdims=True)
    ms = jnp.where(lo_half, s_lo, s_hi) * (1.0 / HEAD_DIM)
    return x * lax.rsqrt(ms + EPS) * g


def _proj_in_kernel(x_ref, mod_ref, ng_ref, win_ref, qng_ref, wuq_ref, kvng_ref, wuk_ref, wuv_ref,
                    qg_ref, kg_ref, bqg_ref, bkg_ref, rc_ref, rs_ref, qpad_ref, kpad_ref,
                    qm_ref, km_ref, vm_ref, gm_ref, qb_ref, kb_ref, vb_ref, gb_ref, kmean_ref):
    gs = ng_ref[...] * (1.0 + mod_ref[0, 1:2, :])
    shift = mod_ref[0, 0:1, :]
    qg = qg_ref[...]
    kg = kg_ref[...]
    qpad = qpad_ref[...]
    kpad = kpad_ref[...]
    bqg = bqg_ref[...]
    bkg = bkg_ref[...]
    lane = lax.broadcasted_iota(jnp.int32, (MOBA_BLOCK, LANES), 1)
    lo_half = lane < HEAD_DIM

    for sub in range(TM // MOBA_BLOCK):
        rows = slice(MOBA_BLOCK * sub, MOBA_BLOCK * (sub + 1))
        hb = (_rms(x_ref[0, rows, :], gs) + shift).astype(BF16)

        def proj(lo, hi, hb=hb):
            return jnp.dot(hb, win_ref[:, lo:hi], preferred_element_type=F32)

        rc = rc_ref[rows, :]
        rs = rs_ref[rows, :]

        cqn = _rms(proj(0, 256), qng_ref[...]).astype(BF16)
        q = jnp.dot(cqn, wuq_ref[...], preferred_element_type=F32)
        for hh in range(H_MLA):
            sl = slice(LANES * hh, LANES * (hh + 1))
            qm_ref[0, rows, sl] = (_mla_head(q[:, sl], qg, rc, rs) + qpad).astype(BF16)

        ckvn = _rms(proj(256, 384), kvng_ref[...]).astype(BF16)
        kn = jnp.dot(ckvn, wuk_ref[...], preferred_element_type=F32)
        kr = proj(384, 512)
        for hh in range(H_MLA):
            sl = slice(LANES * hh, LANES * (hh + 1))
            km_ref[0, rows, sl] = (_mla_head(kn[:, sl] + kr, kg, rc, rs) + kpad).astype(BF16)
        vm_ref[0, rows, :] = jnp.dot(ckvn, wuv_ref[...], preferred_element_type=F32).astype(BF16)
        gm_ref[0, rows, :] = proj(512, 1024).astype(BF16)

        blk_id = pl.program_id(1) * (TM // MOBA_BLOCK) + sub
        onehot = (lane == PEN_LANE + blk_id).astype(F32)
        qbz = proj(1024, 1536)
        kbz = proj(1536, 2048)
        for p in range(N_PAIRS):
            sl = slice(LANES * p, LANES * (p + 1))
            qb_ref[0, rows, sl] = _moba_pair_norm(qbz[:, sl], bqg, lo_half)
            kn_p = _moba_pair_norm(kbz[:, sl], bkg, lo_half)
            kb_ref[0, rows, LANES * 2 * p:LANES * (2 * p + 1)] = jnp.where(
                lo_half, kn_p, onehot).astype(BF16)
            kb_ref[0, rows, LANES * (2 * p + 1):LANES * (2 * p + 2)] = jnp.where(
                lo_half, pltpu.roll(kn_p, HEAD_DIM, 1), onehot).astype(BF16)
            kmean_ref[0, 0, sub:sub + 1, sl] = jnp.sum(kn_p, axis=0, keepdims=True) * (1.0 / MOBA_BLOCK)
        vb_ref[0, rows, :] = proj(2048, 2560).astype(BF16)
        gb_ref[0, rows, :] = proj(2560, 3072).astype(BF16)


def _proj_in_call(layer, x, mod3, ng, win, qng, wuq, kvng, wuk, wuv, qg, kg, bqg, bkg, rope_c, rope_s,
                  qpad, kpad):
    b, s, d = x.shape
    nt = s // TM
    const2 = lambda bi, i: (0, 0)
    tok = lambda w: pl.BlockSpec((1, TM, w), lambda bi, i: (bi, i, 0))
    full = lambda a: pl.BlockSpec(a.shape, const2)
    of_layer = lambda a: pl.BlockSpec((None,) + a.shape[1:], lambda bi, i: (layer, 0, 0))
    rope_spec = pl.BlockSpec((TM, LANES), lambda bi, i: (bi * nt + i, 0))
    out_shape = (
        jax.ShapeDtypeStruct((b, s, H_MLA * LANES), BF16),
        jax.ShapeDtypeStruct((b, s, H_MLA * LANES), BF16),
        jax.ShapeDtypeStruct((b, s, MLA_WIDTH), BF16),
        jax.ShapeDtypeStruct((b, s, MLA_WIDTH), BF16),
        jax.ShapeDtypeStruct((b, s, MOBA_WIDTH), F32),
        jax.ShapeDtypeStruct((b, s, H_MOBA * LANES), BF16),
        jax.ShapeDtypeStruct((b, s, MOBA_WIDTH), BF16),
        jax.ShapeDtypeStruct((b, s, MOBA_WIDTH), BF16),
        jax.ShapeDtypeStruct((b, nt, TM // MOBA_BLOCK, MOBA_WIDTH), F32),
    )
    out_specs = (tok(H_MLA * LANES), tok(H_MLA * LANES), tok(MLA_WIDTH), tok(MLA_WIDTH),
                 tok(MOBA_WIDTH), tok(H_MOBA * LANES), tok(MOBA_WIDTH), tok(MOBA_WIDTH),
                 pl.BlockSpec((1, 1, TM // MOBA_BLOCK, MOBA_WIDTH), lambda bi, i: (bi, i, 0, 0)))
    return pl.pallas_call(
        _proj_in_kernel,
        out_shape=out_shape,
        grid=(b, nt),
        in_specs=[tok(d), pl.BlockSpec((1, 3, d), lambda bi, i: (bi, 0, 0)), full(ng), of_layer(win),
                  full(qng), of_layer(wuq), full(kvng), of_layer(wuk), of_layer(wuv),
                  full(qg), full(kg), full(bqg), full(bkg), rope_spec, rope_spec,
                  full(qpad), full(kpad)],
        out_specs=out_specs,
        compiler_params=pltpu.CompilerParams(dimension_semantics=("arbitrary", "arbitrary"),
                                             vmem_limit_bytes=VMEM_LIMIT),
        name="proj_in",
    )(x, mod3, ng, win, qng, wuq, kvng, wuk, wuv, qg, kg, bqg, bkg, rope_c, rope_s, qpad, kpad)


def _moba_gate_kernel(neg_shift_ref, q_ref, kmean_ref, qx_ref):
    nb = kmean_ref.shape[1]
    nh = H_MOBA
    km = kmean_ref[0]
    q = q_ref[0]
    head_of_lane = jnp.right_shift(lax.broadcasted_iota(jnp.int32, (nb, MOBA_WIDTH), 1),
                                   HEAD_DIM.bit_length() - 1)
    km_heads = jnp.concatenate([jnp.where(head_of_lane == h, km, 0.0) for h in range(nh)], axis=0)
    g = lax.dot_general(km_heads, q, _NT, precision=lax.Precision.HIGHEST,
                        preferred_element_type=F32).reshape(nh, nb, TG)
    blk = lax.broadcasted_iota(jnp.int32, (nh, nb, TG), 1)
    blkf = blk.astype(F32)
    own = pl.program_id(1) * (TG // MOBA_BLOCK) + jnp.right_shift(
        lax.broadcasted_iota(jnp.int32, (nh, nb, TG), 2), MOBA_BLOCK.bit_length() - 1)
    past = blk < own
    rem = jnp.where(past, g, -jnp.inf)
    for k in range(MOBA_TOPK):
        mx = jnp.max(rem, axis=1, keepdims=True)
        idx = jnp.min(jnp.where(rem == mx, blkf, float(nb)), axis=1, keepdims=True)
        rem = jnp.where(blkf == idx, -jnp.inf, rem)
    on = (blk == own) | (past & (rem == -jnp.inf))
    pen = jnp.where(on, neg_shift_ref[0], PEN_OFF).astype(BF16).reshape(nh * nb, TG)
    r = lax.broadcasted_iota(jnp.int32, (nh * nb, nh * LANES), 0)
    to_lane = (lax.broadcasted_iota(jnp.int32, (nh * nb, nh * LANES), 1)
               == jnp.right_shift(r, nb.bit_length() - 1) * LANES + PEN_LANE
               + jnp.bitwise_and(r, nb - 1)).astype(BF16)
    pen_lanes = lax.dot_general(pen, to_lane, (((0,), (0,)), ((), ())),
                                preferred_element_type=F32)
    lo_q = lax.broadcasted_iota(jnp.int32, (TG, LANES), 1) < HEAD_DIM
    qs = q * MOBA_QSCALE
    tiles = []
    for p in range(N_PAIRS):
        qs_p = qs[:, LANES * p:LANES * (p + 1)]
        tiles += [jnp.where(lo_q, qs_p, 0.0), pltpu.roll(jnp.where(lo_q, 0.0, qs_p), HEAD_DIM, 1)]
    qx_ref[0] = (jnp.concatenate(tiles, axis=1) + pen_lanes).astype(BF16)


def _moba_gate_call(neg_shift, qb, kmean):
    b, s, _ = qb.shape
    nb = kmean.shape[1]
    return pl.pallas_call(
        _moba_gate_kernel,
        out_shape=jax.ShapeDtypeStruct((b, s, H_MOBA * LANES), BF16),
        grid=(b, s // TG),
        in_specs=[pl.BlockSpec(memory_space=pltpu.SMEM),
                  pl.BlockSpec((1, TG, MOBA_WIDTH), lambda bi, i: (bi, i, 0)),
                  pl.BlockSpec((1, nb, MOBA_WIDTH), lambda bi, i: (bi, 0, 0))],
        out_specs=pl.BlockSpec((1, TG, H_MOBA * LANES), lambda bi, i: (bi, i, 0)),
        compiler_params=pltpu.CompilerParams(dimension_semantics=("arbitrary", "arbitrary"),
                                             vmem_limit_bytes=VMEM_LIMIT),
        name="moba_gate",
    )(neg_shift, qb, kmean)


def _flash_update(hh, s, v_ext, m_sc, acc_sc, first, keep, fixed_shift):
    if fixed_shift:
        acc_sc[hh] = acc_sc[hh] * keep + jnp.dot(jnp.exp2(s).astype(BF16), v_ext,
                                                 preferred_element_type=F32)
        return
    m_prev = jnp.where(first == 1, NEG_BIG, m_sc[hh])
    m_new = jnp.maximum(m_prev, jnp.max(s, axis=-1, keepdims=True))
    alpha = jnp.exp2(m_prev - m_new)
    p = jnp.concatenate([jnp.exp2(s[:, LANES * c:LANES * (c + 1)] - m_new)
                         for c in range(s.shape[1] // LANES)], axis=1).astype(BF16)
    pv = jnp.dot(p, v_ext, preferred_element_type=F32)
    acc_sc[hh] = jnp.concatenate([alpha, alpha], axis=1) * acc_sc[hh] + pv
    m_sc[hh] = m_new


def _with_ones(v):
    return jnp.concatenate([v, jnp.ones(v.shape, v.dtype)], axis=1)


def _flash_init(m_sc, acc_sc):
    m_sc[...] = jnp.full(m_sc.shape, NEG_BIG, F32)
    acc_sc[...] = jnp.zeros(acc_sc.shape, F32)


def _flash_finish(o_ref, rows, acc_sc):
    lo_half = lax.broadcasted_iota(jnp.int32, (acc_sc.shape[1], LANES), 1) < HEAD_DIM
    o = jnp.where(lo_half, acc_sc[0, :, :LANES] / acc_sc[0, :, LANES:],
                  acc_sc[1, :, :LANES] / acc_sc[1, :, LANES:])
    o_ref[0, rows, :] = o.astype(o_ref.dtype)


def _causal_steps(n_tiles):
    pairs = [(qi, u) for qi in range(n_tiles) for u in range(qi + 1)]
    return (jnp.asarray([p[0] for p in pairs], jnp.int32), jnp.asarray([p[1] for p in pairs], jnp.int32),
            jnp.asarray([float(p[1] > 0) for p in pairs], F32))


def _pipelined(n_steps, produce, consume, bufs):
    produce(0, bufs[0])
    iters = (n_steps - 1) // STEPS_PER_ITER

    def body(it, carry):
        for j in range(STEPS_PER_ITER):
            n = STEPS_PER_ITER * it + j
            produce(n + 1, bufs[(j + 1) % 2])
            consume(n, bufs[j % 2])
        return carry

    lax.fori_loop(0, iters, body, 0)
    for n in range(STEPS_PER_ITER * iters, n_steps):
        if n + 1 < n_steps:
            produce(n + 1, bufs[(n + 1) % 2])
        consume(n, bufs[n % 2])


def _mla_kernel(qtab_ref, utab_ref, keep_ref, q_ref, k_ref, v_ref, o_ref,
                m_sc, acc_sc, sa_sc, sb_sc, mask_sc, *, fixed_shift, n_steps):
    t = TQ_MLA
    _flash_init(m_sc, acc_sc)
    row = lax.broadcasted_iota(jnp.int32, (t, t), 0)
    col = lax.broadcasted_iota(jnp.int32, (t, t), 1)
    mask_sc[0] = jnp.zeros((t, t), F32)
    mask_sc[1] = jnp.where(col <= row, 0.0, NEG_BIG)

    def produce(n, buf):
        qi, u = qtab_ref[n], utab_ref[n]
        qrows = pl.ds(pl.multiple_of(qi * t, t), t)
        krows = pl.ds(pl.multiple_of(u * t, t), t)
        diag = (u == qi).astype(jnp.int32)
        for hh in range(2):
            sl = slice(LANES * hh, LANES * (hh + 1))
            buf[hh] = lax.dot_general(q_ref[0, qrows, sl], k_ref[0, krows, sl], _NT,
                                      preferred_element_type=F32) + mask_sc[diag]

    def consume(n, buf):
        qi, u = qtab_ref[n], utab_ref[n]
        v = _with_ones(v_ref[0, pl.ds(pl.multiple_of(u * t, t), t), :])
        first = (u == 0).astype(jnp.int32)
        for hh in range(2):
            _flash_update(hh, buf[hh], v, m_sc, acc_sc, first, keep_ref[n], fixed_shift)
        _flash_finish(o_ref, pl.ds(pl.multiple_of(qi * t, t), t), acc_sc)

    _pipelined(n_steps, produce, consume, (sa_sc, sb_sc))


def _mla_call(qm, km, vm, fixed_shift):
    b, s, _ = qm.shape
    t = TQ_MLA
    qtab, utab, keep = _causal_steps(s // t)
    seq = lambda w: pl.BlockSpec((1, s, w), lambda p, bi, *_: (bi, 0, p))
    return pl.pallas_call(
        functools.partial(_mla_kernel, fixed_shift=fixed_shift, n_steps=int(qtab.shape[0])),
        out_shape=jax.ShapeDtypeStruct((b, s, MLA_WIDTH), BF16),
        grid_spec=pltpu.PrefetchScalarGridSpec(
            num_scalar_prefetch=2,
            grid=(N_PAIRS, b),
            in_specs=[pl.BlockSpec(memory_space=pltpu.SMEM),
                      seq(2 * LANES), seq(2 * LANES), seq(LANES)],
            out_specs=seq(LANES),
            scratch_shapes=[pltpu.VMEM((2, t, LANES), F32), pltpu.VMEM((2, t, 2 * LANES), F32),
                            pltpu.VMEM((2, t, t), F32), pltpu.VMEM((2, t, t), F32),
                            pltpu.VMEM((2, t, t), F32)],
        ),
        compiler_params=pltpu.CompilerParams(dimension_semantics=("arbitrary", "arbitrary"),
                                             vmem_limit_bytes=VMEM_LIMIT),
        name="mla_attention",
    )(qtab, utab, keep, qm, km, vm)


def _moba_kernel(qtab_ref, utab_ref, keep_ref, qx_ref, k_ref, v_ref, bt_ref, o_ref,
                 m_sc, acc_sc, sa_sc, sb_sc, *, fixed_shift, n_steps):
    t = MOBA_BLOCK
    w = MOBA_TILE
    nsub = w // t
    _flash_init(m_sc, acc_sc)

    def produce(n, buf):
        qi, u = qtab_ref[n], utab_ref[n]
        qrows = pl.ds(pl.multiple_of(qi * w, w), w)
        krows = pl.ds(pl.multiple_of(u * w, w), w)
        for hh in range(2):
            sl = slice(LANES * hh, LANES * (hh + 1))
            s = lax.dot_general(qx_ref[0, qrows, sl], k_ref[0, krows, sl], _NT,
                                preferred_element_type=F32)
            for qs in range(nsub):
                for ks in range(nsub):
                    d = jnp.clip(nsub * (qi - u) + (qs - ks), 0, bt_ref.shape[2] - 1)
                    rows, cols = slice(t * qs, t * (qs + 1)), slice(t * ks, t * (ks + 1))
                    buf[hh, rows, cols] = s[rows, cols] + bt_ref[0, hh, d]

    def consume(n, buf):
        qi, u = qtab_ref[n], utab_ref[n]
        v = _with_ones(v_ref[0, pl.ds(pl.multiple_of(u * w, w), w), :])
        first = (u == 0).astype(jnp.int32)
        for hh in range(2):
            _flash_update(hh, buf[hh], v, m_sc, acc_sc, first, keep_ref[n], fixed_shift)
        _flash_finish(o_ref, pl.ds(pl.multiple_of(qi * w, w), w), acc_sc)

    _pipelined(n_steps, produce, consume, (sa_sc, sb_sc))


def _moba_call(qx, kb, vb, bias_tiles, fixed_shift):
    b, s, _ = vb.shape
    t = MOBA_TILE
    n_tiles = bias_tiles.shape[2]
    qtab, utab, keep = _causal_steps(s // t)
    seq = lambda w: pl.BlockSpec((1, s, w), lambda p, bi, *_: (bi, 0, p))
    return pl.pallas_call(
        functools.partial(_moba_kernel, fixed_shift=fixed_shift, n_steps=int(qtab.shape[0])),
        out_shape=jax.ShapeDtypeStruct((b, s, MOBA_WIDTH), BF16),
        grid_spec=pltpu.PrefetchScalarGridSpec(
            num_scalar_prefetch=2,
            grid=(N_PAIRS, b),
            in_specs=[pl.BlockSpec(memory_space=pltpu.SMEM),
                      seq(2 * LANES), seq(2 * LANES), seq(LANES),
                      pl.BlockSpec((1, 2, n_tiles, MOBA_BLOCK, MOBA_BLOCK),
                                   lambda p, bi, *_: (p, 0, 0, 0, 0))],
            out_specs=seq(LANES),
            scratch_shapes=[pltpu.VMEM((2, t, LANES), F32), pltpu.VMEM((2, t, 2 * LANES), F32),
                            pltpu.VMEM((2, t, t), F32), pltpu.VMEM((2, t, t), F32)],
        ),
        compiler_params=pltpu.CompilerParams(dimension_semantics=("arbitrary", "arbitrary"),
                                             vmem_limit_bytes=VMEM_LIMIT),
        name="moba_attention",
    )(qtab, utab, keep, qx, kb, vb, bias_tiles)


def _proj_out_kernel(om_ref, gm_ref, ob_ref, gb_ref, w_ref, x_ref, mod_ref, o_ref):
    am = (_silu(gm_ref[0].astype(F32)) * om_ref[0].astype(F32)).astype(BF16)
    ab = (_silu(gb_ref[0].astype(F32)) * ob_ref[0].astype(F32)).astype(BF16)
    y = (jnp.dot(am, w_ref[0:MLA_WIDTH, :], preferred_element_type=F32)
         + jnp.dot(ab, w_ref[MLA_WIDTH:, :], preferred_element_type=F32))
    o_ref[0] = x_ref[0] + mod_ref[0, 2:3, :] * y


def _proj_out_call(layer, o_mla, g_mla, o_moba, g_moba, w_out, x, mod3):
    b, s, d = x.shape
    tok = lambda w: pl.BlockSpec((1, TM_OUT, w), lambda bi, i: (bi, i, 0))
    return pl.pallas_call(
        _proj_out_kernel,
        out_shape=jax.ShapeDtypeStruct((b, s, d), F32),
        grid=(b, s // TM_OUT),
        in_specs=[tok(MLA_WIDTH), tok(MLA_WIDTH), tok(MOBA_WIDTH), tok(MOBA_WIDTH),
                  pl.BlockSpec((None,) + w_out.shape[1:], lambda bi, i: (layer, 0, 0)),
                  tok(d), pl.BlockSpec((1, 3, d), lambda bi, i: (bi, 0, 0))],
        out_specs=tok(d),
        compiler_params=pltpu.CompilerParams(dimension_semantics=("arbitrary", "arbitrary"),
                                             vmem_limit_bytes=VMEM_LIMIT),
        name="proj_out",
    )(o_mla, g_mla, o_moba, g_moba, w_out, x, mod3)


def _to_head_lanes(a):
    a = jnp.concatenate([a, jnp.zeros(a.shape[:-1] + (1,), a.dtype)], axis=-1)
    return a[..., _LANE_SRC]


def _layer_weights(w_in, w_uq, w_ukv, q_g, k_g, bq_g, bk_g):
    o0 = Q_LORA
    o1 = o0 + KV_LORA
    o2 = o1 + ROPE_DIM
    k_rope = w_in[:, o1:o2]
    kr128 = jnp.zeros((D_MODEL, LANES), w_in.dtype)
    kr128 = kr128.at[:, 0:HALF_ROPE].set(k_rope[:, :HALF_ROPE])
    kr128 = kr128.at[:, LANES // 2:LANES // 2 + HALF_ROPE].set(k_rope[:, HALF_ROPE:])
    win = jnp.concatenate([w_in[:, :o1], kr128, w_in[:, o2:]], axis=1).astype(BF16)
    wuq = _to_head_lanes(w_uq.reshape(Q_LORA, H_MLA, QK_DIM)).reshape(Q_LORA, H_MLA * LANES)
    ukv = w_ukv.reshape(KV_LORA, H_MLA, NOPE_DIM + HEAD_DIM)
    k_nope = jnp.concatenate([ukv[..., :NOPE_DIM], jnp.zeros((KV_LORA, H_MLA, ROPE_DIM), ukv.dtype)], -1)
    wuk = _to_head_lanes(k_nope).reshape(KV_LORA, H_MLA * LANES)
    wuv = ukv[..., NOPE_DIM:].reshape(KV_LORA, MLA_WIDTH)
    pair = lambda g: jnp.concatenate([g, g]).reshape(1, LANES)
    return (win, wuq.astype(BF16), wuk.astype(BF16), wuv.astype(BF16),
            _to_head_lanes(q_g * MLA_QSCALE).reshape(1, LANES), _to_head_lanes(k_g).reshape(1, LANES),
            pair(bq_g), pair(bk_g))


def kernel(x, c, positions, norm_g, w_ada, b_ada, w_in, mla_q_norm_g, mla_w_uq, mla_kv_norm_g,
           mla_w_ukv, mla_q_g, mla_k_g, moba_q_g, moba_k_g, w_out, rel_bias):
    b, s, d = x.shape
    depth = w_in.shape[0]
    nb = s // MOBA_BLOCK
    assert d == D_MODEL and s % TM == 0 and s % TQ_MLA == 0 and TM % MOBA_BLOCK == 0 and b <= 8
    assert s % TG == 0 and TG % MOBA_BLOCK == 0 and s % MOBA_TILE == 0
    assert nb <= LANES // 4 and nb & (nb - 1) == 0
    n_tiles = _check_far_tiles(s)

    c8 = jnp.pad(c, ((0, 8 - b), (0, 0)))
    mod = _mod_call(c8, w_ada, b_ada.reshape(depth, 1, 3 * d))[:, :b].reshape(depth, b, 3, d)

    inv_freq = ROPE_THETA ** (-jnp.arange(0, HALF_ROPE, dtype=F32) / HALF_ROPE)
    freq = jnp.zeros((LANES,), F32).at[0:HALF_ROPE].set(inv_freq)
    freq = freq.at[LANES // 2:LANES // 2 + HALF_ROPE].set(inv_freq).reshape(1, LANES)
    sign = jnp.zeros((LANES,), F32).at[0:HALF_ROPE].set(-1.0)
    sign = sign.at[LANES // 2:LANES // 2 + HALF_ROPE].set(1.0).reshape(1, LANES)
    rope_c, rope_s = _rope_call(positions.reshape(b * s, 1), freq, sign)

    bkt = _rel_bucket(jnp.arange(s, dtype=jnp.int32))
    thr = jnp.sum(bkt[None, :] < jnp.arange(N_BUCKETS, dtype=jnp.int32)[:, None], axis=1).astype(jnp.int32)
    ttab, vtab = _far_tables(thr, rel_bias, n_tiles)
    bias_tiles = _bias_tile_call(thr, ttab, rel_bias.astype(F32).reshape(-1), vtab, n_tiles)
    bias_tiles = bias_tiles.reshape(N_PAIRS, 2, n_tiles, MOBA_BLOCK, MOBA_BLOCK)
    kpad = jnp.zeros((1, LANES), F32).at[0, SHIFT_LANE].set(1.0)
    amax = lambda a: jnp.max(jnp.abs(a.astype(F32)))

    def attention(fixed_shift, qm, km, vm, qx, kb, vb):
        return (_mla_call(qm, km, vm, fixed_shift),
                _moba_call(qx, kb, vb, bias_tiles, fixed_shift))

    layer_weights = jax.vmap(_layer_weights)(w_in, mla_w_uq, mla_w_ukv, mla_q_g, mla_k_g,
                                             moba_q_g, moba_k_g)
    w_out_bf16 = w_out.astype(BF16)
    for l in range(depth):
        win, wuq, wuk, wuv = layer_weights[:4]
        qg, kg, bqg, bkg = (w[l] for w in layer_weights[4:])
        bound_mla = QK_DIM * amax(mla_q_g[l]) * amax(mla_k_g[l]) * MLA_QSCALE * SHIFT_SLACK
        bound_moba = (HEAD_DIM * amax(moba_q_g[l]) * amax(moba_k_g[l]) * MOBA_QSCALE
                      + amax(rel_bias) * LOG2E) * SHIFT_SLACK
        fixed_ok = (bound_mla <= MAX_SHIFT) & (bound_moba <= MAX_SHIFT)
        qpad = jnp.zeros((1, LANES), F32).at[0, SHIFT_LANE].set(jnp.where(fixed_ok, -bound_mla, 0.0))
        qm, km, vm, gm, qb, kb, vb, gb, kmean = _proj_in_call(
            l, x, mod[l], norm_g[l].reshape(1, d), win, mla_q_norm_g[l].reshape(1, Q_LORA), wuq,
            mla_kv_norm_g[l].reshape(1, KV_LORA), wuk, wuv, qg, kg, bqg, bkg, rope_c, rope_s,
            qpad, kpad)
        qx = _moba_gate_call(jnp.where(fixed_ok, -bound_moba, 0.0).reshape(1), qb,
                             kmean.reshape(b, nb, MOBA_WIDTH))
        o_mla, o_moba = lax.cond(
            fixed_ok, functools.partial(attention, True), functools.partial(attention, False),
            qm, km, vm, qx, kb, vb)
        x = _proj_out_call(l, o_mla, gm, o_moba, gb, w_out_bf16, x, mod[l])
    return x
```

```python
import functools
import math

import numpy as np
import jax
import jax.numpy as jnp
from jax import lax
from jax.experimental import pallas as pl
from jax.experimental.pallas import tpu as pltpu

F32 = jnp.float32
BF16 = jnp.bfloat16

D_MODEL = 1024
DEPTH = 2
HEAD_DIM = 64
H_MLA = 8
H_MOBA = 8
MLA_WIDTH = H_MLA * HEAD_DIM
MOBA_WIDTH = H_MOBA * HEAD_DIM
Q_LORA = 256
KV_LORA = 128
NOPE_DIM = 64
ROPE_DIM = 32
QK_DIM = NOPE_DIM + ROPE_DIM
ROPE_THETA = 10000.0
MOBA_BLOCK = 256
MOBA_TOPK = 3
N_BUCKETS = 32
REL_MAX_DIST = 4096
EPS = 1e-6

LANES = 128
HALF_ROPE = ROPE_DIM // 2
N_PAIRS = H_MLA // 2
LOG2E = math.log2(math.e)
MLA_QSCALE = QK_DIM ** -0.5 * LOG2E
MOBA_QSCALE = HEAD_DIM ** -0.5 * LOG2E
NEG_BIG = -1e30
FAR_THRESHOLDS = 2
NEAR_TILES = 3
NEVER = 1 << 20
PEN_OFF = -2.0 ** 127
SHIFT_LANE = QK_DIM
PEN_LANE = HEAD_DIM
MAX_SHIFT = 48.0
SHIFT_SLACK = 1.02

TM = 512
TM_OUT = 1024
TG = 1024
TQ_MLA = 512
MOBA_TILE = 2 * MOBA_BLOCK
STEPS_PER_ITER = 22
VMEM_LIMIT = 56 * 1024 * 1024

_NT = (((1,), (1,)), ((), ()))


def _head_lane_source():
    src = np.full((LANES,), QK_DIM, np.int32)
    src[0:16] = NOPE_DIM + np.arange(16)
    src[16:64] = np.arange(48)
    src[64:80] = NOPE_DIM + HALF_ROPE + np.arange(16)
    src[80:96] = 48 + np.arange(16)
    return src


_LANE_SRC = _head_lane_source()


def _bucket_np(n, dtype):
    n = np.asarray(n)
    max_exact = N_BUCKETS // 2
    nf = np.maximum(n, 1).astype(dtype)
    large = max_exact + (np.log(nf / dtype(max_exact)) / dtype(math.log(REL_MAX_DIST / max_exact))
                         * dtype(N_BUCKETS - max_exact)).astype(np.int32)
    large = np.minimum(large, N_BUCKETS - 1)
    return np.where(n < max_exact, n, large)


def _check_far_tiles(seq):
    n = np.arange(seq + 2)
    nb = seq // MOBA_BLOCK
    n_tiles = 0
    for dtype in (np.float32, np.float64):
        b = _bucket_np(n, dtype)
        chg = np.concatenate([[0], (b[1:] != b[:-1]).astype(np.int64)])
        for d in range(NEAR_TILES, nb):
            lo, hi = MOBA_BLOCK * (d - 1) + 1, MOBA_BLOCK * (d + 1) - 1
            assert chg[lo - 1:hi + 1].sum() <= FAR_THRESHOLDS, (d, dtype)
        saturated = [d for d in range(NEAR_TILES, nb)
                     if b[MOBA_BLOCK * (d - 1) - 1] == N_BUCKETS - 1]
        n_tiles = max(n_tiles, (saturated[0] if saturated else nb - 1) + 1)
    return min(n_tiles, nb)


def _rel_bucket(dist):
    n = jnp.maximum(dist, 0)
    max_exact = N_BUCKETS // 2
    nf = jnp.maximum(n, 1).astype(F32)
    large = max_exact + (jnp.log(nf / max_exact) / math.log(REL_MAX_DIST / max_exact)
                         * (N_BUCKETS - max_exact)).astype(jnp.int32)
    large = jnp.minimum(large, N_BUCKETS - 1)
    return jnp.where(n < max_exact, n, large)


def _rms(x, g):
    return x * lax.rsqrt(jnp.mean(x * x, axis=-1, keepdims=True) + EPS) * g


def _silu(x):
    return x * jax.nn.sigmoid(x)


def _mod_kernel(c_ref, w_ref, b_ref, o_ref):
    c = c_ref[...]
    o_ref[0] = jnp.dot(_silu(c), w_ref[0], precision=lax.Precision.HIGHEST,
                       preferred_element_type=F32) + b_ref[0]


def _mod_call(c8, w_ada, b_ada):
    depth, d, d3 = w_ada.shape
    nchunk = d3 // d
    return pl.pallas_call(
        _mod_kernel,
        out_shape=jax.ShapeDtypeStruct((depth, 8, d3), F32),
        grid=(depth, nchunk),
        in_specs=[pl.BlockSpec((8, d), lambda l, j: (0, 0)),
                  pl.BlockSpec((1, d, d), lambda l, j: (l, 0, j)),
                  pl.BlockSpec((1, 1, d), lambda l, j: (l, 0, j))],
        out_specs=pl.BlockSpec((1, 8, d), lambda l, j: (l, 0, j)),
        compiler_params=pltpu.CompilerParams(dimension_semantics=("arbitrary", "arbitrary"),
                                             vmem_limit_bytes=VMEM_LIMIT),
        name="adaln_mod",
    )(c8, w_ada, b_ada)


def _rope_kernel(pos_ref, freq_ref, sign_ref, c_ref, s_ref):
    ang = pos_ref[...].astype(F32) * freq_ref[...]
    c_ref[...] = jnp.cos(ang)
    s_ref[...] = jnp.sin(ang) * sign_ref[...]


def _rope_call(pos_col, freq, sign):
    n = pos_col.shape[0]
    tr = 1024
    return pl.pallas_call(
        _rope_kernel,
        out_shape=(jax.ShapeDtypeStruct((n, LANES), F32), jax.ShapeDtypeStruct((n, LANES), F32)),
        grid=(n // tr,),
        in_specs=[pl.BlockSpec((tr, 1), lambda i: (i, 0)),
                  pl.BlockSpec((1, LANES), lambda i: (0, 0)),
                  pl.BlockSpec((1, LANES), lambda i: (0, 0))],
        out_specs=(pl.BlockSpec((tr, LANES), lambda i: (i, 0)),
                   pl.BlockSpec((tr, LANES), lambda i: (i, 0))),
        compiler_params=pltpu.CompilerParams(dimension_semantics=("arbitrary",)),
        name="rope_tables",
    )(pos_col, freq, sign)


def _bias_tile_kernel(thr_ref, ttab_ref, rb_ref, vtab_ref, o_ref):
    h = pl.program_id(0)
    nb = o_ref.shape[1]
    shape = (MOBA_BLOCK, MOBA_BLOCK)
    rc = lax.broadcasted_iota(jnp.int32, shape, 0) - lax.broadcasted_iota(jnp.int32, shape, 1)

    for d in range(NEAR_TILES):
        dist = rc + d * MOBA_BLOCK
        val = jnp.full(shape, rb_ref[h], F32)
        for b in range(1, N_BUCKETS):
            val = jnp.where(dist >= thr_ref[b], rb_ref[b * H_MOBA + h], val)
        o_ref[0, d] = jnp.where(dist >= 0, val * LOG2E, NEG_BIG)

    def far(d, carry):
        base = (d * H_MOBA + h) * (FAR_THRESHOLDS + 1)
        val = jnp.full(shape, vtab_ref[base], F32)
        for k in range(FAR_THRESHOLDS):
            val = jnp.where(rc >= ttab_ref[d * FAR_THRESHOLDS + k], vtab_ref[base + k + 1], val)
        o_ref[0, d] = val
        return carry

    lax.fori_loop(NEAR_TILES, nb, far, 0)


def _bias_tile_call(thr, ttab, rb_flat, vtab, nb):
    smem = pl.BlockSpec(memory_space=pltpu.SMEM)
    return pl.pallas_call(
        _bias_tile_kernel,
        out_shape=jax.ShapeDtypeStruct((H_MOBA, nb, MOBA_BLOCK, MOBA_BLOCK), F32),
        grid_spec=pltpu.PrefetchScalarGridSpec(
            num_scalar_prefetch=2,
            grid=(H_MOBA,),
            in_specs=[smem, smem],
            out_specs=pl.BlockSpec((1, nb, MOBA_BLOCK, MOBA_BLOCK), lambda h, *_: (h, 0, 0, 0)),
        ),
        compiler_params=pltpu.CompilerParams(dimension_semantics=("arbitrary",),
                                             vmem_limit_bytes=VMEM_LIMIT),
        name="moba_bias_tiles",
    )(thr, ttab, rb_flat, vtab)


def _far_tables(thr, rel_bias, n_tiles):
    edge = thr[1:]
    bucket = jnp.arange(1, N_BUCKETS, dtype=jnp.int32)
    keep = jnp.concatenate([edge[1:] > edge[:-1], jnp.ones((1,), bool)])
    d = jnp.arange(n_tiles, dtype=jnp.int32)[:, None]
    lo = jnp.maximum(MOBA_BLOCK * (d - 1) + 1, 0)
    hi = MOBA_BLOCK * (d + 1) - 1
    inside = keep[None, :] & (edge[None, :] > lo) & (edge[None, :] <= hi)
    cum = jnp.cumsum(inside.astype(jnp.int32), axis=1)
    idx = [jnp.sum((edge[None, :] <= lo).astype(jnp.int32), axis=1)]
    tt = []
    for k in range(1, FAR_THRESHOLDS + 1):
        sel = inside & (cum == k)
        tt.append(jnp.min(jnp.where(sel, edge[None, :], NEVER), axis=1) - MOBA_BLOCK * d[:, 0])
        idx.append(jnp.min(jnp.where(sel, bucket[None, :], N_BUCKETS - 1), axis=1))
    ttab = jnp.stack(tt, axis=1).reshape(-1).astype(jnp.int32)
    vals = rel_bias.astype(F32)[jnp.stack(idx, axis=1)] * LOG2E
    vtab = vals.transpose(0, 2, 1).reshape(-1)
    return ttab, vtab


def _mla_head(xh, g, rc, rs):
    ms = jnp.sum(xh * xh, axis=-1, keepdims=True) * (1.0 / QK_DIM)
    xn = xh * lax.rsqrt(ms + EPS) * g
    return xn * rc + pltpu.roll(xn, LANES // 2, 1) * rs


def _moba_pair_norm(x, g, lo_half):
    x2 = x * x
    s_lo = jnp.sum(jnp.where(lo_half, x2, 0.0), axis=-1, keepdims=True)
    s_hi = jnp.sum(jnp.where(lo_half, 0.0, x2), axis=-1, keepdims=True)
    ms = jnp.where(lo_half, s_lo, s_hi) * (1.0 / HEAD_DIM)
    return x * lax.rsqrt(ms + EPS) * g


def _proj_in_kernel(x_ref, mod_ref, ng_ref, win_ref, qng_ref, wuq_ref, kvng_ref, wuk_ref, wuv_ref,
                    qg_ref, kg_ref, bqg_ref, bkg_ref, rc_ref, rs_ref, qpad_ref, kpad_ref,
                    qm_ref, km_ref, vm_ref, gm_ref, qb_ref, kb_ref, vb_ref, gb_ref, kmean_ref):
    gs = ng_ref[...] * (1.0 + mod_ref[0, 1:2, :])
    shift = mod_ref[0, 0:1, :]
    qg = qg_ref[...]
    kg = kg_ref[...]
    qpad = qpad_ref[...]
    kpad = kpad_ref[...]
    bqg = bqg_ref[...]
    bkg = bkg_ref[...]
    lane = lax.broadcasted_iota(jnp.int32, (MOBA_BLOCK, LANES), 1)
    lo_half = lane < HEAD_DIM

    for sub in range(TM // MOBA_BLOCK):
        rows = slice(MOBA_BLOCK * sub, MOBA_BLOCK * (sub + 1))
        hb = (_rms(x_ref[0, rows, :], gs) + shift).astype(BF16)

        def proj(lo, hi, hb=hb):
            return jnp.dot(hb, win_ref[:, lo:hi], preferred_element_type=F32)

        rc = rc_ref[rows, :]
        rs = rs_ref[rows, :]

        cqn = _rms(proj(0, 256), qng_ref[...]).astype(BF16)
        q = jnp.dot(cqn, wuq_ref[...], preferred_element_type=F32)
        for hh in range(H_MLA):
            sl = slice(LANES * hh, LANES * (hh + 1))
            qm_ref[0, rows, sl] = (_mla_head(q[:, sl], qg, rc, rs) + qpad).astype(BF16)

        ckvn = _rms(proj(256, 384), kvng_ref[...]).astype(BF16)
        kn = jnp.dot(ckvn, wuk_ref[...], preferred_element_type=F32)
        kr = proj(384, 512)
        for hh in range(H_MLA):
            sl = slice(LANES * hh, LANES * (hh + 1))
            km_ref[0, rows, sl] = (_mla_head(kn[:, sl] + kr, kg, rc, rs) + kpad).astype(BF16)
        vm_ref[0, rows, :] = jnp.dot(ckvn, wuv_ref[...], preferred_element_type=F32).astype(BF16)
        gm_ref[0, rows, :] = proj(512, 1024).astype(BF16)

        blk_id = pl.program_id(1) * (TM // MOBA_BLOCK) + sub
        onehot = (lane == PEN_LANE + blk_id).astype(F32)
        qbz = proj(1024, 1536)
        kbz = proj(1536, 2048)
        for p in range(N_PAIRS):
            sl = slice(LANES * p, LANES * (p + 1))
            qb_ref[0, rows, sl] = _moba_pair_norm(qbz[:, sl], bqg, lo_half)
            kn_p = _moba_pair_norm(kbz[:, sl], bkg, lo_half)
            kb_ref[0, rows, LANES * 2 * p:LANES * (2 * p + 1)] = jnp.where(
                lo_half, kn_p, onehot).astype(BF16)
            kb_ref[0, rows, LANES * (2 * p + 1):LANES * (2 * p + 2)] = jnp.where(
                lo_half, pltpu.roll(kn_p, HEAD_DIM, 1), onehot).astype(BF16)
            kmean_ref[0, 0, sub:sub + 1, sl] = jnp.sum(kn_p, axis=0, keepdims=True) * (1.0 / MOBA_BLOCK)
        vb_ref[0, rows, :] = proj(2048, 2560).astype(BF16)
        gb_ref[0, rows, :] = proj(2560, 3072).astype(BF16)


def _proj_in_call(layer, x, mod3, ng, win, qng, wuq, kvng, wuk, wuv, qg, kg, bqg, bkg, rope_c, rope_s,
                  qpad, kpad):
    b, s, d = x.shape
    nt = s // TM
    const2 = lambda bi, i: (0, 0)
    tok = lambda w: pl.BlockSpec((1, TM, w), lambda bi, i: (bi, i, 0))
    full = lambda a: pl.BlockSpec(a.shape, const2)
    of_layer = lambda a: pl.BlockSpec((None,) + a.shape[1:], lambda bi, i: (layer, 0, 0))
    rope_spec = pl.BlockSpec((TM, LANES), lambda bi, i: (bi * nt + i, 0))
    out_shape = (
        jax.ShapeDtypeStruct((b, s, H_MLA * LANES), BF16),
        jax.ShapeDtypeStruct((b, s, H_MLA * LANES), BF16),
        jax.ShapeDtypeStruct((b, s, MLA_WIDTH), BF16),
        jax.ShapeDtypeStruct((b, s, MLA_WIDTH), BF16),
        jax.ShapeDtypeStruct((b, s, MOBA_WIDTH), F32),
        jax.ShapeDtypeStruct((b, s, H_MOBA * LANES), BF16),
        jax.ShapeDtypeStruct((b, s, MOBA_WIDTH), BF16),
        jax.ShapeDtypeStruct((b, s, MOBA_WIDTH), BF16),
        jax.ShapeDtypeStruct((b, nt, TM // MOBA_BLOCK, MOBA_WIDTH), F32),
    )
    out_specs = (tok(H_MLA * LANES), tok(H_MLA * LANES), tok(MLA_WIDTH), tok(MLA_WIDTH),
                 tok(MOBA_WIDTH), tok(H_MOBA * LANES), tok(MOBA_WIDTH), tok(MOBA_WIDTH),
                 pl.BlockSpec((1, 1, TM // MOBA_BLOCK, MOBA_WIDTH), lambda bi, i: (bi, i, 0, 0)))
    return pl.pallas_call(
        _proj_in_kernel,
        out_shape=out_shape,
        grid=(b, nt),
        in_specs=[tok(d), pl.BlockSpec((1, 3, d), lambda bi, i: (bi, 0, 0)), full(ng), of_layer(win),
                  full(qng), of_layer(wuq), full(kvng), of_layer(wuk), of_layer(wuv),
                  full(qg), full(kg), full(bqg), full(bkg), rope_spec, rope_spec,
                  full(qpad), full(kpad)],
        out_specs=out_specs,
        compiler_params=pltpu.CompilerParams(dimension_semantics=("arbitrary", "arbitrary"),
                                             vmem_limit_bytes=VMEM_LIMIT),
        name="proj_in",
    )(x, mod3, ng, win, qng, wuq, kvng, wuk, wuv, qg, kg, bqg, bkg, rope_c, rope_s, qpad, kpad)


def _moba_gate_kernel(neg_shift_ref, q_ref, kmean_ref, qx_ref):
    nb = kmean_ref.shape[1]
    nh = H_MOBA
    km = kmean_ref[0]
    q = q_ref[0]
    head_of_lane = jnp.right_shift(lax.broadcasted_iota(jnp.int32, (nb, MOBA_WIDTH), 1),
                                   HEAD_DIM.bit_length() - 1)
    km_heads = jnp.concatenate([jnp.where(head_of_lane == h, km, 0.0) for h in range(nh)], axis=0)
    g = lax.dot_general(km_heads, q, _NT, precision=lax.Precision.HIGHEST,
                        preferred_element_type=F32).reshape(nh, nb, TG)
    blk = lax.broadcasted_iota(jnp.int32, (nh, nb, TG), 1)
    blkf = blk.astype(F32)
    own = pl.program_id(1) * (TG // MOBA_BLOCK) + jnp.right_shift(
        lax.broadcasted_iota(jnp.int32, (nh, nb, TG), 2), MOBA_BLOCK.bit_length() - 1)
    past = blk < own
    rem = jnp.where(past, g, -jnp.inf)
    for k in range(MOBA_TOPK):
        mx = jnp.max(rem, axis=1, keepdims=True)
        idx = jnp.min(jnp.where(rem == mx, blkf, float(nb)), axis=1, keepdims=True)
        rem = jnp.where(blkf == idx, -jnp.inf, rem)
    on = (blk == own) | (past & (rem == -jnp.inf))
    pen = jnp.where(on, neg_shift_ref[0], PEN_OFF).astype(BF16).reshape(nh * nb, TG)
    r = lax.broadcasted_iota(jnp.int32, (nh * nb, nh * LANES), 0)
    to_lane = (lax.broadcasted_iota(jnp.int32, (nh * nb, nh * LANES), 1)
               == jnp.right_shift(r, nb.bit_length() - 1) * LANES + PEN_LANE
               + jnp.bitwise_and(r, nb - 1)).astype(BF16)
    pen_lanes = lax.dot_general(pen, to_lane, (((0,), (0,)), ((), ())),
                                preferred_element_type=F32)
    lo_q = lax.broadcasted_iota(jnp.int32, (TG, LANES), 1) < HEAD_DIM
    qs = q * MOBA_QSCALE
    tiles = []
    for p in range(N_PAIRS):
        qs_p = qs[:, LANES * p:LANES * (p + 1)]
        tiles += [jnp.where(lo_q, qs_p, 0.0), pltpu.roll(jnp.where(lo_q, 0.0, qs_p), HEAD_DIM, 1)]
    qx_ref[0] = (jnp.concatenate(tiles, axis=1) + pen_lanes).astype(BF16)


def _moba_gate_call(neg_shift, qb, kmean):
    b, s, _ = qb.shape
    nb = kmean.shape[1]
    return pl.pallas_call(
        _moba_gate_kernel,
        out_shape=jax.ShapeDtypeStruct((b, s, H_MOBA * LANES), BF16),
        grid=(b, s // TG),
        in_specs=[pl.BlockSpec(memory_space=pltpu.SMEM),
                  pl.BlockSpec((1, TG, MOBA_WIDTH), lambda bi, i: (bi, i, 0)),
                  pl.BlockSpec((1, nb, MOBA_WIDTH), lambda bi, i: (bi, 0, 0))],
        out_specs=pl.BlockSpec((1, TG, H_MOBA * LANES), lambda bi, i: (bi, i, 0)),
        compiler_params=pltpu.CompilerParams(dimension_semantics=("arbitrary", "arbitrary"),
                                             vmem_limit_bytes=VMEM_LIMIT),
        name="moba_gate",
    )(neg_shift, qb, kmean)


def _flash_update(hh, s, v_ext, m_sc, acc_sc, first, keep, fixed_shift):
    if fixed_shift:
        acc_sc[hh] = acc_sc[hh] * keep + jnp.dot(jnp.exp2(s).astype(BF16), v_ext,
                                                 preferred_element_type=F32)
        return
    m_prev = jnp.where(first == 1, NEG_BIG, m_sc[hh])
    m_new = jnp.maximum(m_prev, jnp.max(s, axis=-1, keepdims=True))
    alpha = jnp.exp2(m_prev - m_new)
    p = jnp.concatenate([jnp.exp2(s[:, LANES * c:LANES * (c + 1)] - m_new)
                         for c in range(s.shape[1] // LANES)], axis=1).astype(BF16)
    pv = jnp.dot(p, v_ext, preferred_element_type=F32)
    acc_sc[hh] = jnp.concatenate([alpha, alpha], axis=1) * acc_sc[hh] + pv
    m_sc[hh] = m_new


def _with_ones(v):
    return jnp.concatenate([v, jnp.ones(v.shape, v.dtype)], axis=1)


def _flash_init(m_sc, acc_sc):
    m_sc[...] = jnp.full(m_sc.shape, NEG_BIG, F32)
    acc_sc[...] = jnp.zeros(acc_sc.shape, F32)


def _flash_finish(o_ref, rows, acc_sc):
    lo_half = lax.broadcasted_iota(jnp.int32, (acc_sc.shape[1], LANES), 1) < HEAD_DIM
    o = jnp.where(lo_half, acc_sc[0, :, :LANES] / acc_sc[0, :, LANES:],
                  acc_sc[1, :, :LANES] / acc_sc[1, :, LANES:])
    o_ref[0, rows, :] = o.astype(o_ref.dtype)


def _causal_steps(n_tiles):
    pairs = [(qi, u) for qi in range(n_tiles) for u in range(qi + 1)]
    return (jnp.asarray([p[0] for p in pairs], jnp.int32), jnp.asarray([p[1] for p in pairs], jnp.int32),
            jnp.asarray([float(p[1] > 0) for p in pairs], F32))


def _pipelined(n_steps, produce, consume, bufs):
    produce(0, bufs[0])
    iters = (n_steps - 1) // STEPS_PER_ITER

    def body(it, carry):
        for j in range(STEPS_PER_ITER):
            n = STEPS_PER_ITER * it + j
            produce(n + 1, bufs[(j + 1) % 2])
            consume(n, bufs[j % 2])
        return carry

    lax.fori_loop(0, iters, body, 0)
    for n in range(STEPS_PER_ITER * iters, n_steps):
        if n + 1 < n_steps:
            produce(n + 1, bufs[(n + 1) % 2])
        consume(n, bufs[n % 2])


def _mla_kernel(qtab_ref, utab_ref, keep_ref, q_ref, k_ref, v_ref, o_ref,
                m_sc, acc_sc, sa_sc, sb_sc, mask_sc, *, fixed_shift, n_steps):
    t = TQ_MLA
    _flash_init(m_sc, acc_sc)
    row = lax.broadcasted_iota(jnp.int32, (t, t), 0)
    col = lax.broadcasted_iota(jnp.int32, (t, t), 1)
    mask_sc[0] = jnp.zeros((t, t), F32)
    mask_sc[1] = jnp.where(col <= row, 0.0, NEG_BIG)

    def produce(n, buf):
        qi, u = qtab_ref[n], utab_ref[n]
        qrows = pl.ds(pl.multiple_of(qi * t, t), t)
        krows = pl.ds(pl.multiple_of(u * t, t), t)
        diag = (u == qi).astype(jnp.int32)
        for hh in range(2):
            sl = slice(LANES * hh, LANES * (hh + 1))
            buf[hh] = lax.dot_general(q_ref[0, qrows, sl], k_ref[0, krows, sl], _NT,
                                      preferred_element_type=F32) + mask_sc[diag]

    def consume(n, buf):
        qi, u = qtab_ref[n], utab_ref[n]
        v = _with_ones(v_ref[0, pl.ds(pl.multiple_of(u * t, t), t), :])
        first = (u == 0).astype(jnp.int32)
        for hh in range(2):
            _flash_update(hh, buf[hh], v, m_sc, acc_sc, first, keep_ref[n], fixed_shift)
        _flash_finish(o_ref, pl.ds(pl.multiple_of(qi * t, t), t), acc_sc)

    _pipelined(n_steps, produce, consume, (sa_sc, sb_sc))


def _mla_call(qm, km, vm, fixed_shift):
    b, s, _ = qm.shape
    t = TQ_MLA
    qtab, utab, keep = _causal_steps(s // t)
    seq = lambda w: pl.BlockSpec((1, s, w), lambda p, bi, *_: (bi, 0, p))
    return pl.pallas_call(
        functools.partial(_mla_kernel, fixed_shift=fixed_shift, n_steps=int(qtab.shape[0])),
        out_shape=jax.ShapeDtypeStruct((b, s, MLA_WIDTH), BF16),
        grid_spec=pltpu.PrefetchScalarGridSpec(
            num_scalar_prefetch=2,
            grid=(N_PAIRS, b),
            in_specs=[pl.BlockSpec(memory_space=pltpu.SMEM),
                      seq(2 * LANES), seq(2 * LANES), seq(LANES)],
            out_specs=seq(LANES),
            scratch_shapes=[pltpu.VMEM((2, t, LANES), F32), pltpu.VMEM((2, t, 2 * LANES), F32),
                            pltpu.VMEM((2, t, t), F32), pltpu.VMEM((2, t, t), F32),
                            pltpu.VMEM((2, t, t), F32)],
        ),
        compiler_params=pltpu.CompilerParams(dimension_semantics=("arbitrary", "arbitrary"),
                                             vmem_limit_bytes=VMEM_LIMIT),
        name="mla_attention",
    )(qtab, utab, keep, qm, km, vm)


def _moba_kernel(qtab_ref, utab_ref, keep_ref, qx_ref, k_ref, v_ref, bt_ref, o_ref,
                 m_sc, acc_sc, sa_sc, sb_sc, *, fixed_shift, n_steps):
    t = MOBA_BLOCK
    w = MOBA_TILE
    nsub = w // t
    _flash_init(m_sc, acc_sc)

    def produce(n, buf):
        qi, u = qtab_ref[n], utab_ref[n]
        qrows = pl.ds(pl.multiple_of(qi * w, w), w)
        krows = pl.ds(pl.multiple_of(u * w, w), w)
        for hh in range(2):
            sl = slice(LANES * hh, LANES * (hh + 1))
            s = lax.dot_general(qx_ref[0, qrows, sl], k_ref[0, krows, sl], _NT,
                                preferred_element_type=F32)
            for qs in range(nsub):
                for ks in range(nsub):
                    d = jnp.clip(nsub * (qi - u) + (qs - ks), 0, bt_ref.shape[2] - 1)
                    rows, cols = slice(t * qs, t * (qs + 1)), slice(t * ks, t * (ks + 1))
                    buf[hh, rows, cols] = s[rows, cols] + bt_ref[0, hh, d]

    def consume(n, buf):
        qi, u = qtab_ref[n], utab_ref[n]
        v = _with_ones(v_ref[0, pl.ds(pl.multiple_of(u * w, w), w), :])
        first = (u == 0).astype(jnp.int32)
        for hh in range(2):
            _flash_update(hh, buf[hh], v, m_sc, acc_sc, first, keep_ref[n], fixed_shift)
        _flash_finish(o_ref, pl.ds(pl.multiple_of(qi * w, w), w), acc_sc)

    _pipelined(n_steps, produce, consume, (sa_sc, sb_sc))


def _moba_call(qx, kb, vb, bias_tiles, fixed_shift):
    b, s, _ = vb.shape
    t = MOBA_TILE
    n_tiles = bias_tiles.shape[2]
    qtab, utab, keep = _causal_steps(s // t)
    seq = lambda w: pl.BlockSpec((1, s, w), lambda p, bi, *_: (bi, 0, p))
    return pl.pallas_call(
        functools.partial(_moba_kernel, fixed_shift=fixed_shift, n_steps=int(qtab.shape[0])),
        out_shape=jax.ShapeDtypeStruct((b, s, MOBA_WIDTH), BF16),
        grid_spec=pltpu.PrefetchScalarGridSpec(
            num_scalar_prefetch=2,
            grid=(N_PAIRS, b),
            in_specs=[pl.BlockSpec(memory_space=pltpu.SMEM),
                      seq(2 * LANES), seq(2 * LANES), seq(LANES),
                      pl.BlockSpec((1, 2, n_tiles, MOBA_BLOCK, MOBA_BLOCK),
                                   lambda p, bi, *_: (p, 0, 0, 0, 0))],
            out_specs=seq(LANES),
            scratch_shapes=[pltpu.VMEM((2, t, LANES), F32), pltpu.VMEM((2, t, 2 * LANES), F32),
                            pltpu.VMEM((2, t, t), F32), pltpu.VMEM((2, t, t), F32)],
        ),
        compiler_params=pltpu.CompilerParams(dimension_semantics=("arbitrary", "arbitrary"),
                                             vmem_limit_bytes=VMEM_LIMIT),
        name="moba_attention",
    )(qtab, utab, keep, qx, kb, vb, bias_tiles)


def _proj_out_kernel(om_ref, gm_ref, ob_ref, gb_ref, w_ref, x_ref, mod_ref, o_ref):
    am = (_silu(gm_ref[0].astype(F32)) * om_ref[0].astype(F32)).astype(BF16)
    ab = (_silu(gb_ref[0].astype(F32)) * ob_ref[0].astype(F32)).astype(BF16)
    y = (jnp.dot(am, w_ref[0:MLA_WIDTH, :], preferred_element_type=F32)
         + jnp.dot(ab, w_ref[MLA_WIDTH:, :], preferred_element_type=F32))
    o_ref[0] = x_ref[0] + mod_ref[0, 2:3, :] * y


def _proj_out_call(layer, o_mla, g_mla, o_moba, g_moba, w_out, x, mod3):
    b, s, d = x.shape
    tok = lambda w: pl.BlockSpec((1, TM_OUT, w), lambda bi, i: (bi, i, 0))
    return pl.pallas_call(
        _proj_out_kernel,
        out_shape=jax.ShapeDtypeStruct((b, s, d), F32),
        grid=(b, s // TM_OUT),
        in_specs=[tok(MLA_WIDTH), tok(MLA_WIDTH), tok(MOBA_WIDTH), tok(MOBA_WIDTH),
                  pl.BlockSpec((None,) + w_out.shape[1:], lambda bi, i: (layer, 0, 0)),
                  tok(d), pl.BlockSpec((1, 3, d), lambda bi, i: (bi, 0, 0))],
        out_specs=tok(d),
        compiler_params=pltpu.CompilerParams(dimension_semantics=("arbitrary", "arbitrary"),
                                             vmem_limit_bytes=VMEM_LIMIT),
        name="proj_out",
    )(o_mla, g_mla, o_moba, g_moba, w_out, x, mod3)


def _to_head_lanes(a):
    a = jnp.concatenate([a, jnp.zeros(a.shape[:-1] + (1,), a.dtype)], axis=-1)
    return a[..., _LANE_SRC]


def _layer_weights(w_in, w_uq, w_ukv, q_g, k_g, bq_g, bk_g):
    o0 = Q_LORA
    o1 = o0 + KV_LORA
    o2 = o1 + ROPE_DIM
    k_rope = w_in[:, o1:o2]
    kr128 = jnp.zeros((D_MODEL, LANES), w_in.dtype)
    kr128 = kr128.at[:, 0:HALF_ROPE].set(k_rope[:, :HALF_ROPE])
    kr128 = kr128.at[:, LANES // 2:LANES // 2 + HALF_ROPE].set(k_rope[:, HALF_ROPE:])
    win = jnp.concatenate([w_in[:, :o1], kr128, w_in[:, o2:]], axis=1).astype(BF16)
    wuq = _to_head_lanes(w_uq.reshape(Q_LORA, H_MLA, QK_DIM)).reshape(Q_LORA, H_MLA * LANES)
    ukv = w_ukv.reshape(KV_LORA, H_MLA, NOPE_DIM + HEAD_DIM)
    k_nope = jnp.concatenate([ukv[..., :NOPE_DIM], jnp.zeros((KV_LORA, H_MLA, ROPE_DIM), ukv.dtype)], -1)
    wuk = _to_head_lanes(k_nope).reshape(KV_LORA, H_MLA * LANES)
    wuv = ukv[..., NOPE_DIM:].reshape(KV_LORA, MLA_WIDTH)
    pair = lambda g: jnp.concatenate([g, g]).reshape(1, LANES)
    return (win, wuq.astype(BF16), wuk.astype(BF16), wuv.astype(BF16),
            _to_head_lanes(q_g * MLA_QSCALE).reshape(1, LANES), _to_head_lanes(k_g).reshape(1, LANES),
            pair(bq_g), pair(bk_g))


def kernel(x, c, positions, norm_g, w_ada, b_ada, w_in, mla_q_norm_g, mla_w_uq, mla_kv_norm_g,
           mla_w_ukv, mla_q_g, mla_k_g, moba_q_g, moba_k_g, w_out, rel_bias):
    b, s, d = x.shape
    depth = w_in.shape[0]
    nb = s // MOBA_BLOCK
    assert d == D_MODEL and s % TM == 0 and s % TQ_MLA == 0 and TM % MOBA_BLOCK == 0 and b <= 8
    assert s % TG == 0 and TG % MOBA_BLOCK == 0 and s % MOBA_TILE == 0
    assert nb <= LANES // 4 and nb & (nb - 1) == 0
    n_tiles = _check_far_tiles(s)

    c8 = jnp.pad(c, ((0, 8 - b), (0, 0)))
    mod = _mod_call(c8, w_ada, b_ada.reshape(depth, 1, 3 * d))[:, :b].reshape(depth, b, 3, d)

    inv_freq = ROPE_THETA ** (-jnp.arange(0, HALF_ROPE, dtype=F32) / HALF_ROPE)
    freq = jnp.zeros((LANES,), F32).at[0:HALF_ROPE].set(inv_freq)
    freq = freq.at[LANES // 2:LANES // 2 + HALF_ROPE].set(inv_freq).reshape(1, LANES)
    sign = jnp.zeros((LANES,), F32).at[0:HALF_ROPE].set(-1.0)
    sign = sign.at[LANES // 2:LANES // 2 + HALF_ROPE].set(1.0).reshape(1, LANES)
    rope_c, rope_s = _rope_call(positions.reshape(b * s, 1), freq, sign)

    bkt = _rel_bucket(jnp.arange(s, dtype=jnp.int32))
    thr = jnp.sum(bkt[None, :] < jnp.arange(N_BUCKETS, dtype=jnp.int32)[:, None], axis=1).astype(jnp.int32)
    ttab, vtab = _far_tables(thr, rel_bias, n_tiles)
    bias_tiles = _bias_tile_call(thr, ttab, rel_bias.astype(F32).reshape(-1), vtab, n_tiles)
    bias_tiles = bias_tiles.reshape(N_PAIRS, 2, n_tiles, MOBA_BLOCK, MOBA_BLOCK)
    kpad = jnp.zeros((1, LANES), F32).at[0, SHIFT_LANE].set(1.0)
    amax = lambda a: jnp.max(jnp.abs(a.astype(F32)))

    def attention(fixed_shift, qm, km, vm, qx, kb, vb):
        return (_mla_call(qm, km, vm, fixed_shift),
                _moba_call(qx, kb, vb, bias_tiles, fixed_shift))

    layer_weights = jax.vmap(_layer_weights)(w_in, mla_w_uq, mla_w_ukv, mla_q_g, mla_k_g,
                                             moba_q_g, moba_k_g)
    w_out_bf16 = w_out.astype(BF16)
    for l in range(depth):
        win, wuq, wuk, wuv = layer_weights[:4]
        qg, kg, bqg, bkg = (w[l] for w in layer_weights[4:])
        bound_mla = QK_DIM * amax(mla_q_g[l]) * amax(mla_k_g[l]) * MLA_QSCALE * SHIFT_SLACK
        bound_moba = (HEAD_DIM * amax(moba_q_g[l]) * amax(moba_k_g[l]) * MOBA_QSCALE
                      + amax(rel_bias) * LOG2E) * SHIFT_SLACK
        fixed_ok = (bound_mla <= MAX_SHIFT) & (bound_moba <= MAX_SHIFT)
        qpad = jnp.zeros((1, LANES), F32).at[0, SHIFT_LANE].set(jnp.where(fixed_ok, -bound_mla, 0.0))
        qm, km, vm, gm, qb, kb, vb, gb, kmean = _proj_in_call(
            l, x, mod[l], norm_g[l].reshape(1, d), win, mla_q_norm_g[l].reshape(1, Q_LORA), wuq,
            mla_kv_norm_g[l].reshape(1, KV_LORA), wuk, wuv, qg, kg, bqg, bkg, rope_c, rope_s,
            qpad, kpad)
        qx = _moba_gate_call(jnp.where(fixed_ok, -bound_moba, 0.0).reshape(1), qb,
                             kmean.reshape(b, nb, MOBA_WIDTH))
        o_mla, o_moba = lax.cond(
            fixed_ok, functools.partial(attention, True), functools.partial(attention, False),
            qm, km, vm, qx, kb, vb)
        x = _proj_out_call(l, o_mla, gm, o_moba, gb, w_out_bf16, x, mod[l])
    return x
```
